```python
import functools
import jax, jax.numpy as jnp
from jax import lax
import numpy as np

D_MODEL = 2048
BATCH = 2
SEQ = 16384
DEPTH = 1
DEC_BATCH = 32
DEC_SEQ = 16
PAST_LEN = 2048

CHUNK = 64
SG_CHUNK = 128
SG_GROUPS = 16
SG_WIDTH = D_MODEL
SG_GROUP_DIM = SG_WIDTH // SG_GROUPS
N_HEADS = 16
Q_LORA = 512
KV_LORA = 512
QK_NOPE = 128
QK_ROPE = 64
V_HEAD = 128
QK_HEAD = QK_NOPE + QK_ROPE
ROPE_BASE = 10000.0
ATTN_SCALE = QK_HEAD ** -0.5
Q_BLOCK = 128
MLP_HIDDEN = 4 * D_MODEL
EPS = 1e-6
OFF_Q = 2 * SG_WIDTH
OFF_KV = OFF_Q + Q_LORA
OFF_GATE = OFF_KV + KV_LORA + QK_ROPE
IN_COLS = OFF_GATE + 2 * D_MODEL

kernel_name = 'hybrid_gmlp_mla_adaln_stream_step'


def rmsnorm(x, g):
    xf = x.astype(jnp.float32)
    y = xf * lax.rsqrt(jnp.mean(xf * xf, axis=-1, keepdims=True) + EPS)
    return (y * g.astype(jnp.float32)).astype(x.dtype)


def rope_tables(pos):
    inv = jnp.float32(ROPE_BASE) ** (-jnp.arange(0, QK_ROPE, 2, dtype=jnp.float32) / QK_ROPE)
    ang = pos.astype(jnp.float32)[:, None] * inv[None, :]
    return jnp.cos(ang), jnp.sin(ang)


def apply_rope(x, cos, sin):
    half = QK_ROPE // 2
    xf = x.astype(jnp.float32)
    x1, x2 = xf[..., :half], xf[..., half:]
    return jnp.concatenate([x1 * cos - x2 * sin, x2 * cos + x1 * sin], axis=-1).astype(x.dtype)


def ada_params(c, w_ada, b_ada):
    mod = jax.nn.silu(c) @ w_ada + b_ada
    return jnp.split(mod[:, None, :], 6, axis=-1)


def sg_inputs(z_uv, g_sg):
    a = jax.nn.gelu(z_uv, approximate=False)
    return a[..., :SG_WIDTH], rmsnorm(a[..., SG_WIDTH:], g_sg)


def spatial_gate(u, v, w_s, b_s):
    B, N, T, _ = v.shape
    w = (w_s * jnp.tril(jnp.ones((SG_CHUNK, SG_CHUNK), w_s.dtype)))[:, :T, :T]
    vg = v.reshape(B, N, T, SG_GROUPS, SG_GROUP_DIM)
    mix = jnp.einsum('gij,bnjgc->bnigc', w, vg) + b_s[:, :T].T[None, None, :, :, None]
    return u * mix.reshape(B, N, T, SG_WIDTH)


def mla_project(z_q, z_kv, pos, p):
    B, S = z_q.shape[:2]
    cos, sin = rope_tables(pos)
    q = (rmsnorm(z_q, p['g_q_a']) @ p['w_uq']).reshape(B, S, N_HEADS, QK_HEAD)
    q_nope = rmsnorm(q[..., :QK_NOPE], p['g_q_nope'])
    q_rope = apply_rope(rmsnorm(q[..., QK_NOPE:], p['g_q_rope']), cos[:, None, :], sin[:, None, :])
    c_kv = rmsnorm(z_kv[..., :KV_LORA], p['g_kv_a'])
    k_rope = apply_rope(rmsnorm(z_kv[..., KV_LORA:], p['g_k_rope']), cos, sin)
    return q_nope, q_rope, c_kv, k_rope


def expand_keys(c_kv, p):
    return rmsnorm(jnp.einsum('bsc,chd->bshd', c_kv, p['w_uk']), p['g_k_nope'])


def attn_scores(q_nope, q_rope, k_nope, k_rope):
    s = jnp.einsum('bqhd,bkhd->bhqk', q_nope, k_nope, preferred_element_type=jnp.float32)
    s = s + jnp.einsum('bqhr,bkr->bhqk', q_rope, k_rope, preferred_element_type=jnp.float32)
    return s * ATTN_SCALE


def attn_values(prob, c_kv, p):
    o_lat = jnp.einsum('bhqk,bkc->bqhc', prob.astype(c_kv.dtype), c_kv)
    return jnp.einsum('bqhc,chd->bqhd', o_lat, p['w_uv'])


def mla_attend_prompt(q_nope, q_rope, c_kv, k_rope, p):
    B, S = c_kv.shape[:2]
    k_nope = expand_keys(c_kv, p)
    nblk = S // Q_BLOCK
    qn = q_nope.reshape(B, nblk, Q_BLOCK, N_HEADS, QK_NOPE).transpose(1, 0, 2, 3, 4)
    qr = q_rope.reshape(B, nblk, Q_BLOCK, N_HEADS, QK_ROPE).transpose(1, 0, 2, 3, 4)
    k_chunk = jnp.arange(S) // CHUNK

    def one_block(args):
        qn_b, qr_b, blk = args
        s = attn_scores(qn_b, qr_b, k_nope, k_rope)
        q_chunk = (blk * Q_BLOCK + jnp.arange(Q_BLOCK)) // CHUNK
        s = jnp.where(k_chunk[None, :] <= q_chunk[:, None], s, -jnp.inf)
        return attn_values(jax.nn.softmax(s, axis=-1), c_kv, p)

    o = lax.map(one_block, (qn, qr, jnp.arange(nblk)))
    return o.transpose(1, 0, 2, 3, 4).reshape(B, S, N_HEADS * V_HEAD)


def mla_attend_sample(q_nope, q_rope, c_kv_all, k_rope_all, p):
    B, T = q_nope.shape[:2]
    k_nope = expand_keys(c_kv_all, p)
    s = attn_scores(q_nope, q_rope, k_nope, k_rope_all)
    return attn_values(jax.nn.softmax(s, axis=-1), c_kv_all, p).reshape(B, T, N_HEADS * V_HEAD)


def merge(o_sg, o_mla, z_g, p):
    g = jax.nn.sigmoid(z_g)
    m = g[..., :D_MODEL] * (o_sg @ p['w_pa']) + g[..., D_MODEL:] * (o_mla @ p['w_pb'])
    return m @ p['w_o']


def mixer_prompt(h, p):
    B, S, _ = h.shape
    z = h @ p['w_in']
    u, v = sg_inputs(z[..., :OFF_Q], p['g_sg'])
    n = S // SG_CHUNK
    o_sg = spatial_gate(u.reshape(B, n, SG_CHUNK, SG_WIDTH), v.reshape(B, n, SG_CHUNK, SG_WIDTH),
                        p['w_s'], p['b_s']).reshape(B, S, SG_WIDTH)
    q_nope, q_rope, c_kv, k_rope = mla_project(z[..., OFF_Q:OFF_KV], z[..., OFF_KV:OFF_GATE], jnp.arange(S), p)
    o_mla = mla_attend_prompt(q_nope, q_rope, c_kv, k_rope, p)
    return merge(o_sg, o_mla, z[..., OFF_GATE:], p), c_kv, k_rope


def mixer_sample(h, cache_lat, cache_kr, p):
    B, T, _ = h.shape
    z = h @ p['w_in']
    u, v = sg_inputs(z[..., :OFF_Q], p['g_sg'])
    o_sg = spatial_gate(u[:, None], v[:, None], p['w_s'], p['b_s'])[:, 0]
    P = cache_lat.shape[1]
    q_nope, q_rope, c_kv, k_rope = mla_project(z[..., OFF_Q:OFF_KV], z[..., OFF_KV:OFF_GATE], P + jnp.arange(T), p)
    o_mla = mla_attend_sample(q_nope, q_rope, jnp.concatenate([cache_lat, c_kv], axis=1),
                              jnp.concatenate([cache_kr, k_rope], axis=1), p)
    return merge(o_sg, o_mla, z[..., OFF_GATE:], p), c_kv, k_rope, v


def trunk_layer(x, c, mixer, p):
    sh1, sc1, g1, sh2, sc2, g2 = ada_params(c, p['w_ada'], p['b_ada'])
    mix_out, *state = mixer(rmsnorm(x, p['g_norm1']) * (1 + sc1) + sh1)
    x = x + g1 * mix_out
    h = rmsnorm(x, p['g_norm2']) * (1 + sc2) + sh2
    x = x + g2 * (jnp.square(jax.nn.relu(h @ p['w_up'])) @ p['w_down'])
    return x, state


def setup_inputs(seed: int = 0) -> dict:
    key = jax.random.key(seed)
    ks = iter(jax.random.split(key, 40))
    f32 = jnp.float32

    def nrm(shape, scale=1.0):
        return jax.random.normal(next(ks), shape, f32) * scale

    def gain(n):
        return 1.0 + nrm((DEPTH, n), 0.02)

    return {
        'x_prompt': nrm((BATCH, SEQ, D_MODEL)),
        'x_sample': nrm((DEC_BATCH, DEC_SEQ, D_MODEL)),
        'cache_kv_latent': nrm((DEPTH, DEC_BATCH, PAST_LEN, KV_LORA)),
        'cache_k_rope': nrm((DEPTH, DEC_BATCH, PAST_LEN, QK_ROPE)),
        'c_prompt': nrm((BATCH, D_MODEL)),
        'c_sample': nrm((DEC_BATCH, D_MODEL)),
        'w_ada': nrm((DEPTH, D_MODEL, 6 * D_MODEL), 0.5 * D_MODEL ** -0.5),
        'b_ada': nrm((DEPTH, 6 * D_MODEL), 0.01),
        'g_norm1': gain(D_MODEL),
        'g_norm2': gain(D_MODEL),
        'w_in': nrm((DEPTH, D_MODEL, IN_COLS), D_MODEL ** -0.5),
        'g_sg': gain(SG_WIDTH),
        'w_s': nrm((DEPTH, SG_GROUPS, SG_CHUNK, SG_CHUNK), SG_CHUNK ** -0.5),
        'b_s': 1.0 + nrm((DEPTH, SG_GROUPS, SG_CHUNK), 0.1),
        'g_q_a': gain(Q_LORA),
        'w_uq': nrm((DEPTH, Q_LORA, N_HEADS * QK_HEAD), Q_LORA ** -0.5),
        'g_q_nope': gain(QK_NOPE),
        'g_q_rope': gain(QK_ROPE),
        'g_kv_a': gain(KV_LORA),
        'g_k_rope': gain(QK_ROPE),
        'w_uk': nrm((DEPTH, KV_LORA, N_HEADS, QK_NOPE), KV_LORA ** -0.5),
        'g_k_nope': gain(QK_NOPE),
        'w_uv': nrm((DEPTH, KV_LORA, N_HEADS, V_HEAD), KV_LORA ** -0.5),
        'w_pa': nrm((DEPTH, SG_WIDTH, D_MODEL), SG_WIDTH ** -0.5),
        'w_pb': nrm((DEPTH, N_HEADS * V_HEAD, D_MODEL), (N_HEADS * V_HEAD) ** -0.5),
        'w_o': nrm((DEPTH, D_MODEL, D_MODEL), D_MODEL ** -0.5),
        'w_up': nrm((DEPTH, D_MODEL, MLP_HIDDEN), D_MODEL ** -0.5),
        'w_down': nrm((DEPTH, MLP_HIDDEN, D_MODEL), MLP_HIDDEN ** -0.5),
    }


def reference(x_prompt, x_sample, cache_kv_latent, cache_k_rope, c_prompt, c_sample,
              w_ada, b_ada, g_norm1, g_norm2, w_in, g_sg, w_s, b_s,
              g_q_a, w_uq, g_q_nope, g_q_rope, g_kv_a, g_k_rope, w_uk, g_k_nope, w_uv,
              w_pa, w_pb, w_o, w_up, w_down):
    y_p, y_s = x_prompt, x_sample
    lat_p, kr_p, lat_s, kr_s, v_s = [], [], [], [], []
    for l in range(DEPTH):
        p = {'w_ada': w_ada[l], 'b_ada': b_ada[l], 'g_norm1': g_norm1[l], 'g_norm2': g_norm2[l],
             'w_in': w_in[l], 'g_sg': g_sg[l], 'w_s': w_s[l], 'b_s': b_s[l],
             'g_q_a': g_q_a[l], 'w_uq': w_uq[l], 'g_q_nope': g_q_nope[l], 'g_q_rope': g_q_rope[l],
             'g_kv_a': g_kv_a[l], 'g_k_rope': g_k_rope[l], 'w_uk': w_uk[l], 'g_k_nope': g_k_nope[l],
             'w_uv': w_uv[l], 'w_pa': w_pa[l], 'w_pb': w_pb[l], 'w_o': w_o[l],
             'w_up': w_up[l], 'w_down': w_down[l]}
        y_p, (lp, kp) = trunk_layer(y_p, c_prompt, functools.partial(mixer_prompt, p=p), p)
        y_s, (ls, ksm, vs) = trunk_layer(
            y_s, c_sample,
            functools.partial(mixer_sample, cache_lat=cache_kv_latent[l], cache_kr=cache_k_rope[l], p=p), p)
        lat_p.append(lp); kr_p.append(kp); lat_s.append(ls); kr_s.append(ksm); v_s.append(vs)
    return (y_p, y_s, jnp.stack(lat_p), jnp.stack(kr_p), jnp.stack(lat_s), jnp.stack(kr_s), jnp.stack(v_s))
```

```python
import functools
import math

import jax
import jax.numpy as jnp
import numpy as np
from jax import lax
from jax.experimental import pallas as pl
from jax.experimental.pallas import tpu as pltpu

F32 = jnp.float32
BF16 = jnp.bfloat16

EPS = 1e-6
ROPE_BASE = 10000.0
N_HEADS = 16
CHUNK = 64
SG_CHUNK = 128
SG_GROUPS = 16
LANES = 128
HEAD_PAD = 256
LOG2E = math.log2(math.e)

VMEM_LIMIT = 56 * 1024 * 1024


def _params(*sem):
    return pltpu.CompilerParams(dimension_semantics=sem, vmem_limit_bytes=VMEM_LIMIT)


def _rms(x, g):
    ms = jnp.mean(x * x, axis=-1, keepdims=True)
    return x * lax.rsqrt(ms + EPS) * g


def _rms_rope_group(t, g_pad, n_rope):
    ms = jnp.sum(t * t, axis=-1, keepdims=True) * (1.0 / n_rope)
    return t * lax.rsqrt(ms + EPS) * g_pad


def _rope_group(t, tc, ts1, ts2, half):
    x2_at_x1 = pltpu.roll(t, LANES - half, 1)
    x1_at_x2 = pltpu.roll(t, half, 1)
    return t * tc + x2_at_x1 * ts1 + x1_at_x2 * ts2


def _gelu(z):
    return 0.5 * z * (1.0 + lax.erf(z * np.float32(math.sqrt(0.5))))


def _mod_spec(mod, ts):
    d = mod.shape[-1]
    if mod.shape[1] == 1:
        return pl.BlockSpec((None, 1, d), lambda g, s, *_: (g, 0, 0))
    return pl.BlockSpec((None, ts, d), lambda g, s, *_: (g, s, 0))


def _ada_kernel(c_ref, w_ref, b_ref, o_ref):
    c = c_ref[...]
    s = (c * jax.nn.sigmoid(c)).astype(BF16)
    o_ref[...] = jnp.dot(s, w_ref[...].astype(BF16), preferred_element_type=F32) + b_ref[...]


def ada_project(c, w_ada, b_ada, tn=1024):
    r, d = c.shape
    n = w_ada.shape[1]
    return pl.pallas_call(
        _ada_kernel,
        grid=(n // tn,),
        in_specs=[pl.BlockSpec((r, d), lambda j: (0, 0)),
                  pl.BlockSpec((d, tn), lambda j: (0, j)),
                  pl.BlockSpec((1, tn), lambda j: (0, j))],
        out_specs=pl.BlockSpec((r, tn), lambda j: (0, j)),
        out_shape=jax.ShapeDtypeStruct((r, n), F32),
        compiler_params=_params("arbitrary"),
        name="ada_project",
    )(c, w_ada, b_ada.reshape(1, n))


def _prenorm_kernel(x_ref, g_ref, sc_ref, sh_ref, o_ref):
    y = _rms(x_ref[...], g_ref[...])
    o_ref[...] = (y * (1.0 + sc_ref[...]) + sh_ref[...]).astype(o_ref.dtype)


def prenorm(x, g, sc, sh, ts):
    G, S, D = x.shape
    return pl.pallas_call(
        _prenorm_kernel,
        grid=(G, S // ts),
        in_specs=[pl.BlockSpec((None, ts, D), lambda g_, s: (g_, s, 0)),
                  pl.BlockSpec((1, D), lambda g_, s: (0, 0)),
                  _mod_spec(sc, ts), _mod_spec(sh, ts)],
        out_specs=pl.BlockSpec((None, ts, D), lambda g_, s: (g_, s, 0)),
        out_shape=jax.ShapeDtypeStruct((G, S, D), BF16),
        compiler_params=_params("arbitrary", "arbitrary"),
        name="prenorm",
    )(x, g.reshape(1, D), sc, sh)


def _proj_act_kernel(h_ref, w_ref, o_ref, *, act):
    z = jnp.dot(h_ref[...], w_ref[...], preferred_element_type=F32)
    if act == "gelu":
        a = _gelu(z)
    else:
        a = jax.nn.sigmoid(z)
    o_ref[...] = a.astype(o_ref.dtype)


def proj_act(h, w, act, ts, tn):
    G, S, D = h.shape
    n = w.shape[1]
    return pl.pallas_call(
        functools.partial(_proj_act_kernel, act=act),
        grid=(G, S // ts, n // tn),
        in_specs=[pl.BlockSpec((None, ts, D), lambda g, s, j: (g, s, 0)),
                  pl.BlockSpec((D, tn), lambda g, s, j: (0, j))],
        out_specs=pl.BlockSpec((None, ts, tn), lambda g, s, j: (g, s, j)),
        out_shape=jax.ShapeDtypeStruct((G, S, n), BF16),
        compiler_params=_params("arbitrary", "arbitrary", "arbitrary"),
        name="proj_" + act,
    )(h, w)


def _proj_gelu_norm_kernel(h_ref, w_ref, g_ref, o_ref):
    z = jnp.dot(h_ref[...], w_ref[...], preferred_element_type=F32)
    o_ref[...] = _rms(_gelu(z), g_ref[...]).astype(o_ref.dtype)


def proj_gelu_norm(h, w, g, ts, out_dtype):
    G, S, D = h.shape
    n = w.shape[1]
    return pl.pallas_call(
        _proj_gelu_norm_kernel,
        grid=(G, S // ts),
        in_specs=[pl.BlockSpec((None, ts, D), lambda g_, s: (g_, s, 0)),
                  pl.BlockSpec((D, n), lambda g_, s: (0, 0)),
                  pl.BlockSpec((1, n), lambda g_, s: (0, 0))],
        out_specs=pl.BlockSpec((None, ts, n), lambda g_, s: (g_, s, 0)),
        out_shape=jax.ShapeDtypeStruct((G, S, n), out_dtype),
        compiler_params=_params("arbitrary", "arbitrary"),
        name="proj_gelu_norm",
    )(h, w, g.reshape(1, n))


def _q_proj_kernel(h_ref, wq_ref, gqa_ref, wuq_ref, gn_ref, gr_ref, tc_ref, ts1_ref, ts2_ref,
                   q_ref, *, n_rope, q_scale):
    zq = jnp.dot(h_ref[...], wq_ref[...], preferred_element_type=F32)
    zn = _rms(zq, gqa_ref[...]).astype(BF16)
    tc, ts1, ts2 = tc_ref[...], ts1_ref[...], ts2_ref[...]
    gn, gr = gn_ref[...], gr_ref[...]
    for hh in range(q_ref.shape[0]):
        blk = jnp.dot(zn, wuq_ref[:, hh * HEAD_PAD:(hh + 1) * HEAD_PAD], preferred_element_type=F32)
        nope = _rms(blk[:, :LANES], gn)
        rope = _rope_group(_rms_rope_group(blk[:, LANES:], gr, n_rope), tc, ts1, ts2, n_rope // 2)
        q_ref[hh, :, :LANES] = (nope * q_scale).astype(q_ref.dtype)
        q_ref[hh, :, LANES:] = (rope * q_scale).astype(q_ref.dtype)


def q_project(h, w_q, g_q_a, w_uq_pad, g_nope, g_rope_pad, tabs, ts, n_rope, q_scale):
    G, S, D = h.shape
    ql = w_q.shape[1]
    H = w_uq_pad.shape[1] // HEAD_PAD
    const = lambda g, s: (0, 0)
    tab_spec = pl.BlockSpec((ts, LANES), lambda g, s: (s, 0))
    return pl.pallas_call(
        functools.partial(_q_proj_kernel, n_rope=n_rope, q_scale=q_scale),
        grid=(G, S // ts),
        in_specs=[pl.BlockSpec((None, ts, D), lambda g, s: (g, s, 0)),
                  pl.BlockSpec((D, ql), const),
                  pl.BlockSpec((1, ql), const),
                  pl.BlockSpec((ql, H * HEAD_PAD), const),
                  pl.BlockSpec((1, LANES), const),
                  pl.BlockSpec((1, LANES), const),
                  tab_spec, tab_spec, tab_spec],
        out_specs=pl.BlockSpec((None, H, ts, HEAD_PAD), lambda g, s: (g, 0, s, 0)),
        out_shape=jax.ShapeDtypeStruct((G, H, S, HEAD_PAD), BF16),
        compiler_params=_params("arbitrary", "arbitrary"),
        name="q_project",
    )(h, w_q, g_q_a.reshape(1, ql), w_uq_pad, g_nope.reshape(1, LANES), g_rope_pad, *tabs)


def _kv_proj_kernel(h_ref, wkv_ref, gkva_ref, gkr_ref, tc_ref, ts1_ref, ts2_ref, *rest,
                    n_rope, expand):
    if expand:
        wuk_ref, wuv_ref, gkn_ref, c_ref, kr_ref, k_ref, v_ref = rest
    else:
        c_ref, kr_ref, krp_ref = rest
    lat = c_ref.shape[-1]
    z = jnp.dot(h_ref[...], wkv_ref[...], preferred_element_type=F32)
    c = _rms(z[:, :lat], gkva_ref[...])
    c_ref[...] = c
    rope = _rope_group(_rms_rope_group(z[:, lat:], gkr_ref[...], n_rope),
                       tc_ref[...], ts1_ref[...], ts2_ref[...], n_rope // 2)
    kr_ref[...] = rope[:, :n_rope]
    rope_b = rope.astype(BF16)
    if not expand:
        krp_ref[...] = rope_b
        return
    cb = c.astype(BF16)
    gkn = gkn_ref[...]
    for pair in range(k_ref.shape[0] // 2):
        cols = slice(pair * 2 * LANES, (pair + 1) * 2 * LANES)
        kn2 = jnp.dot(cb, wuk_ref[:, cols], preferred_element_type=F32)
        v2 = jnp.dot(cb, wuv_ref[:, cols], preferred_element_type=F32)
        for sub in range(2):
            hh = 2 * pair + sub
            lanes = slice(sub * LANES, (sub + 1) * LANES)
            k_ref[hh, :, :LANES] = _rms(kn2[:, lanes], gkn).astype(k_ref.dtype)
            k_ref[hh, :, LANES:] = rope_b
            v_ref[hh] = v2[:, lanes].astype(v_ref.dtype)


def kv_project(h, w_kv, g_kv_a, g_k_rope_pad, tabs, ts, n_rope, expand_weights=None):
    G, S, D = h.shape
    lat = g_kv_a.shape[-1]
    const = lambda g, s: (0, 0)
    tab_spec = pl.BlockSpec((ts, LANES), lambda g, s: (s, 0))
    in_specs = [pl.BlockSpec((None, ts, D), lambda g, s: (g, s, 0)),
                pl.BlockSpec((D, lat + LANES), const),
                pl.BlockSpec((1, lat), const),
                pl.BlockSpec((1, LANES), const),
                tab_spec, tab_spec, tab_spec]
    args = [h, w_kv, g_kv_a.reshape(1, lat), g_k_rope_pad, *tabs]
    out_specs = [pl.BlockSpec((None, ts, lat), lambda g, s: (g, s, 0)),
                 pl.BlockSpec((None, ts, n_rope), lambda g, s: (g, s, 0))]
    out_shape = [jax.ShapeDtypeStruct((G, S, lat), F32),
                 jax.ShapeDtypeStruct((G, S, n_rope), F32)]
    expand = expand_weights is not None
    if expand:
        w_uk, w_uv, g_k_nope = expand_weights
        H = w_uk.shape[1] // LANES
        in_specs += [pl.BlockSpec(w_uk.shape, const), pl.BlockSpec(w_uv.shape, const),
                     pl.BlockSpec((1, LANES), const)]
        args += [w_uk, w_uv, g_k_nope.reshape(1, LANES)]
        out_specs += [pl.BlockSpec((None, H, ts, HEAD_PAD), lambda g, s: (g, 0, s, 0)),
                      pl.BlockSpec((None, H, ts, LANES), lambda g, s: (g, 0, s, 0))]
        out_shape += [jax.ShapeDtypeStruct((G, H, S, HEAD_PAD), BF16),
                      jax.ShapeDtypeStruct((G, H, S, LANES), BF16)]
    else:
        out_specs.append(pl.BlockSpec((None, ts, LANES), lambda g, s: (g, s, 0)))
        out_shape.append(jax.ShapeDtypeStruct((G, S, LANES), BF16))
    return pl.pallas_call(
        functools.partial(_kv_proj_kernel, n_rope=n_rope, expand=expand),
        grid=(G, S // ts),
        in_specs=in_specs,
        out_specs=out_specs,
        out_shape=out_shape,
        compiler_params=_params("arbitrary", "arbitrary"),
        name="kv_project",
    )(*args)


def _spatial_gate_kernel(u_ref, v_ref, w_ref, b_ref, o_ref, *, n_chunks):
    row = lax.broadcasted_iota(jnp.int32, (SG_CHUNK, SG_CHUNK), 0)
    col = lax.broadcasted_iota(jnp.int32, (SG_CHUNK, SG_CHUNK), 1)
    tril = col <= row
    for g in range(w_ref.shape[0]):
        w = jnp.where(tril, w_ref[g], 0.0).astype(BF16)
        lanes = slice(g * LANES, (g + 1) * LANES)
        bias = b_ref[:, lanes]

        def chunk_body(n, carry, w=w, lanes=lanes, bias=bias):
            rows = pl.ds(pl.multiple_of(n * SG_CHUNK, SG_CHUNK), SG_CHUNK)
            mix = jnp.dot(w, v_ref[rows, lanes], preferred_element_type=F32) + bias
            o_ref[rows, lanes] = (u_ref[rows, lanes].astype(F32) * mix).astype(o_ref.dtype)
            return carry

        lax.fori_loop(0, n_chunks, chunk_body, 0)


def spatial_gate(u, v, w_s, bias_rows, ts):
    G, S, W = u.shape
    blk = pl.BlockSpec((None, ts, W), lambda g, s: (g, s, 0))
    return pl.pallas_call(
        functools.partial(_spatial_gate_kernel, n_chunks=ts // SG_CHUNK),
        grid=(G, S // ts),
        in_specs=[blk, blk,
                  pl.BlockSpec(w_s.shape, lambda g, s: (0, 0, 0)),
                  pl.BlockSpec(bias_rows.shape, lambda g, s: (0, 0))],
        out_specs=blk,
        out_shape=jax.ShapeDtypeStruct((G, S, W), BF16),
        compiler_params=_params("arbitrary", "arbitrary"),
        name="spatial_gate",
    )(u, v, w_s, bias_rows)


def _spatial_gate_open_kernel(u_ref, v_ref, wl_ref, b_ref, o_ref):
    T = u_ref.shape[1]
    for i in range(T):
        acc = b_ref[i:i + 1, :] + wl_ref[i, 0:1, :] * v_ref[:, 0, :]
        for j in range(1, i + 1):
            acc = acc + wl_ref[i, j:j + 1, :] * v_ref[:, j, :]
        o_ref[:, i, :] = (u_ref[:, i, :].astype(F32) * acc).astype(o_ref.dtype)


def spatial_gate_open(u, v, w_lanes, bias_rows):
    B, T, W = u.shape
    full = lambda a: pl.BlockSpec(a.shape, lambda i: (0,) * a.ndim)
    return pl.pallas_call(
        _spatial_gate_open_kernel,
        grid=(1,),
        in_specs=[full(u), full(v), full(w_lanes), full(bias_rows)],
        out_specs=pl.BlockSpec((B, T, W), lambda i: (0, 0, 0)),
        out_shape=jax.ShapeDtypeStruct((B, T, W), BF16),
        compiler_params=_params("arbitrary"),
        name="spatial_gate_open",
    )(u, v, w_lanes, bias_rows)


def _attn_kernel(q_ref, k_ref, v_ref, o_ref, m_sc, l_sc, acc_sc, *, tq):
    i = pl.program_id(2)
    q = q_ref[...]
    nt = (((1,), (1,)), ((), ()))

    d0 = pl.multiple_of(i * tq, tq)
    s = lax.dot_general(q, k_ref[pl.ds(d0, tq), :], nt, preferred_element_type=F32)
    row = lax.broadcasted_iota(jnp.int32, (tq, tq), 0) // CHUNK
    col = lax.broadcasted_iota(jnp.int32, (tq, tq), 1) // CHUNK
    s = jnp.where(col <= row, s, -jnp.inf)
    m = jnp.max(s, axis=-1, keepdims=True)
    p = jnp.exp2(s - m)
    m_sc[...] = m
    l_sc[...] = jnp.sum(p, axis=-1, keepdims=True)
    acc_sc[...] = jnp.dot(p.astype(BF16), v_ref[pl.ds(d0, tq), :], preferred_element_type=F32)

    def body(j, carry):
        k0 = pl.multiple_of(j * tq, tq)
        s = lax.dot_general(q, k_ref[pl.ds(k0, tq), :], nt, preferred_element_type=F32)
        m_old = m_sc[...]
        m_new = jnp.maximum(m_old, jnp.max(s, axis=-1, keepdims=True))
        alpha = jnp.exp2(m_old - m_new)
        p = jnp.exp2(s - m_new)
        m_sc[...] = m_new
        l_sc[...] = alpha * l_sc[...] + jnp.sum(p, axis=-1, keepdims=True)
        acc_sc[...] = alpha * acc_sc[...] + jnp.dot(p.astype(BF16), v_ref[pl.ds(k0, tq), :],
                                                    preferred_element_type=F32)
        return carry

    lax.fori_loop(0, i, body, 0)
    o_ref[...] = (acc_sc[...] / l_sc[...]).astype(o_ref.dtype)


def attention_prompt(q, k, v, tq):
    G, H, S, _ = q.shape
    dv = v.shape[-1]
    return pl.pallas_call(
        functools.partial(_attn_kernel, tq=tq),
        grid=(G, H, S // tq),
        in_specs=[pl.BlockSpec((None, None, tq, HEAD_PAD), lambda g, h, i: (g, h, i, 0)),
                  pl.BlockSpec((None, None, S, HEAD_PAD), lambda g, h, i: (g, h, 0, 0)),
                  pl.BlockSpec((None, None, S, dv), lambda g, h, i: (g, h, 0, 0))],
        out_specs=pl.BlockSpec((None, tq, dv), lambda g, h, i: (g, i, h)),
        out_shape=jax.ShapeDtypeStruct((G, S, H * dv), BF16),
        scratch_shapes=[pltpu.VMEM((tq, 1), F32), pltpu.VMEM((tq, 1), F32), pltpu.VMEM((tq, dv), F32)],
        compiler_params=_params("arbitrary", "arbitrary", "arbitrary"),
        name="attention_prompt",
    )(q, k, v)


def _attn_cached_kernel(q_ref, cache_ref, ckr_ref, cnew_ref, krnew_ref, wuk_ref, wuv_ref, gkn_ref,
                        o_ref, call_sc, krall_sc, p_sc, *, past, n_new):
    H = q_ref.shape[0]
    L = past + n_new
    Lp = call_sc.shape[0]
    lat = call_sc.shape[1]
    call_sc[0:past, :] = cache_ref[...].astype(BF16)
    call_sc[past:L, :] = cnew_ref[...].astype(BF16)
    call_sc[L:Lp, :] = jnp.zeros((Lp - L, lat), BF16)
    krall_sc[0:past, :] = ckr_ref[...].astype(BF16)
    krall_sc[past:L, :] = krnew_ref[...]
    krall_sc[L:Lp, :] = jnp.zeros((Lp - L, LANES), BF16)
    call = call_sc[...]
    krall = krall_sc[...]
    gkn = gkn_ref[...]
    valid = lax.broadcasted_iota(jnp.int32, (n_new, Lp), 1) < L
    nt = (((1,), (1,)), ((), ()))
    for pair in range(H // 2):
        cols = slice(pair * 2 * LANES, (pair + 1) * 2 * LANES)
        kn2 = jnp.dot(call, wuk_ref[:, cols], preferred_element_type=F32)
        for sub in range(2):
            hh = 2 * pair + sub
            kn = _rms(kn2[:, sub * LANES:(sub + 1) * LANES], gkn).astype(BF16)
            kh = jnp.concatenate([kn, krall], axis=1)
            s = lax.dot_general(q_ref[hh], kh, nt, preferred_element_type=F32)
            s = jnp.where(valid, s, -jnp.inf)
            m = jnp.max(s, axis=-1, keepdims=True)
            p = jnp.exp2(s - m)
            p = p / jnp.sum(p, axis=-1, keepdims=True)
            p_sc[hh * n_new:(hh + 1) * n_new, :] = p.astype(BF16)
    o_lat = jnp.dot(p_sc[...], call, preferred_element_type=F32).astype(BF16)
    for hh in range(H):
        o_ref[:, hh * LANES:(hh + 1) * LANES] = jnp.dot(
            o_lat[hh * n_new:(hh + 1) * n_new, :], wuv_ref[:, hh * LANES:(hh + 1) * LANES],
            preferred_element_type=F32).astype(o_ref.dtype)


def attention_cached(q, cache_lat, cache_kr_pad, c_new, kr_new_pad, w_uk, w_uv, g_k_nope, n_new):
    _, H, BT, _ = q.shape
    B, past, lat = cache_lat.shape
    L = past + n_new
    Lp = -(-L // LANES) * LANES
    const = lambda b: (0, 0)
    return pl.pallas_call(
        functools.partial(_attn_cached_kernel, past=past, n_new=n_new),
        grid=(B,),
        in_specs=[pl.BlockSpec((None, H, n_new, HEAD_PAD), lambda b: (0, 0, b, 0)),
                  pl.BlockSpec((None, past, lat), lambda b: (b, 0, 0)),
                  pl.BlockSpec((None, past, LANES), lambda b: (b, 0, 0)),
                  pl.BlockSpec((None, n_new, lat), lambda b: (0, b, 0)),
                  pl.BlockSpec((None, n_new, LANES), lambda b: (0, b, 0)),
                  pl.BlockSpec(w_uk.shape, const), pl.BlockSpec(w_uv.shape, const),
                  pl.BlockSpec((1, LANES), const)],
        out_specs=pl.BlockSpec((None, n_new, H * LANES), lambda b: (0, b, 0)),
        out_shape=jax.ShapeDtypeStruct((1, BT, H * LANES), BF16),
        scratch_shapes=[pltpu.VMEM((Lp, lat), BF16), pltpu.VMEM((Lp, LANES), BF16),
                        pltpu.VMEM((H * n_new, Lp), BF16)],
        compiler_params=_params("arbitrary"),
        name="attention_cached",
    )(q, cache_lat, cache_kr_pad, c_new, kr_new_pad, w_uk, w_uv, g_k_nope.reshape(1, LANES))


def _merge_kernel(a_ref, b_ref, ga_ref, gb_ref, wpa_ref, wpb_ref, o_ref):
    pa = jnp.dot(a_ref[...], wpa_ref[...], preferred_element_type=F32)
    pb = jnp.dot(b_ref[...], wpb_ref[...], preferred_element_type=F32)
    o_ref[...] = (ga_ref[...].astype(F32) * pa + gb_ref[...].astype(F32) * pb).astype(o_ref.dtype)


def merge_branches(o_sg, o_mla, gates, w_pa, w_pb, ts, tn):
    G, S, W = o_sg.shape
    D = w_pa.shape[1]
    nj = D // tn
    row = lambda g, s, j: (g, s, 0)
    return pl.pallas_call(
        _merge_kernel,
        grid=(G, S // ts, nj),
        in_specs=[pl.BlockSpec((None, ts, W), row),
                  pl.BlockSpec((None, ts, o_mla.shape[-1]), row),
                  pl.BlockSpec((None, ts, tn), lambda g, s, j: (g, s, j)),
                  pl.BlockSpec((None, ts, tn), lambda g, s, j: (g, s, j + nj)),
                  pl.BlockSpec((W, tn), lambda g, s, j: (0, j)),
                  pl.BlockSpec((o_mla.shape[-1], tn), lambda g, s, j: (0, j))],
        out_specs=pl.BlockSpec((None, ts, tn), lambda g, s, j: (g, s, j)),
        out_shape=jax.ShapeDtypeStruct((G, S, D), BF16),
        compiler_params=_params("arbitrary", "arbitrary", "arbitrary"),
        name="merge_branches",
    )(o_sg, o_mla, gates, gates, w_pa, w_pb)


def _out_proj_kernel(m_ref, wo_ref, x_ref, g1_ref, gn_ref, sc_ref, sh_ref, x1_ref, h2_ref):
    y = jnp.dot(m_ref[...], wo_ref[...], preferred_element_type=F32)
    x1 = x_ref[...] + g1_ref[...] * y
    x1_ref[...] = x1
    h2_ref[...] = (_rms(x1, gn_ref[...]) * (1.0 + sc_ref[...]) + sh_ref[...]).astype(h2_ref.dtype)


def out_project(m, w_o, x, g1, g_norm2, sc2, sh2, ts):
    G, S, D = x.shape
    row = pl.BlockSpec((None, ts, D), lambda g, s: (g, s, 0))
    return pl.pallas_call(
        _out_proj_kernel,
        grid=(G, S // ts),
        in_specs=[row, pl.BlockSpec(w_o.shape, lambda g, s: (0, 0)), row,
                  _mod_spec(g1, ts), pl.BlockSpec((1, D), lambda g, s: (0, 0)),
                  _mod_spec(sc2, ts), _mod_spec(sh2, ts)],
        out_specs=[row, row],
        out_shape=[jax.ShapeDtypeStruct((G, S, D), F32), jax.ShapeDtypeStruct((G, S, D), BF16)],
        compiler_params=_params("arbitrary", "arbitrary"),
        name="out_project",
    )(m, w_o, x, g1, g_norm2.reshape(1, D), sc2, sh2)


def _mlp_kernel(h_ref, wup_ref, wdn_ref, x1_ref, g2_ref, o_ref):
    j = pl.program_id(2)
    hid = jnp.dot(h_ref[...], wup_ref[...], preferred_element_type=F32)
    hid = jnp.square(jnp.maximum(hid, 0.0)).astype(BF16)
    part = jnp.dot(hid, wdn_ref[...], preferred_element_type=F32)

    @pl.when(j == 0)
    def _():
        o_ref[...] = part

    @pl.when(j > 0)
    def _():
        o_ref[...] += part

    @pl.when(j == pl.num_programs(2) - 1)
    def _():
        o_ref[...] = x1_ref[...] + g2_ref[...] * o_ref[...]


def mlp_residual(h2, w_up, w_down, x1, g2, ts, th):
    G, S, D = x1.shape
    hidden = w_up.shape[1]
    row = pl.BlockSpec((None, ts, D), lambda g, s, j: (g, s, 0))
    return pl.pallas_call(
        _mlp_kernel,
        grid=(G, S // ts, hidden // th),
        in_specs=[row,
                  pl.BlockSpec((D, th), lambda g, s, j: (0, j)),
                  pl.BlockSpec((th, D), lambda g, s, j: (j, 0)),
                  row, _mod_spec(g2, ts)],
        out_specs=row,
        out_shape=jax.ShapeDtypeStruct((G, S, D), F32),
        compiler_params=_params("arbitrary", "arbitrary", "arbitrary"),
        name="mlp_residual",
    )(h2, w_up, w_down, x1, g2)


def _rope_tables(pos, n_rope):
    half = n_rope // 2
    inv = jnp.float32(ROPE_BASE) ** (-jnp.arange(0, n_rope, 2, dtype=F32) / n_rope)
    ang = pos.astype(F32)[:, None] * inv[None, :]
    cos, sin = jnp.cos(ang), jnp.sin(ang)
    z = jnp.zeros_like(cos)
    pad = jnp.zeros((pos.shape[0], LANES - n_rope), F32)
    tc = jnp.concatenate([cos, cos, pad], axis=1)
    ts1 = jnp.concatenate([-sin, z, pad], axis=1)
    ts2 = jnp.concatenate([z, sin, pad], axis=1)
    return tc, ts1, ts2


def _pad_lanes(g, n):
    return jnp.concatenate([g.astype(F32), jnp.zeros((n - g.shape[0],), F32)]).reshape(1, n)


def _layer(x, mods, pos, P, ts, *, cache=None):
    sh1, sc1, g1, sh2, sc2, g2 = mods
    n_rope = P["n_rope"]
    tabs = _rope_tables(pos, n_rope)
    h = prenorm(x, P["g_norm1"], sc1, sh1, ts)
    u = proj_act(h, P["w_u"], "gelu", ts, P["w_u"].shape[1])
    gates = proj_act(h, P["w_g"], "sigmoid", ts, P["w_g"].shape[1] // 2)
    q = q_project(h, P["w_q"], P["g_q_a"], P["w_uq"], P["g_q_nope"], P["g_q_rope"], tabs, ts,
                  n_rope, P["q_scale"])
    if cache is None:
        v = proj_gelu_norm(h, P["w_v"], P["g_sg"], ts, BF16)
        o_sg = spatial_gate(u, v, P["w_s"], P["b_rows"], ts)
        c_kv, k_rope, k, vv = kv_project(h, P["w_kv"], P["g_kv_a"], P["g_k_rope"], tabs, ts, n_rope,
                                         expand_weights=(P["w_uk"], P["w_uv"], P["g_k_nope"]))
        o_mla = attention_prompt(q, k, vv, tq=min(512, x.shape[1]))
        extra = ()
    else:
        cache_lat, cache_kr, B, T = cache
        v = proj_gelu_norm(h, P["w_v"], P["g_sg"], ts, F32)
        W = v.shape[-1]
        o_sg = spatial_gate_open(u.reshape(B, T, W), v.reshape(B, T, W), P["w_lanes"][:T, :T],
                                 P["b_rows"][:T]).reshape(1, B * T, W)
        c_kv, k_rope, kr_pad = kv_project(h, P["w_kv"], P["g_kv_a"], P["g_k_rope"], tabs, ts, n_rope)
        cache_kr_pad = jnp.pad(cache_kr, ((0, 0), (0, 0), (0, LANES - n_rope)))
        o_mla = attention_cached(q, cache_lat, cache_kr_pad, c_kv, kr_pad, P["w_uk"], P["w_uv"],
                                 P["g_k_nope"], T)
        extra = (v,)
    m = merge_branches(o_sg, o_mla, gates, P["w_pa"], P["w_pb"], ts, P["w_pa"].shape[1] // 2)
    x1, h2 = out_project(m, P["w_o"], x, g1, P["g_norm2"], sc2, sh2, ts)
    y = mlp_residual(h2, P["w_up"], P["w_down"], x1, g2, ts, min(1024, P["w_up"].shape[1]))
    return (y, c_kv, k_rope) + extra


def kernel(x_prompt, x_sample, cache_kv_latent, cache_k_rope, c_prompt, c_sample, w_ada, b_ada, g_norm1, g_norm2, w_in, g_sg, w_s, b_s, g_q_a, w_uq, g_q_nope, g_q_rope, g_kv_a, g_k_rope, w_uk, g_k_nope, w_uv, w_pa, w_pb, w_o, w_up, w_down):
    depth = w_in.shape[0]
    Bp, S, D = x_prompt.shape
    Bs, T, _ = x_sample.shape
    past = cache_kv_latent.shape[2]
    q_lora = g_q_a.shape[-1]
    lat = g_kv_a.shape[-1]
    n_nope = g_q_nope.shape[-1]
    n_rope = g_q_rope.shape[-1]
    H = w_uk.shape[2]
    sg_w = g_sg.shape[-1]
    off_q = 2 * sg_w
    off_kv = off_q + q_lora
    off_gate = off_kv + lat + n_rope
    assert n_nope == LANES and w_uv.shape[-1] == LANES and n_rope <= LANES // 2
    assert sg_w // SG_GROUPS == LANES and H == N_HEADS

    nb = Bp + Bs
    nb_pad = -(-nb // 8) * 8
    c_all = jnp.concatenate([c_prompt, c_sample, jnp.zeros((nb_pad - nb, D), F32)], axis=0)

    y_p, y_s = x_prompt, x_sample.reshape(1, Bs * T, D)
    outs = [[] for _ in range(5)]
    for l in range(depth):
        wi = w_in[l].astype(BF16)
        w_uq_l = w_uq[l].astype(BF16).reshape(q_lora, H, n_nope + n_rope)
        w_uq_pad = jnp.pad(w_uq_l, ((0, 0), (0, 0), (0, HEAD_PAD - n_nope - n_rope)))
        P = {
            "n_rope": n_rope,
            "q_scale": float((n_nope + n_rope) ** -0.5 * LOG2E),
            "g_norm1": g_norm1[l], "g_norm2": g_norm2[l], "g_sg": g_sg[l],
            "w_u": wi[:, :sg_w], "w_v": wi[:, sg_w:off_q], "w_q": wi[:, off_q:off_kv],
            "w_kv": jnp.pad(wi[:, off_kv:off_gate], ((0, 0), (0, LANES - n_rope))),
            "w_g": wi[:, off_gate:],
            "g_q_a": g_q_a[l], "w_uq": w_uq_pad.reshape(q_lora, H * HEAD_PAD),
            "g_q_nope": g_q_nope[l], "g_q_rope": _pad_lanes(g_q_rope[l], LANES),
            "g_kv_a": g_kv_a[l], "g_k_rope": _pad_lanes(g_k_rope[l], LANES),
            "w_uk": w_uk[l].astype(BF16).reshape(lat, H * n_nope),
            "w_uv": w_uv[l].astype(BF16).reshape(lat, H * LANES),
            "g_k_nope": g_k_nope[l],
            "w_s": w_s[l],
            "b_rows": jnp.repeat(b_s[l].T, LANES, axis=1),
            "w_lanes": jnp.repeat(w_s[l][:, :T, :T].transpose(1, 2, 0), LANES, axis=2),
            "w_pa": w_pa[l].astype(BF16), "w_pb": w_pb[l].astype(BF16), "w_o": w_o[l].astype(BF16),
            "w_up": w_up[l].astype(BF16), "w_down": w_down[l].astype(BF16),
        }
        mod = ada_project(c_all, w_ada[l], b_ada[l])
        mods_p = [a.reshape(Bp, 1, D) for a in jnp.split(mod[:Bp], 6, axis=-1)]
        mods_s = [jnp.repeat(a, T, axis=0).reshape(1, Bs * T, D)
                  for a in jnp.split(mod[Bp:nb], 6, axis=-1)]

        ts_p = min(512, S)
        y_p, lp, kp = _layer(y_p, mods_p, jnp.arange(S), P, ts_p)
        pos_s = jnp.tile(past + jnp.arange(T), Bs)
        y_s, ls, ks, vs = _layer(y_s, mods_s, pos_s, P, Bs * T,
                                 cache=(cache_kv_latent[l], cache_k_rope[l], Bs, T))
        for lst, a in zip(outs, (lp, kp, ls.reshape(Bs, T, lat), ks.reshape(Bs, T, n_rope),
                                 vs.reshape(Bs, T, sg_w))):
            lst.append(a)
    return (y_p, y_s.reshape(Bs, T, D)) + tuple(jnp.stack(o) for o in outs)
```

```python
import functools
import math

import jax
import jax.numpy as jnp
import numpy as np
from jax import lax
from jax.experimental import pallas as pl
from jax.experimental.pallas import tpu as pltpu

F32 = jnp.float32
BF16 = jnp.bfloat16

EPS = 1e-6
ROPE_BASE = 10000.0
N_HEADS = 16
CHUNK = 64
SG_CHUNK = 128
SG_GROUPS = 16
LANES = 128
HEAD_PAD = 256
LOG2E = math.log2(math.e)

VMEM_LIMIT = 56 * 1024 * 1024


def _params(*sem):
    return pltpu.CompilerParams(dimension_semantics=sem, vmem_limit_bytes=VMEM_LIMIT)


def _rms(x, g):
    ms = jnp.mean(x * x, axis=-1, keepdims=True)
    return x * lax.rsqrt(ms + EPS) * g


def _rms_rope_group(t, g_pad, n_rope):
    ms = jnp.sum(t * t, axis=-1, keepdims=True) * (1.0 / n_rope)
    return t * lax.rsqrt(ms + EPS) * g_pad


def _rope_group(t, tc, ts1, ts2, half):
    x2_at_x1 = pltpu.roll(t, LANES - half, 1)
    x1_at_x2 = pltpu.roll(t, half, 1)
    return t * tc + x2_at_x1 * ts1 + x1_at_x2 * ts2


def _gelu(z):
    return 0.5 * z * (1.0 + lax.erf(z * np.float32(math.sqrt(0.5))))


def _mod_spec(mod, ts):
    d = mod.shape[-1]
    if mod.shape[1] == 1:
        return pl.BlockSpec((None, 1, d), lambda g, s, *_: (g, 0, 0))
    return pl.BlockSpec((None, ts, d), lambda g, s, *_: (g, s, 0))


def _ada_kernel(c_ref, w_ref, b_ref, o_ref):
    c = c_ref[...]
    s = (c * jax.nn.sigmoid(c)).astype(BF16)
    o_ref[...] = jnp.dot(s, w_ref[...].astype(BF16), preferred_element_type=F32) + b_ref[...]


def ada_project(c, w_ada, b_ada, tn=1024):
    r, d = c.shape
    n = w_ada.shape[1]
    return pl.pallas_call(
        _ada_kernel,
        grid=(n // tn,),
        in_specs=[pl.BlockSpec((r, d), lambda j: (0, 0)),
                  pl.BlockSpec((d, tn), lambda j: (0, j)),
                  pl.BlockSpec((1, tn), lambda j: (0, j))],
        out_specs=pl.BlockSpec((r, tn), lambda j: (0, j)),
        out_shape=jax.ShapeDtypeStruct((r, n), F32),
        compiler_params=_params("arbitrary"),
        name="ada_project",
    )(c, w_ada, b_ada.reshape(1, n))


def _prenorm_kernel(x_ref, g_ref, sc_ref, sh_ref, o_ref):
    y = _rms(x_ref[...], g_ref[...])
    o_ref[...] = (y * (1.0 + sc_ref[...]) + sh_ref[...]).astype(o_ref.dtype)


def prenorm(x, g, sc, sh, ts):
    G, S, D = x.shape
    return pl.pallas_call(
        _prenorm_kernel,
        grid=(G, S // ts),
        in_specs=[pl.BlockSpec((None, ts, D), lambda g_, s: (g_, s, 0)),
                  pl.BlockSpec((1, D), lambda g_, s: (0, 0)),
                  _mod_spec(sc, ts), _mod_spec(sh, ts)],
        out_specs=pl.BlockSpec((None, ts, D), lambda g_, s: (g_, s, 0)),
        out_shape=jax.ShapeDtypeStruct((G, S, D), BF16),
        compiler_params=_params("arbitrary", "arbitrary"),
        name="prenorm",
    )(x, g.reshape(1, D), sc, sh)


def _proj_act_kernel(h_ref, w_ref, o_ref, *, act):
    z = jnp.dot(h_ref[...], w_ref[...], preferred_element_type=F32)
    if act == "gelu":
        a = _gelu(z)
    else:
        a = jax.nn.sigmoid(z)
    o_ref[...] = a.astype(o_ref.dtype)


def proj_act(h, w, act, ts, tn):
    G, S, D = h.shape
    n = w.shape[1]
    return pl.pallas_call(
        functools.partial(_proj_act_kernel, act=act),
        grid=(G, S // ts, n // tn),
        in_specs=[pl.BlockSpec((None, ts, D), lambda g, s, j: (g, s, 0)),
                  pl.BlockSpec((D, tn), lambda g, s, j: (0, j))],
        out_specs=pl.BlockSpec((None, ts, tn), lambda g, s, j: (g, s, j)),
        out_shape=jax.ShapeDtypeStruct((G, S, n), BF16),
        compiler_params=_params("arbitrary", "arbitrary", "arbitrary"),
        name="proj_" + act,
    )(h, w)


def _proj_gelu_norm_kernel(h_ref, w_ref, g_ref, o_ref):
    z = jnp.dot(h_ref[...], w_ref[...], preferred_element_type=F32)
    o_ref[...] = _rms(_gelu(z), g_ref[...]).astype(o_ref.dtype)


def proj_gelu_norm(h, w, g, ts, out_dtype):
    G, S, D = h.shape
    n = w.shape[1]
    return pl.pallas_call(
        _proj_gelu_norm_kernel,
        grid=(G, S // ts),
        in_specs=[pl.BlockSpec((None, ts, D), lambda g_, s: (g_, s, 0)),
                  pl.BlockSpec((D, n), lambda g_, s: (0, 0)),
                  pl.BlockSpec((1, n), lambda g_, s: (0, 0))],
        out_specs=pl.BlockSpec((None, ts, n), lambda g_, s: (g_, s, 0)),
        out_shape=jax.ShapeDtypeStruct((G, S, n), out_dtype),
        compiler_params=_params("arbitrary", "arbitrary"),
        name="proj_gelu_norm",
    )(h, w, g.reshape(1, n))


def _q_proj_kernel(h_ref, wq_ref, gqa_ref, wuq_ref, gn_ref, gr_ref, tc_ref, ts1_ref, ts2_ref,
                   q_ref, *, n_rope, q_scale):
    zq = jnp.dot(h_ref[...], wq_ref[...], preferred_element_type=F32)
    zn = _rms(zq, gqa_ref[...]).astype(BF16)
    tc, ts1, ts2 = tc_ref[...], ts1_ref[...], ts2_ref[...]
    gn, gr = gn_ref[...], gr_ref[...]
    for hh in range(q_ref.shape[0]):
        blk = jnp.dot(zn, wuq_ref[:, hh * HEAD_PAD:(hh + 1) * HEAD_PAD], preferred_element_type=F32)
        nope = _rms(blk[:, :LANES], gn)
        rope = _rope_group(_rms_rope_group(blk[:, LANES:], gr, n_rope), tc, ts1, ts2, n_rope // 2)
        q_ref[hh, :, :LANES] = (nope * q_scale).astype(q_ref.dtype)
        q_ref[hh, :, LANES:] = (rope * q_scale).astype(q_ref.dtype)


def q_project(h, w_q, g_q_a, w_uq_pad, g_nope, g_rope_pad, tabs, ts, n_rope, q_scale):
    G, S, D = h.shape
    ql = w_q.shape[1]
    H = w_uq_pad.shape[1] // HEAD_PAD
    const = lambda g, s: (0, 0)
    tab_spec = pl.BlockSpec((ts, LANES), lambda g, s: (s, 0))
    return pl.pallas_call(
        functools.partial(_q_proj_kernel, n_rope=n_rope, q_scale=q_scale),
        grid=(G, S // ts),
        in_specs=[pl.BlockSpec((None, ts, D), lambda g, s: (g, s, 0)),
                  pl.BlockSpec((D, ql), const),
                  pl.BlockSpec((1, ql), const),
                  pl.BlockSpec((ql, H * HEAD_PAD), const),
                  pl.BlockSpec((1, LANES), const),
                  pl.BlockSpec((1, LANES), const),
                  tab_spec, tab_spec, tab_spec],
        out_specs=pl.BlockSpec((None, H, ts, HEAD_PAD), lambda g, s: (g, 0, s, 0)),
        out_shape=jax.ShapeDtypeStruct((G, H, S, HEAD_PAD), BF16),
        compiler_params=_params("arbitrary", "arbitrary"),
        name="q_project",
    )(h, w_q, g_q_a.reshape(1, ql), w_uq_pad, g_nope.reshape(1, LANES), g_rope_pad, *tabs)


def _kv_proj_kernel(h_ref, wkv_ref, gkva_ref, gkr_ref, tc_ref, ts1_ref, ts2_ref, *rest,
                    n_rope, expand):
    if expand:
        wuk_ref, wuv_ref, gkn_ref, c_ref, kr_ref, k_ref, v_ref = rest
    else:
        c_ref, kr_ref, krp_ref = rest
    lat = c_ref.shape[-1]
    z = jnp.dot(h_ref[...], wkv_ref[...], preferred_element_type=F32)
    c = _rms(z[:, :lat], gkva_ref[...])
    c_ref[...] = c
    rope = _rope_group(_rms_rope_group(z[:, lat:], gkr_ref[...], n_rope),
                       tc_ref[...], ts1_ref[...], ts2_ref[...], n_rope // 2)
    kr_ref[...] = rope[:, :n_rope]
    rope_b = rope.astype(BF16)
    if not expand:
        krp_ref[...] = rope_b
        return
    cb = c.astype(BF16)
    gkn = gkn_ref[...]
    for pair in range(k_ref.shape[0] // 2):
        cols = slice(pair * 2 * LANES, (pair + 1) * 2 * LANES)
        kn2 = jnp.dot(cb, wuk_ref[:, cols], preferred_element_type=F32)
        v2 = jnp.dot(cb, wuv_ref[:, cols], preferred_element_type=F32)
        for sub in range(2):
            hh = 2 * pair + sub
            lanes = slice(sub * LANES, (sub + 1) * LANES)
            k_ref[hh, :, :LANES] = _rms(kn2[:, lanes], gkn).astype(k_ref.dtype)
            k_ref[hh, :, LANES:] = rope_b
            v_ref[hh] = v2[:, lanes].astype(v_ref.dtype)


def kv_project(h, w_kv, g_kv_a, g_k_rope_pad, tabs, ts, n_rope, expand_weights=None):
    G, S, D = h.shape
    lat = g_kv_a.shape[-1]
    const = lambda g, s: (0, 0)
    tab_spec = pl.BlockSpec((ts, LANES), lambda g, s: (s, 0))
    in_specs = [pl.BlockSpec((None, ts, D), lambda g, s: (g, s, 0)),
                pl.BlockSpec((D, lat + LANES), const),
                pl.BlockSpec((1, lat), const),
                pl.BlockSpec((1, LANES), const),
                tab_spec, tab_spec, tab_spec]
    args = [h, w_kv, g_kv_a.reshape(1, lat), g_k_rope_pad, *tabs]
    out_specs = [pl.BlockSpec((None, ts, lat), lambda g, s: (g, s, 0)),
                 pl.BlockSpec((None, ts, n_rope), lambda g, s: (g, s, 0))]
    out_shape = [jax.ShapeDtypeStruct((G, S, lat), F32),
                 jax.ShapeDtypeStruct((G, S, n_rope), F32)]
    expand = expand_weights is not None
    if expand:
        w_uk, w_uv, g_k_nope = expand_weights
        H = w_uk.shape[1] // LANES
        in_specs += [pl.BlockSpec(w_uk.shape, const), pl.BlockSpec(w_uv.shape, const),
                     pl.BlockSpec((1, LANES), const)]
        args += [w_uk, w_uv, g_k_nope.reshape(1, LANES)]
        out_specs += [pl.BlockSpec((None, H, ts, HEAD_PAD), lambda g, s: (g, 0, s, 0)),
                      pl.BlockSpec((None, H, ts, LANES), lambda g, s: (g, 0, s, 0))]
        out_shape += [jax.ShapeDtypeStruct((G, H, S, HEAD_PAD), BF16),
                      jax.ShapeDtypeStruct((G, H, S, LANES), BF16)]
    else:
        out_specs.append(pl.BlockSpec((None, ts, LANES), lambda g, s: (g, s, 0)))
        out_shape.append(jax.ShapeDtypeStruct((G, S, LANES), BF16))
    return pl.pallas_call(
        functools.partial(_kv_proj_kernel, n_rope=n_rope, expand=expand),
        grid=(G, S // ts),
        in_specs=in_specs,
        out_specs=out_specs,
        out_shape=out_shape,
        compiler_params=_params("arbitrary", "arbitrary"),
        name="kv_project",
    )(*args)


def _spatial_gate_kernel(u_ref, v_ref, w_ref, b_ref, o_ref, *, n_chunks):
    row = lax.broadcasted_iota(jnp.int32, (SG_CHUNK, SG_CHUNK), 0)
    col = lax.broadcasted_iota(jnp.int32, (SG_CHUNK, SG_CHUNK), 1)
    tril = col <= row
    for g in range(w_ref.shape[0]):
        w = jnp.where(tril, w_ref[g], 0.0).astype(BF16)
        lanes = slice(g * LANES, (g + 1) * LANES)
        bias = b_ref[:, lanes]

        def chunk_body(n, carry, w=w, lanes=lanes, bias=bias):
            rows = pl.ds(pl.multiple_of(n * SG_CHUNK, SG_CHUNK), SG_CHUNK)
            mix = jnp.dot(w, v_ref[rows, lanes], preferred_element_type=F32) + bias
            o_ref[rows, lanes] = (u_ref[rows, lanes].astype(F32) * mix).astype(o_ref.dtype)
            return carry

        lax.fori_loop(0, n_chunks, chunk_body, 0)


def spatial_gate(u, v, w_s, bias_rows, ts):
    G, S, W = u.shape
    blk = pl.BlockSpec((None, ts, W), lambda g, s: (g, s, 0))
    return pl.pallas_call(
        functools.partial(_spatial_gate_kernel, n_chunks=ts // SG_CHUNK),
        grid=(G, S // ts),
        in_specs=[blk, blk,
                  pl.BlockSpec(w_s.shape, lambda g, s: (0, 0, 0)),
                  pl.BlockSpec(bias_rows.shape, lambda g, s: (0, 0))],
        out_specs=blk,
        out_shape=jax.ShapeDtypeStruct((G, S, W), BF16),
        compiler_params=_params("arbitrary", "arbitrary"),
        name="spatial_gate",
    )(u, v, w_s, bias_rows)


def _spatial_gate_open_kernel(u_ref, v_ref, wl_ref, b_ref, o_ref):
    T = u_ref.shape[1]
    for i in range(T):
        acc = b_ref[i:i + 1, :] + wl_ref[i, 0:1, :] * v_ref[:, 0, :]
        for j in range(1, i + 1):
            acc = acc + wl_ref[i, j:j + 1, :] * v_ref[:, j, :]
        o_ref[:, i, :] = (u_ref[:, i, :].astype(F32) * acc).astype(o_ref.dtype)


def spatial_gate_open(u, v, w_lanes, bias_rows):
    B, T, W = u.shape
    full = lambda a: pl.BlockSpec(a.shape, lambda i: (0,) * a.ndim)
    return pl.pallas_call(
        _spatial_gate_open_kernel,
        grid=(1,),
        in_specs=[full(u), full(v), full(w_lanes), full(bias_rows)],
        out_specs=pl.BlockSpec((B, T, W), lambda i: (0, 0, 0)),
        out_shape=jax.ShapeDtypeStruct((B, T, W), BF16),
        compiler_params=_params("arbitrary"),
        name="spatial_gate_open",
    )(u, v, w_lanes, bias_rows)


def _attn_kernel(q_ref, k_ref, v_ref, o_ref, m_sc, acc_sc, *, tq, tk):
    i = pl.program_id(2)
    q = q_ref[...]
    dv = v_ref.shape[-1]
    unroll = tq // tk
    nt = (((1,), (1,)), ((), ()))
    ones = jnp.ones((tk, LANES), BF16)

    def scores(k0):
        return lax.dot_general(q, k_ref[pl.ds(k0, tk), :], nt, preferred_element_type=F32)

    def step(s, k0, m_old, acc_old):
        row_max = jnp.max(s, axis=-1, keepdims=True)
        m_new = jnp.broadcast_to(row_max, (tq, LANES)) if m_old is None else jnp.maximum(m_old, row_max)
        p = jnp.concatenate([jnp.exp2(s[:, c * LANES:(c + 1) * LANES] - m_new)
                             for c in range(tk // LANES)], axis=1).astype(BF16)
        v1 = jnp.concatenate([v_ref[pl.ds(k0, tk), :], ones], axis=1)
        pv = jnp.dot(p, v1, preferred_element_type=F32)
        if m_old is None:
            return m_new, pv
        alpha = jnp.exp2(m_old - m_new)
        return m_new, jnp.concatenate([alpha] * ((dv + LANES) // LANES), axis=1) * acc_old + pv

    d0 = pl.multiple_of(i * tq, tq)
    row = lax.broadcasted_iota(jnp.int32, (tq, tk), 0) // CHUNK
    col = lax.broadcasted_iota(jnp.int32, (tq, tk), 1) // CHUNK
    m, acc = None, None
    for d in range(unroll):
        s = jnp.where(col + (d * tk) // CHUNK <= row, scores(d0 + d * tk), -jnp.inf)
        m, acc = step(s, d0 + d * tk, m, acc)
    m_sc[...] = m
    acc_sc[...] = acc

    def body(j, carry):
        k0 = pl.multiple_of(j * tq, tq)
        ss = [scores(k0 + u * tk) for u in range(unroll)]
        m, acc = m_sc[...], acc_sc[...]
        for u in range(unroll):
            m, acc = step(ss[u], k0 + u * tk, m, acc)
        m_sc[...] = m
        acc_sc[...] = acc
        return carry

    lax.fori_loop(0, i, body, 0)
    acc = acc_sc[...]
    o_ref[...] = (acc[:, :dv] / acc[:, dv:]).astype(o_ref.dtype)


def attention_prompt(q, k, v, tq, tk):
    G, H, S, _ = q.shape
    dv = v.shape[-1]
    assert dv == LANES and tq % tk == 0 and tk % CHUNK == 0
    return pl.pallas_call(
        functools.partial(_attn_kernel, tq=tq, tk=tk),
        grid=(G, H, S // tq),
        in_specs=[pl.BlockSpec((None, None, tq, HEAD_PAD), lambda g, h, i: (g, h, i, 0)),
                  pl.BlockSpec((None, None, S, HEAD_PAD), lambda g, h, i: (g, h, 0, 0)),
                  pl.BlockSpec((None, None, S, dv), lambda g, h, i: (g, h, 0, 0))],
        out_specs=pl.BlockSpec((None, tq, dv), lambda g, h, i: (g, i, h)),
        out_shape=jax.ShapeDtypeStruct((G, S, H * dv), BF16),
        scratch_shapes=[pltpu.VMEM((tq, LANES), F32), pltpu.VMEM((tq, dv + LANES), F32)],
        compiler_params=_params("arbitrary", "arbitrary", "arbitrary"),
        name="attention_prompt",
    )(q, k, v)


def _attn_cached_kernel(q_ref, cache_ref, ckr_ref, cnew_ref, krnew_ref, wuk_ref, wuv_ref, gkn_ref,
                        o_ref, call_sc, krall_sc, p_sc, *, past, n_new):
    H = q_ref.shape[0]
    L = past + n_new
    Lp = call_sc.shape[0]
    lat = call_sc.shape[1]
    call_sc[0:past, :] = cache_ref[...].astype(BF16)
    call_sc[past:L, :] = cnew_ref[...].astype(BF16)
    call_sc[L:Lp, :] = jnp.zeros((Lp - L, lat), BF16)
    krall_sc[0:past, :] = ckr_ref[...].astype(BF16)
    krall_sc[past:L, :] = krnew_ref[...]
    krall_sc[L:Lp, :] = jnp.zeros((Lp - L, LANES), BF16)
    call = call_sc[...]
    krall = krall_sc[...]
    gkn = gkn_ref[...]
    valid = lax.broadcasted_iota(jnp.int32, (n_new, Lp), 1) < L
    nt = (((1,), (1,)), ((), ()))
    for pair in range(H // 2):
        cols = slice(pair * 2 * LANES, (pair + 1) * 2 * LANES)
        kn2 = jnp.dot(call, wuk_ref[:, cols], preferred_element_type=F32)
        for sub in range(2):
            hh = 2 * pair + sub
            kn = _rms(kn2[:, sub * LANES:(sub + 1) * LANES], gkn).astype(BF16)
            kh = jnp.concatenate([kn, krall], axis=1)
            s = lax.dot_general(q_ref[hh], kh, nt, preferred_element_type=F32)
            s = jnp.where(valid, s, -jnp.inf)
            m = jnp.max(s, axis=-1, keepdims=True)
            p = jnp.exp2(s - m)
            p = p / jnp.sum(p, axis=-1, keepdims=True)
            p_sc[hh * n_new:(hh + 1) * n_new, :] = p.astype(BF16)
    o_lat = jnp.dot(p_sc[...], call, preferred_element_type=F32).astype(BF16)
    for hh in range(H):
        o_ref[:, hh * LANES:(hh + 1) * LANES] = jnp.dot(
            o_lat[hh * n_new:(hh + 1) * n_new, :], wuv_ref[:, hh * LANES:(hh + 1) * LANES],
            preferred_element_type=F32).astype(o_ref.dtype)


def attention_cached(q, cache_lat, cache_kr_pad, c_new, kr_new_pad, w_uk, w_uv, g_k_nope, n_new):
    _, H, BT, _ = q.shape
    B, past, lat = cache_lat.shape
    L = past + n_new
    Lp = -(-L // LANES) * LANES
    const = lambda b: (0, 0)
    return pl.pallas_call(
        functools.partial(_attn_cached_kernel, past=past, n_new=n_new),
        grid=(B,),
        in_specs=[pl.BlockSpec((None, H, n_new, HEAD_PAD), lambda b: (0, 0, b, 0)),
                  pl.BlockSpec((None, past, lat), lambda b: (b, 0, 0)),
                  pl.BlockSpec((None, past, LANES), lambda b: (b, 0, 0)),
                  pl.BlockSpec((None, n_new, lat), lambda b: (0, b, 0)),
                  pl.BlockSpec((None, n_new, LANES), lambda b: (0, b, 0)),
                  pl.BlockSpec(w_uk.shape, const), pl.BlockSpec(w_uv.shape, const),
                  pl.BlockSpec((1, LANES), const)],
        out_specs=pl.BlockSpec((None, n_new, H * LANES), lambda b: (0, b, 0)),
        out_shape=jax.ShapeDtypeStruct((1, BT, H * LANES), BF16),
        scratch_shapes=[pltpu.VMEM((Lp, lat), BF16), pltpu.VMEM((Lp, LANES), BF16),
                        pltpu.VMEM((H * n_new, Lp), BF16)],
        compiler_params=_params("arbitrary"),
        name="attention_cached",
    )(q, cache_lat, cache_kr_pad, c_new, kr_new_pad, w_uk, w_uv, g_k_nope.reshape(1, LANES))


def _merge_kernel(a_ref, b_ref, ga_ref, gb_ref, wpa_ref, wpb_ref, o_ref):
    pa = jnp.dot(a_ref[...], wpa_ref[...], preferred_element_type=F32)
    pb = jnp.dot(b_ref[...], wpb_ref[...], preferred_element_type=F32)
    o_ref[...] = (ga_ref[...].astype(F32) * pa + gb_ref[...].astype(F32) * pb).astype(o_ref.dtype)


def merge_branches(o_sg, o_mla, gates, w_pa, w_pb, ts, tn):
    G, S, W = o_sg.shape
    D = w_pa.shape[1]
    nj = D // tn
    row = lambda g, s, j: (g, s, 0)
    return pl.pallas_call(
        _merge_kernel,
        grid=(G, S // ts, nj),
        in_specs=[pl.BlockSpec((None, ts, W), row),
                  pl.BlockSpec((None, ts, o_mla.shape[-1]), row),
                  pl.BlockSpec((None, ts, tn), lambda g, s, j: (g, s, j)),
                  pl.BlockSpec((None, ts, tn), lambda g, s, j: (g, s, j + nj)),
                  pl.BlockSpec((W, tn), lambda g, s, j: (0, j)),
                  pl.BlockSpec((o_mla.shape[-1], tn), lambda g, s, j: (0, j))],
        out_specs=pl.BlockSpec((None, ts, tn), lambda g, s, j: (g, s, j)),
        out_shape=jax.ShapeDtypeStruct((G, S, D), BF16),
        compiler_params=_params("arbitrary", "arbitrary", "arbitrary"),
        name="merge_branches",
    )(o_sg, o_mla, gates, gates, w_pa, w_pb)


def _out_proj_kernel(m_ref, wo_ref, x_ref, g1_ref, gn_ref, sc_ref, sh_ref, x1_ref, h2_ref):
    y = jnp.dot(m_ref[...], wo_ref[...], preferred_element_type=F32)
    x1 = x_ref[...] + g1_ref[...] * y
    x1_ref[...] = x1
    h2_ref[...] = (_rms(x1, gn_ref[...]) * (1.0 + sc_ref[...]) + sh_ref[...]).astype(h2_ref.dtype)


def out_project(m, w_o, x, g1, g_norm2, sc2, sh2, ts):
    G, S, D = x.shape
    row = pl.BlockSpec((None, ts, D), lambda g, s: (g, s, 0))
    return pl.pallas_call(
        _out_proj_kernel,
        grid=(G, S // ts),
        in_specs=[row, pl.BlockSpec(w_o.shape, lambda g, s: (0, 0)), row,
                  _mod_spec(g1, ts), pl.BlockSpec((1, D), lambda g, s: (0, 0)),
                  _mod_spec(sc2, ts), _mod_spec(sh2, ts)],
        out_specs=[row, row],
        out_shape=[jax.ShapeDtypeStruct((G, S, D), F32), jax.ShapeDtypeStruct((G, S, D), BF16)],
        compiler_params=_params("arbitrary", "arbitrary"),
        name="out_project",
    )(m, w_o, x, g1, g_norm2.reshape(1, D), sc2, sh2)


def _mlp_kernel(h_ref, wup_ref, wdn_ref, x1_ref, g2_ref, o_ref):
    j = pl.program_id(2)
    hid = jnp.dot(h_ref[...], wup_ref[...], preferred_element_type=F32)
    hid = jnp.square(jnp.maximum(hid, 0.0)).astype(BF16)
    part = jnp.dot(hid, wdn_ref[...], preferred_element_type=F32)

    @pl.when(j == 0)
    def _():
        o_ref[...] = part

    @pl.when(j > 0)
    def _():
        o_ref[...] += part

    @pl.when(j == pl.num_programs(2) - 1)
    def _():
        o_ref[...] = x1_ref[...] + g2_ref[...] * o_ref[...]


def mlp_residual(h2, w_up, w_down, x1, g2, ts, th):
    G, S, D = x1.shape
    hidden = w_up.shape[1]
    row = pl.BlockSpec((None, ts, D), lambda g, s, j: (g, s, 0))
    return pl.pallas_call(
        _mlp_kernel,
        grid=(G, S // ts, hidden // th),
        in_specs=[row,
                  pl.BlockSpec((D, th), lambda g, s, j: (0, j)),
                  pl.BlockSpec((th, D), lambda g, s, j: (j, 0)),
                  row, _mod_spec(g2, ts)],
        out_specs=row,
        out_shape=jax.ShapeDtypeStruct((G, S, D), F32),
        compiler_params=_params("arbitrary", "arbitrary", "arbitrary"),
        name="mlp_residual",
    )(h2, w_up, w_down, x1, g2)


def _rope_tables(pos, n_rope):
    half = n_rope // 2
    inv = jnp.float32(ROPE_BASE) ** (-jnp.arange(0, n_rope, 2, dtype=F32) / n_rope)
    ang = pos.astype(F32)[:, None] * inv[None, :]
    cos, sin = jnp.cos(ang), jnp.sin(ang)
    z = jnp.zeros_like(cos)
    pad = jnp.zeros((pos.shape[0], LANES - n_rope), F32)
    tc = jnp.concatenate([cos, cos, pad], axis=1)
    ts1 = jnp.concatenate([-sin, z, pad], axis=1)
    ts2 = jnp.concatenate([z, sin, pad], axis=1)
    return tc, ts1, ts2


def _pad_lanes(g, n):
    return jnp.concatenate([g.astype(F32), jnp.zeros((n - g.shape[0],), F32)]).reshape(1, n)


def _layer(x, mods, pos, P, ts, *, cache=None):
    sh1, sc1, g1, sh2, sc2, g2 = mods
    n_rope = P["n_rope"]
    tabs = _rope_tables(pos, n_rope)
    h = prenorm(x, P["g_norm1"], sc1, sh1, ts)
    u = proj_act(h, P["w_u"], "gelu", ts, P["w_u"].shape[1])
    gates = proj_act(h, P["w_g"], "sigmoid", ts, P["w_g"].shape[1] // 2)
    q = q_project(h, P["w_q"], P["g_q_a"], P["w_uq"], P["g_q_nope"], P["g_q_rope"], tabs, ts,
                  n_rope, P["q_scale"])
    if cache is None:
        v = proj_gelu_norm(h, P["w_v"], P["g_sg"], ts, BF16)
        o_sg = spatial_gate(u, v, P["w_s"], P["b_rows"], ts)
        c_kv, k_rope, k, vv = kv_project(h, P["w_kv"], P["g_kv_a"], P["g_k_rope"], tabs, ts, n_rope,
                                         expand_weights=(P["w_uk"], P["w_uv"], P["g_k_nope"]))
        o_mla = attention_prompt(q, k, vv, tq=min(2048, x.shape[1]), tk=min(1024, x.shape[1]))
        extra = ()
    else:
        cache_lat, cache_kr, B, T = cache
        v = proj_gelu_norm(h, P["w_v"], P["g_sg"], ts, F32)
        W = v.shape[-1]
        o_sg = spatial_gate_open(u.reshape(B, T, W), v.reshape(B, T, W), P["w_lanes"][:T, :T],
                                 P["b_rows"][:T]).reshape(1, B * T, W)
        c_kv, k_rope, kr_pad = kv_project(h, P["w_kv"], P["g_kv_a"], P["g_k_rope"], tabs, ts, n_rope)
        cache_kr_pad = jnp.pad(cache_kr, ((0, 0), (0, 0), (0, LANES - n_rope)))
        o_mla = attention_cached(q, cache_lat, cache_kr_pad, c_kv, kr_pad, P["w_uk"], P["w_uv"],
                                 P["g_k_nope"], T)
        extra = (v,)
    m = merge_branches(o_sg, o_mla, gates, P["w_pa"], P["w_pb"], ts, P["w_pa"].shape[1] // 2)
    x1, h2 = out_project(m, P["w_o"], x, g1, P["g_norm2"], sc2, sh2, ts)
    y = mlp_residual(h2, P["w_up"], P["w_down"], x1, g2, ts, min(1024, P["w_up"].shape[1]))
    return (y, c_kv, k_rope) + extra


def kernel(x_prompt, x_sample, cache_kv_latent, cache_k_rope, c_prompt, c_sample, w_ada, b_ada, g_norm1, g_norm2, w_in, g_sg, w_s, b_s, g_q_a, w_uq, g_q_nope, g_q_rope, g_kv_a, g_k_rope, w_uk, g_k_nope, w_uv, w_pa, w_pb, w_o, w_up, w_down):
    depth = w_in.shape[0]
    Bp, S, D = x_prompt.shape
    Bs, T, _ = x_sample.shape
    past = cache_kv_latent.shape[2]
    q_lora = g_q_a.shape[-1]
    lat = g_kv_a.shape[-1]
    n_nope = g_q_nope.shape[-1]
    n_rope = g_q_rope.shape[-1]
    H = w_uk.shape[2]
    sg_w = g_sg.shape[-1]
    off_q = 2 * sg_w
    off_kv = off_q + q_lora
    off_gate = off_kv + lat + n_rope
    assert n_nope == LANES and w_uv.shape[-1] == LANES and n_rope <= LANES // 2
    assert sg_w // SG_GROUPS == LANES and H == N_HEADS

    nb = Bp + Bs
    nb_pad = -(-nb // 8) * 8
    c_all = jnp.concatenate([c_prompt, c_sample, jnp.zeros((nb_pad - nb, D), F32)], axis=0)

    y_p, y_s = x_prompt, x_sample.reshape(1, Bs * T, D)
    outs = [[] for _ in range(5)]
    for l in range(depth):
        wi = w_in[l].astype(BF16)
        w_uq_l = w_uq[l].astype(BF16).reshape(q_lora, H, n_nope + n_rope)
        w_uq_pad = jnp.pad(w_uq_l, ((0, 0), (0, 0), (0, HEAD_PAD - n_nope - n_rope)))
        P = {
            "n_rope": n_rope,
            "q_scale": float((n_nope + n_rope) ** -0.5 * LOG2E),
            "g_norm1": g_norm1[l], "g_norm2": g_norm2[l], "g_sg": g_sg[l],
            "w_u": wi[:, :sg_w], "w_v": wi[:, sg_w:off_q], "w_q": wi[:, off_q:off_kv],
            "w_kv": jnp.pad(wi[:, off_kv:off_gate], ((0, 0), (0, LANES - n_rope))),
            "w_g": wi[:, off_gate:],
            "g_q_a": g_q_a[l], "w_uq": w_uq_pad.reshape(q_lora, H * HEAD_PAD),
            "g_q_nope": g_q_nope[l], "g_q_rope": _pad_lanes(g_q_rope[l], LANES),
            "g_kv_a": g_kv_a[l], "g_k_rope": _pad_lanes(g_k_rope[l], LANES),
            "w_uk": w_uk[l].astype(BF16).reshape(lat, H * n_nope),
            "w_uv": w_uv[l].astype(BF16).reshape(lat, H * LANES),
            "g_k_nope": g_k_nope[l],
            "w_s": w_s[l],
            "b_rows": jnp.repeat(b_s[l].T, LANES, axis=1),
            "w_lanes": jnp.repeat(w_s[l][:, :T, :T].transpose(1, 2, 0), LANES, axis=2),
            "w_pa": w_pa[l].astype(BF16), "w_pb": w_pb[l].astype(BF16), "w_o": w_o[l].astype(BF16),
            "w_up": w_up[l].astype(BF16), "w_down": w_down[l].astype(BF16),
        }
        mod = ada_project(c_all, w_ada[l], b_ada[l])
        mods_p = [a.reshape(Bp, 1, D) for a in jnp.split(mod[:Bp], 6, axis=-1)]
        mods_s = [jnp.repeat(a, T, axis=0).reshape(1, Bs * T, D)
                  for a in jnp.split(mod[Bp:nb], 6, axis=-1)]

        ts_p = min(512, S)
        y_p, lp, kp = _layer(y_p, mods_p, jnp.arange(S), P, ts_p)
        pos_s = jnp.tile(past + jnp.arange(T), Bs)
        y_s, ls, ks, vs = _layer(y_s, mods_s, pos_s, P, Bs * T,
                                 cache=(cache_kv_latent[l], cache_k_rope[l], Bs, T))
        for lst, a in zip(outs, (lp, kp, ls.reshape(Bs, T, lat), ks.reshape(Bs, T, n_rope),
                                 vs.reshape(Bs, T, sg_w))):
            lst.append(a)
    return (y_p, y_s.reshape(Bs, T, D)) + tuple(jnp.stack(o) for o in outs)
```

```python
import functools
import math

import jax
import jax.numpy as jnp
import numpy as np
from jax import lax
from jax.experimental import pallas as pl
from jax.experimental.pallas import tpu as pltpu

F32 = jnp.float32
BF16 = jnp.bfloat16

EPS = 1e-6
ROPE_BASE = 10000.0
N_HEADS = 16
CHUNK = 64
SG_CHUNK = 128
SG_GROUPS = 16
LANES = 128
HEAD_PAD = 256
LOG2E = math.log2(math.e)

VMEM_LIMIT = 56 * 1024 * 1024


def _params(*sem):
    return pltpu.CompilerParams(dimension_semantics=sem, vmem_limit_bytes=VMEM_LIMIT)


def _rms(x, g):
    ms = jnp.mean(x * x, axis=-1, keepdims=True)
    return x * lax.rsqrt(ms + EPS) * g


def _rms_rope_group(t, g_pad, n_rope):
    ms = jnp.sum(t * t, axis=-1, keepdims=True) * (1.0 / n_rope)
    return t * lax.rsqrt(ms + EPS) * g_pad


def _rope_group(t, tc, ts1, ts2, half):
    x2_at_x1 = pltpu.roll(t, LANES - half, 1)
    x1_at_x2 = pltpu.roll(t, half, 1)
    return t * tc + x2_at_x1 * ts1 + x1_at_x2 * ts2


def _gelu(z):
    return 0.5 * z * (1.0 + lax.erf(z * np.float32(math.sqrt(0.5))))


def _mod_spec(mod, ts):
    d = mod.shape[-1]
    if mod.shape[1] == 1:
        return pl.BlockSpec((None, 1, d), lambda g, s, *_: (g, 0, 0))
    return pl.BlockSpec((None, ts, d), lambda g, s, *_: (g, s, 0))


def _ada_kernel(c_ref, w_ref, b_ref, o_ref):
    c = c_ref[...]
    s = (c * jax.nn.sigmoid(c)).astype(BF16)
    o_ref[...] = jnp.dot(s, w_ref[...].astype(BF16), preferred_element_type=F32) + b_ref[...]


def ada_project(c, w_ada, b_ada, tn=1024):
    r, d = c.shape
    n = w_ada.shape[1]
    return pl.pallas_call(
        _ada_kernel,
        grid=(n // tn,),
        in_specs=[pl.BlockSpec((r, d), lambda j: (0, 0)),
                  pl.BlockSpec((d, tn), lambda j: (0, j)),
                  pl.BlockSpec((1, tn), lambda j: (0, j))],
        out_specs=pl.BlockSpec((r, tn), lambda j: (0, j)),
        out_shape=jax.ShapeDtypeStruct((r, n), F32),
        compiler_params=_params("arbitrary"),
        name="ada_project",
    )(c, w_ada, b_ada.reshape(1, n))


def _prenorm_kernel(x_ref, g_ref, sc_ref, sh_ref, o_ref):
    y = _rms(x_ref[...], g_ref[...])
    o_ref[...] = (y * (1.0 + sc_ref[...]) + sh_ref[...]).astype(o_ref.dtype)


def prenorm(x, g, sc, sh, ts):
    G, S, D = x.shape
    return pl.pallas_call(
        _prenorm_kernel,
        grid=(G, S // ts),
        in_specs=[pl.BlockSpec((None, ts, D), lambda g_, s: (g_, s, 0)),
                  pl.BlockSpec((1, D), lambda g_, s: (0, 0)),
                  _mod_spec(sc, ts), _mod_spec(sh, ts)],
        out_specs=pl.BlockSpec((None, ts, D), lambda g_, s: (g_, s, 0)),
        out_shape=jax.ShapeDtypeStruct((G, S, D), BF16),
        compiler_params=_params("arbitrary", "arbitrary"),
        name="prenorm",
    )(x, g.reshape(1, D), sc, sh)


def _proj_act_kernel(h_ref, w_ref, o_ref, *, act):
    z = jnp.dot(h_ref[...], w_ref[...], preferred_element_type=F32)
    if act == "gelu":
        a = _gelu(z)
    else:
        a = jax.nn.sigmoid(z)
    o_ref[...] = a.astype(o_ref.dtype)


def proj_act(h, w, act, ts, tn):
    G, S, D = h.shape
    n = w.shape[1]
    return pl.pallas_call(
        functools.partial(_proj_act_kernel, act=act),
        grid=(G, S // ts, n // tn),
        in_specs=[pl.BlockSpec((None, ts, D), lambda g, s, j: (g, s, 0)),
                  pl.BlockSpec((D, tn), lambda g, s, j: (0, j))],
        out_specs=pl.BlockSpec((None, ts, tn), lambda g, s, j: (g, s, j)),
        out_shape=jax.ShapeDtypeStruct((G, S, n), BF16),
        compiler_params=_params("arbitrary", "arbitrary", "arbitrary"),
        name="proj_" + act,
    )(h, w)


def _proj_gelu_norm_kernel(h_ref, w_ref, g_ref, o_ref):
    z = jnp.dot(h_ref[...], w_ref[...], preferred_element_type=F32)
    o_ref[...] = _rms(_gelu(z), g_ref[...]).astype(o_ref.dtype)


def proj_gelu_norm(h, w, g, ts, out_dtype):
    G, S, D = h.shape
    n = w.shape[1]
    return pl.pallas_call(
        _proj_gelu_norm_kernel,
        grid=(G, S // ts),
        in_specs=[pl.BlockSpec((None, ts, D), lambda g_, s: (g_, s, 0)),
                  pl.BlockSpec((D, n), lambda g_, s: (0, 0)),
                  pl.BlockSpec((1, n), lambda g_, s: (0, 0))],
        out_specs=pl.BlockSpec((None, ts, n), lambda g_, s: (g_, s, 0)),
        out_shape=jax.ShapeDtypeStruct((G, S, n), out_dtype),
        compiler_params=_params("arbitrary", "arbitrary"),
        name="proj_gelu_norm",
    )(h, w, g.reshape(1, n))


def _q_proj_kernel(h_ref, wq_ref, gqa_ref, wuq_ref, gn_ref, gr_ref, tc_ref, ts1_ref, ts2_ref,
                   q_ref, *, n_rope, q_scale):
    zq = jnp.dot(h_ref[...], wq_ref[...], preferred_element_type=F32)
    zn = _rms(zq, gqa_ref[...]).astype(BF16)
    tc, ts1, ts2 = tc_ref[...], ts1_ref[...], ts2_ref[...]
    gn, gr = gn_ref[...], gr_ref[...]
    for hh in range(q_ref.shape[0]):
        blk = jnp.dot(zn, wuq_ref[:, hh * HEAD_PAD:(hh + 1) * HEAD_PAD], preferred_element_type=F32)
        nope = _rms(blk[:, :LANES], gn)
        rope = _rope_group(_rms_rope_group(blk[:, LANES:], gr, n_rope), tc, ts1, ts2, n_rope // 2)
        q_ref[hh, :, :LANES] = (nope * q_scale).astype(q_ref.dtype)
        q_ref[hh, :, LANES:] = (rope * q_scale).astype(q_ref.dtype)


def q_project(h, w_q, g_q_a, w_uq_pad, g_nope, g_rope_pad, tabs, ts, n_rope, q_scale):
    G, S, D = h.shape
    ql = w_q.shape[1]
    H = w_uq_pad.shape[1] // HEAD_PAD
    const = lambda g, s: (0, 0)
    tab_spec = pl.BlockSpec((ts, LANES), lambda g, s: (s, 0))
    return pl.pallas_call(
        functools.partial(_q_proj_kernel, n_rope=n_rope, q_scale=q_scale),
        grid=(G, S // ts),
        in_specs=[pl.BlockSpec((None, ts, D), lambda g, s: (g, s, 0)),
                  pl.BlockSpec((D, ql), const),
                  pl.BlockSpec((1, ql), const),
                  pl.BlockSpec((ql, H * HEAD_PAD), const),
                  pl.BlockSpec((1, LANES), const),
                  pl.BlockSpec((1, LANES), const),
                  tab_spec, tab_spec, tab_spec],
        out_specs=pl.BlockSpec((None, H, ts, HEAD_PAD), lambda g, s: (g, 0, s, 0)),
        out_shape=jax.ShapeDtypeStruct((G, H, S, HEAD_PAD), BF16),
        compiler_params=_params("arbitrary", "arbitrary"),
        name="q_project",
    )(h, w_q, g_q_a.reshape(1, ql), w_uq_pad, g_nope.reshape(1, LANES), g_rope_pad, *tabs)


def _kv_proj_kernel(h_ref, wkv_ref, gkva_ref, gkr_ref, tc_ref, ts1_ref, ts2_ref, *rest,
                    n_rope, expand):
    if expand:
        wuk_ref, wuv_ref, gkn_ref, c_ref, kr_ref, k_ref, v_ref = rest
    else:
        c_ref, kr_ref, krp_ref = rest
    lat = c_ref.shape[-1]
    z = jnp.dot(h_ref[...], wkv_ref[...], preferred_element_type=F32)
    c = _rms(z[:, :lat], gkva_ref[...])
    c_ref[...] = c
    rope = _rope_group(_rms_rope_group(z[:, lat:], gkr_ref[...], n_rope),
                       tc_ref[...], ts1_ref[...], ts2_ref[...], n_rope // 2)
    kr_ref[...] = rope[:, :n_rope]
    rope_b = rope.astype(BF16)
    if not expand:
        krp_ref[...] = rope_b
        return
    cb = c.astype(BF16)
    gkn = gkn_ref[...]
    for pair in range(k_ref.shape[0] // 2):
        cols = slice(pair * 2 * LANES, (pair + 1) * 2 * LANES)
        kn2 = jnp.dot(cb, wuk_ref[:, cols], preferred_element_type=F32)
        v2 = jnp.dot(cb, wuv_ref[:, cols], preferred_element_type=F32)
        for sub in range(2):
            hh = 2 * pair + sub
            lanes = slice(sub * LANES, (sub + 1) * LANES)
            k_ref[hh, :, :LANES] = _rms(kn2[:, lanes], gkn).astype(k_ref.dtype)
            k_ref[hh, :, LANES:] = rope_b
            v_ref[hh] = v2[:, lanes].astype(v_ref.dtype)


def kv_project(h, w_kv, g_kv_a, g_k_rope_pad, tabs, ts, n_rope, expand_weights=None):
    G, S, D = h.shape
    lat = g_kv_a.shape[-1]
    const = lambda g, s: (0, 0)
    tab_spec = pl.BlockSpec((ts, LANES), lambda g, s: (s, 0))
    in_specs = [pl.BlockSpec((None, ts, D), lambda g, s: (g, s, 0)),
                pl.BlockSpec((D, lat + LANES), const),
                pl.BlockSpec((1, lat), const),
                pl.BlockSpec((1, LANES), const),
                tab_spec, tab_spec, tab_spec]
    args = [h, w_kv, g_kv_a.reshape(1, lat), g_k_rope_pad, *tabs]
    out_specs = [pl.BlockSpec((None, ts, lat), lambda g, s: (g, s, 0)),
                 pl.BlockSpec((None, ts, n_rope), lambda g, s: (g, s, 0))]
    out_shape = [jax.ShapeDtypeStruct((G, S, lat), F32),
                 jax.ShapeDtypeStruct((G, S, n_rope), F32)]
    expand = expand_weights is not None
    if expand:
        w_uk, w_uv, g_k_nope = expand_weights
        H = w_uk.shape[1] // LANES
        in_specs += [pl.BlockSpec(w_uk.shape, const), pl.BlockSpec(w_uv.shape, const),
                     pl.BlockSpec((1, LANES), const)]
        args += [w_uk, w_uv, g_k_nope.reshape(1, LANES)]
        out_specs += [pl.BlockSpec((None, H, ts, HEAD_PAD), lambda g, s: (g, 0, s, 0)),
                      pl.BlockSpec((None, H, ts, LANES), lambda g, s: (g, 0, s, 0))]
        out_shape += [jax.ShapeDtypeStruct((G, H, S, HEAD_PAD), BF16),
                      jax.ShapeDtypeStruct((G, H, S, LANES), BF16)]
    else:
        out_specs.append(pl.BlockSpec((None, ts, LANES), lambda g, s: (g, s, 0)))
        out_shape.append(jax.ShapeDtypeStruct((G, S, LANES), BF16))
    return pl.pallas_call(
        functools.partial(_kv_proj_kernel, n_rope=n_rope, expand=expand),
        grid=(G, S // ts),
        in_specs=in_specs,
        out_specs=out_specs,
        out_shape=out_shape,
        compiler_params=_params("arbitrary", "arbitrary"),
        name="kv_project",
    )(*args)


def _spatial_gate_kernel(u_ref, v_ref, w_ref, b_ref, o_ref, wm_sc, *, n_chunks):
    row = lax.broadcasted_iota(jnp.int32, (SG_CHUNK, SG_CHUNK), 0)
    col = lax.broadcasted_iota(jnp.int32, (SG_CHUNK, SG_CHUNK), 1)
    tril = col <= row
    n_groups = w_ref.shape[0]
    for g in range(n_groups):
        wm_sc[g] = jnp.where(tril, w_ref[g], 0.0).astype(BF16)

    def chunk_body(n, carry):
        rows = pl.ds(pl.multiple_of(n * SG_CHUNK, SG_CHUNK), SG_CHUNK)
        for g in range(n_groups):
            lanes = slice(g * LANES, (g + 1) * LANES)
            mix = jnp.dot(wm_sc[g], v_ref[rows, lanes], preferred_element_type=F32) + b_ref[:, lanes]
            o_ref[rows, lanes] = (u_ref[rows, lanes].astype(F32) * mix).astype(o_ref.dtype)
        return carry

    lax.fori_loop(0, n_chunks, chunk_body, 0)


def spatial_gate(u, v, w_s, bias_rows, ts):
    G, S, W = u.shape
    blk = pl.BlockSpec((None, ts, W), lambda g, s: (g, s, 0))
    return pl.pallas_call(
        functools.partial(_spatial_gate_kernel, n_chunks=ts // SG_CHUNK),
        grid=(G, S // ts),
        in_specs=[blk, blk,
                  pl.BlockSpec(w_s.shape, lambda g, s: (0, 0, 0)),
                  pl.BlockSpec(bias_rows.shape, lambda g, s: (0, 0))],
        out_specs=blk,
        out_shape=jax.ShapeDtypeStruct((G, S, W), BF16),
        scratch_shapes=[pltpu.VMEM(w_s.shape, BF16)],
        compiler_params=_params("arbitrary", "arbitrary"),
        name="spatial_gate",
    )(u, v, w_s, bias_rows)


def _spatial_gate_open_kernel(u_ref, v_ref, wl_ref, b_ref, o_ref):
    T = u_ref.shape[1]
    for i in range(T):
        acc = b_ref[i:i + 1, :] + wl_ref[i, 0:1, :] * v_ref[:, 0, :]
        for j in range(1, i + 1):
            acc = acc + wl_ref[i, j:j + 1, :] * v_ref[:, j, :]
        o_ref[:, i, :] = (u_ref[:, i, :].astype(F32) * acc).astype(o_ref.dtype)


def spatial_gate_open(u, v, w_lanes, bias_rows):
    B, T, W = u.shape
    full = lambda a: pl.BlockSpec(a.shape, lambda i: (0,) * a.ndim)
    return pl.pallas_call(
        _spatial_gate_open_kernel,
        grid=(1,),
        in_specs=[full(u), full(v), full(w_lanes), full(bias_rows)],
        out_specs=pl.BlockSpec((B, T, W), lambda i: (0, 0, 0)),
        out_shape=jax.ShapeDtypeStruct((B, T, W), BF16),
        compiler_params=_params("arbitrary"),
        name="spatial_gate_open",
    )(u, v, w_lanes, bias_rows)


def _attn_kernel(q_ref, k_ref, v_ref, o_ref, m_sc, acc_sc, *, tq, tk, td):
    i = pl.program_id(2)
    q = q_ref[...]
    dv = v_ref.shape[-1]
    unroll = tq // tk
    nt = (((1,), (1,)), ((), ()))
    ones = jnp.ones((tk, LANES), BF16)

    def scores(k0):
        return lax.dot_general(q, k_ref[pl.ds(k0, tk), :], nt, preferred_element_type=F32)

    def step(s, k0, m_old, acc_old):
        rows, keys = s.shape
        row_max = jnp.max(s, axis=-1, keepdims=True)
        m_new = jnp.broadcast_to(row_max, (rows, LANES)) if m_old is None else jnp.maximum(m_old, row_max)
        p = jnp.concatenate([jnp.exp2(s[:, c * LANES:(c + 1) * LANES] - m_new)
                             for c in range(keys // LANES)], axis=1).astype(BF16)
        v1 = jnp.concatenate([v_ref[pl.ds(k0, keys), :], ones[:keys]], axis=1)
        pv = jnp.dot(p, v1, preferred_element_type=F32)
        if m_old is None:
            return m_new, pv
        alpha = jnp.exp2(m_old - m_new)
        return m_new, jnp.concatenate([alpha] * ((dv + LANES) // LANES), axis=1) * acc_old + pv

    d0 = pl.multiple_of(i * tq, tq)
    chunk_mask = (lax.broadcasted_iota(jnp.int32, (td, td), 1) // CHUNK
                  <= lax.broadcasted_iota(jnp.int32, (td, td), 0) // CHUNK)
    for c in range(tq // td):
        r0 = c * td
        s = lax.dot_general(q_ref[r0:, :], k_ref[pl.ds(d0 + r0, td), :], nt, preferred_element_type=F32)
        top = jnp.where(chunk_mask, s[:td], -jnp.inf)
        s = top if r0 + td == tq else jnp.concatenate([top, s[td:]], axis=0)
        if c == 0:
            m, acc = step(s, d0, None, None)
        else:
            m, acc = step(s, d0 + r0, m_sc[r0:, :], acc_sc[r0:, :])
        m_sc[r0:, :] = m
        acc_sc[r0:, :] = acc

    def body(j, carry):
        k0 = pl.multiple_of(j * tq, tq)
        ss = [scores(k0 + u * tk) for u in range(unroll)]
        m, acc = m_sc[...], acc_sc[...]
        for u in range(unroll):
            m, acc = step(ss[u], k0 + u * tk, m, acc)
        m_sc[...] = m
        acc_sc[...] = acc
        return carry

    lax.fori_loop(0, i, body, 0)
    acc = acc_sc[...]
    o_ref[...] = (acc[:, :dv] / acc[:, dv:]).astype(o_ref.dtype)


def attention_prompt(q, k, v, tq, tk, td):
    G, H, S, _ = q.shape
    dv = v.shape[-1]
    assert dv == LANES and tq % tk == 0 and tq % td == 0 and td % LANES == 0 and td <= tk
    return pl.pallas_call(
        functools.partial(_attn_kernel, tq=tq, tk=tk, td=td),
        grid=(G, H, S // tq),
        in_specs=[pl.BlockSpec((None, None, tq, HEAD_PAD), lambda g, h, i: (g, h, i, 0)),
                  pl.BlockSpec((None, None, S, HEAD_PAD), lambda g, h, i: (g, h, 0, 0)),
                  pl.BlockSpec((None, None, S, dv), lambda g, h, i: (g, h, 0, 0))],
        out_specs=pl.BlockSpec((None, tq, dv), lambda g, h, i: (g, i, h)),
        out_shape=jax.ShapeDtypeStruct((G, S, H * dv), BF16),
        scratch_shapes=[pltpu.VMEM((tq, LANES), F32), pltpu.VMEM((tq, dv + LANES), F32)],
        compiler_params=_params("arbitrary", "arbitrary", "arbitrary"),
        name="attention_prompt",
    )(q, k, v)


def _attn_cached_kernel(q_ref, cache_ref, ckr_ref, cnew_ref, krnew_ref, wuk_ref, wuv_ref, gkn_ref,
                        o_ref, call_sc, krall_sc, p_sc, *, past, n_new):
    H = q_ref.shape[0]
    L = past + n_new
    Lp = call_sc.shape[0]
    lat = call_sc.shape[1]
    call_sc[0:past, :] = cache_ref[...].astype(BF16)
    call_sc[past:L, :] = cnew_ref[...].astype(BF16)
    call_sc[L:Lp, :] = jnp.zeros((Lp - L, lat), BF16)
    krall_sc[0:past, :] = ckr_ref[...].astype(BF16)
    krall_sc[past:L, :] = krnew_ref[...]
    krall_sc[L:Lp, :] = jnp.zeros((Lp - L, LANES), BF16)
    call = call_sc[...]
    krall = krall_sc[...]
    gkn = gkn_ref[...]
    valid = lax.broadcasted_iota(jnp.int32, (n_new, Lp), 1) < L
    nt = (((1,), (1,)), ((), ()))
    for pair in range(H // 2):
        cols = slice(pair * 2 * LANES, (pair + 1) * 2 * LANES)
        kn2 = jnp.dot(call, wuk_ref[:, cols], preferred_element_type=F32)
        for sub in range(2):
            hh = 2 * pair + sub
            kn = _rms(kn2[:, sub * LANES:(sub + 1) * LANES], gkn).astype(BF16)
            kh = jnp.concatenate([kn, krall], axis=1)
            s = lax.dot_general(q_ref[hh], kh, nt, preferred_element_type=F32)
            s = jnp.where(valid, s, -jnp.inf)
            m = jnp.max(s, axis=-1, keepdims=True)
            p = jnp.exp2(s - m)
            p = p / jnp.sum(p, axis=-1, keepdims=True)
            p_sc[hh * n_new:(hh + 1) * n_new, :] = p.astype(BF16)
    o_lat = jnp.dot(p_sc[...], call, preferred_element_type=F32).astype(BF16)
    for hh in range(H):
        o_ref[:, hh * LANES:(hh + 1) * LANES] = jnp.dot(
            o_lat[hh * n_new:(hh + 1) * n_new, :], wuv_ref[:, hh * LANES:(hh + 1) * LANES],
            preferred_element_type=F32).astype(o_ref.dtype)


def attention_cached(q, cache_lat, cache_kr_pad, c_new, kr_new_pad, w_uk, w_uv, g_k_nope, n_new):
    _, H, BT, _ = q.shape
    B, past, lat = cache_lat.shape
    L = past + n_new
    Lp = -(-L // LANES) * LANES
    const = lambda b: (0, 0)
    return pl.pallas_call(
        functools.partial(_attn_cached_kernel, past=past, n_new=n_new),
        grid=(B,),
        in_specs=[pl.BlockSpec((None, H, n_new, HEAD_PAD), lambda b: (0, 0, b, 0)),
                  pl.BlockSpec((None, past, lat), lambda b: (b, 0, 0)),
                  pl.BlockSpec((None, past, LANES), lambda b: (b, 0, 0)),
                  pl.BlockSpec((None, n_new, lat), lambda b: (0, b, 0)),
                  pl.BlockSpec((None, n_new, LANES), lambda b: (0, b, 0)),
                  pl.BlockSpec(w_uk.shape, const), pl.BlockSpec(w_uv.shape, const),
                  pl.BlockSpec((1, LANES), const)],
        out_specs=pl.BlockSpec((None, n_new, H * LANES), lambda b: (0, b, 0)),
        out_shape=jax.ShapeDtypeStruct((1, BT, H * LANES), BF16),
        scratch_shapes=[pltpu.VMEM((Lp, lat), BF16), pltpu.VMEM((Lp, LANES), BF16),
                        pltpu.VMEM((H * n_new, Lp), BF16)],
        compiler_params=_params("arbitrary"),
        name="attention_cached",
    )(q, cache_lat, cache_kr_pad, c_new, kr_new_pad, w_uk, w_uv, g_k_nope.reshape(1, LANES))


def _merge_kernel(a_ref, b_ref, ga_ref, gb_ref, wpa_ref, wpb_ref, o_ref):
    pa = jnp.dot(a_ref[...], wpa_ref[...], preferred_element_type=F32)
    pb = jnp.dot(b_ref[...], wpb_ref[...], preferred_element_type=F32)
    o_ref[...] = (ga_ref[...].astype(F32) * pa + gb_ref[...].astype(F32) * pb).astype(o_ref.dtype)


def merge_branches(o_sg, o_mla, gates, w_pa, w_pb, ts, tn):
    G, S, W = o_sg.shape
    D = w_pa.shape[1]
    nj = D // tn
    row = lambda g, s, j: (g, s, 0)
    return pl.pallas_call(
        _merge_kernel,
        grid=(G, S // ts, nj),
        in_specs=[pl.BlockSpec((None, ts, W), row),
                  pl.BlockSpec((None, ts, o_mla.shape[-1]), row),
                  pl.BlockSpec((None, ts, tn), lambda g, s, j: (g, s, j)),
                  pl.BlockSpec((None, ts, tn), lambda g, s, j: (g, s, j + nj)),
                  pl.BlockSpec((W, tn), lambda g, s, j: (0, j)),
                  pl.BlockSpec((o_mla.shape[-1], tn), lambda g, s, j: (0, j))],
        out_specs=pl.BlockSpec((None, ts, tn), lambda g, s, j: (g, s, j)),
        out_shape=jax.ShapeDtypeStruct((G, S, D), BF16),
        compiler_params=_params("arbitrary", "arbitrary", "arbitrary"),
        name="merge_branches",
    )(o_sg, o_mla, gates, gates, w_pa, w_pb)


def _out_proj_kernel(m_ref, wo_ref, x_ref, g1_ref, gn_ref, sc_ref, sh_ref, x1_ref, h2_ref):
    y = jnp.dot(m_ref[...], wo_ref[...], preferred_element_type=F32)
    x1 = x_ref[...] + g1_ref[...] * y
    x1_ref[...] = x1
    h2_ref[...] = (_rms(x1, gn_ref[...]) * (1.0 + sc_ref[...]) + sh_ref[...]).astype(h2_ref.dtype)


def out_project(m, w_o, x, g1, g_norm2, sc2, sh2, ts):
    G, S, D = x.shape
    row = pl.BlockSpec((None, ts, D), lambda g, s: (g, s, 0))
    return pl.pallas_call(
        _out_proj_kernel,
        grid=(G, S // ts),
        in_specs=[row, pl.BlockSpec(w_o.shape, lambda g, s: (0, 0)), row,
                  _mod_spec(g1, ts), pl.BlockSpec((1, D), lambda g, s: (0, 0)),
                  _mod_spec(sc2, ts), _mod_spec(sh2, ts)],
        out_specs=[row, row],
        out_shape=[jax.ShapeDtypeStruct((G, S, D), F32), jax.ShapeDtypeStruct((G, S, D), BF16)],
        compiler_params=_params("arbitrary", "arbitrary"),
        name="out_project",
    )(m, w_o, x, g1, g_norm2.reshape(1, D), sc2, sh2)


def _mlp_kernel(h_ref, wup_ref, wdn_ref, x1_ref, g2_ref, o_ref):
    j = pl.program_id(2)
    hid = jnp.dot(h_ref[...], wup_ref[...], preferred_element_type=F32)
    hid = jnp.square(jnp.maximum(hid, 0.0)).astype(BF16)
    part = jnp.dot(hid, wdn_ref[...], preferred_element_type=F32)

    @pl.when(j == 0)
    def _():
        o_ref[...] = part

    @pl.when(j > 0)
    def _():
        o_ref[...] += part

    @pl.when(j == pl.num_programs(2) - 1)
    def _():
        o_ref[...] = x1_ref[...] + g2_ref[...] * o_ref[...]


def mlp_residual(h2, w_up, w_down, x1, g2, ts, th):
    G, S, D = x1.shape
    hidden = w_up.shape[1]
    row = pl.BlockSpec((None, ts, D), lambda g, s, j: (g, s, 0))
    return pl.pallas_call(
        _mlp_kernel,
        grid=(G, S // ts, hidden // th),
        in_specs=[row,
                  pl.BlockSpec((D, th), lambda g, s, j: (0, j)),
                  pl.BlockSpec((th, D), lambda g, s, j: (j, 0)),
                  row, _mod_spec(g2, ts)],
        out_specs=row,
        out_shape=jax.ShapeDtypeStruct((G, S, D), F32),
        compiler_params=_params("arbitrary", "arbitrary", "arbitrary"),
        name="mlp_residual",
    )(h2, w_up, w_down, x1, g2)


def _rope_tables(pos, n_rope):
    half = n_rope // 2
    inv = jnp.float32(ROPE_BASE) ** (-jnp.arange(0, n_rope, 2, dtype=F32) / n_rope)
    ang = pos.astype(F32)[:, None] * inv[None, :]
    cos, sin = jnp.cos(ang), jnp.sin(ang)
    z = jnp.zeros_like(cos)
    pad = jnp.zeros((pos.shape[0], LANES - n_rope), F32)
    tc = jnp.concatenate([cos, cos, pad], axis=1)
    ts1 = jnp.concatenate([-sin, z, pad], axis=1)
    ts2 = jnp.concatenate([z, sin, pad], axis=1)
    return tc, ts1, ts2


def _pad_lanes(g, n):
    return jnp.concatenate([g.astype(F32), jnp.zeros((n - g.shape[0],), F32)]).reshape(1, n)


def _layer(x, mods, pos, P, ts, *, cache=None):
    sh1, sc1, g1, sh2, sc2, g2 = mods
    n_rope = P["n_rope"]
    tabs = _rope_tables(pos, n_rope)
    h = prenorm(x, P["g_norm1"], sc1, sh1, ts)
    u = proj_act(h, P["w_u"], "gelu", ts, P["w_u"].shape[1])
    gates = proj_act(h, P["w_g"], "sigmoid", ts, P["w_g"].shape[1] // 2)
    q = q_project(h, P["w_q"], P["g_q_a"], P["w_uq"], P["g_q_nope"], P["g_q_rope"], tabs, ts,
                  n_rope, P["q_scale"])
    if cache is None:
        v = proj_gelu_norm(h, P["w_v"], P["g_sg"], ts, BF16)
        o_sg = spatial_gate(u, v, P["w_s"], P["b_rows"], min(1024, x.shape[1]))
        c_kv, k_rope, k, vv = kv_project(h, P["w_kv"], P["g_kv_a"], P["g_k_rope"], tabs, ts, n_rope,
                                         expand_weights=(P["w_uk"], P["w_uv"], P["g_k_nope"]))
        o_mla = attention_prompt(q, k, vv, tq=min(2048, x.shape[1]), tk=min(1024, x.shape[1]),
                                 td=min(512, x.shape[1]))
        extra = ()
    else:
        cache_lat, cache_kr, B, T = cache
        v = proj_gelu_norm(h, P["w_v"], P["g_sg"], ts, F32)
        W = v.shape[-1]
        o_sg = spatial_gate_open(u.reshape(B, T, W), v.reshape(B, T, W), P["w_lanes"][:T, :T],
                                 P["b_rows"][:T]).reshape(1, B * T, W)
        c_kv, k_rope, kr_pad = kv_project(h, P["w_kv"], P["g_kv_a"], P["g_k_rope"], tabs, ts, n_rope)
        cache_kr_pad = jnp.pad(cache_kr, ((0, 0), (0, 0), (0, LANES - n_rope)))
        o_mla = attention_cached(q, cache_lat, cache_kr_pad, c_kv, kr_pad, P["w_uk"], P["w_uv"],
                                 P["g_k_nope"], T)
        extra = (v,)
    m = merge_branches(o_sg, o_mla, gates, P["w_pa"], P["w_pb"], ts, P["w_pa"].shape[1] // 2)
    x1, h2 = out_project(m, P["w_o"], x, g1, P["g_norm2"], sc2, sh2, ts)
    y = mlp_residual(h2, P["w_up"], P["w_down"], x1, g2, ts, min(1024, P["w_up"].shape[1]))
    return (y, c_kv, k_rope) + extra


def kernel(x_prompt, x_sample, cache_kv_latent, cache_k_rope, c_prompt, c_sample, w_ada, b_ada, g_norm1, g_norm2, w_in, g_sg, w_s, b_s, g_q_a, w_uq, g_q_nope, g_q_rope, g_kv_a, g_k_rope, w_uk, g_k_nope, w_uv, w_pa, w_pb, w_o, w_up, w_down):
    depth = w_in.shape[0]
    Bp, S, D = x_prompt.shape
    Bs, T, _ = x_sample.shape
    past = cache_kv_latent.shape[2]
    q_lora = g_q_a.shape[-1]
    lat = g_kv_a.shape[-1]
    n_nope = g_q_nope.shape[-1]
    n_rope = g_q_rope.shape[-1]
    H = w_uk.shape[2]
    sg_w = g_sg.shape[-1]
    off_q = 2 * sg_w
    off_kv = off_q + q_lora
    off_gate = off_kv + lat + n_rope
    assert n_nope == LANES and w_uv.shape[-1] == LANES and n_rope <= LANES // 2
    assert sg_w // SG_GROUPS == LANES and H == N_HEADS

    nb = Bp + Bs
    nb_pad = -(-nb // 8) * 8
    c_all = jnp.concatenate([c_prompt, c_sample, jnp.zeros((nb_pad - nb, D), F32)], axis=0)

    y_p, y_s = x_prompt, x_sample.reshape(1, Bs * T, D)
    outs = [[] for _ in range(5)]
    for l in range(depth):
        wi = w_in[l].astype(BF16)
        w_uq_l = w_uq[l].astype(BF16).reshape(q_lora, H, n_nope + n_rope)
        w_uq_pad = jnp.pad(w_uq_l, ((0, 0), (0, 0), (0, HEAD_PAD - n_nope - n_rope)))
        P = {
            "n_rope": n_rope,
            "q_scale": float((n_nope + n_rope) ** -0.5 * LOG2E),
            "g_norm1": g_norm1[l], "g_norm2": g_norm2[l], "g_sg": g_sg[l],
            "w_u": wi[:, :sg_w], "w_v": wi[:, sg_w:off_q], "w_q": wi[:, off_q:off_kv],
            "w_kv": jnp.pad(wi[:, off_kv:off_gate], ((0, 0), (0, LANES - n_rope))),
            "w_g": wi[:, off_gate:],
            "g_q_a": g_q_a[l], "w_uq": w_uq_pad.reshape(q_lora, H * HEAD_PAD),
            "g_q_nope": g_q_nope[l], "g_q_rope": _pad_lanes(g_q_rope[l], LANES),
            "g_kv_a": g_kv_a[l], "g_k_rope": _pad_lanes(g_k_rope[l], LANES),
            "w_uk": w_uk[l].astype(BF16).reshape(lat, H * n_nope),
            "w_uv": w_uv[l].astype(BF16).reshape(lat, H * LANES),
            "g_k_nope": g_k_nope[l],
            "w_s": w_s[l],
            "b_rows": jnp.repeat(b_s[l].T, LANES, axis=1),
            "w_lanes": jnp.repeat(w_s[l][:, :T, :T].transpose(1, 2, 0), LANES, axis=2),
            "w_pa": w_pa[l].astype(BF16), "w_pb": w_pb[l].astype(BF16), "w_o": w_o[l].astype(BF16),
            "w_up": w_up[l].astype(BF16), "w_down": w_down[l].astype(BF16),
        }
        mod = ada_project(c_all, w_ada[l], b_ada[l])
        mods_p = [a.reshape(Bp, 1, D) for a in jnp.split(mod[:Bp], 6, axis=-1)]
        mods_s = [jnp.repeat(a, T, axis=0).reshape(1, Bs * T, D)
                  for a in jnp.split(mod[Bp:nb], 6, axis=-1)]

        ts_p = min(512, S)
        y_p, lp, kp = _layer(y_p, mods_p, jnp.arange(S), P, ts_p)
        pos_s = jnp.tile(past + jnp.arange(T), Bs)
        y_s, ls, ks, vs = _layer(y_s, mods_s, pos_s, P, Bs * T,
                                 cache=(cache_kv_latent[l], cache_k_rope[l], Bs, T))
        for lst, a in zip(outs, (lp, kp, ls.reshape(Bs, T, lat), ks.reshape(Bs, T, n_rope),
                                 vs.reshape(Bs, T, sg_w))):
            lst.append(a)
    return (y_p, y_s.reshape(Bs, T, D)) + tuple(jnp.stack(o) for o in outs)
```

```python
import functools
import math

import jax
import jax.numpy as jnp
import numpy as np
from jax import lax
from jax.experimental import pallas as pl
from jax.experimental.pallas import tpu as pltpu

F32 = jnp.float32
BF16 = jnp.bfloat16

EPS = 1e-6
ROPE_BASE = 10000.0
N_HEADS = 16
CHUNK = 64
SG_CHUNK = 128
SG_GROUPS = 16
LANES = 128
HEAD_PAD = 256
LOG2E = math.log2(math.e)

VMEM_LIMIT = 58 * 1024 * 1024


def _params(*sem):
    return pltpu.CompilerParams(dimension_semantics=sem, vmem_limit_bytes=VMEM_LIMIT)


def _rms(x, g):
    ms = jnp.mean(x * x, axis=-1, keepdims=True)
    return x * lax.rsqrt(ms + EPS) * g


def _rms_rope_group(t, g_dup):
    ms = jnp.mean(t * t, axis=-1, keepdims=True)
    return t * lax.rsqrt(ms + EPS) * g_dup


def _rope_group(t, tc, ts):
    return t * tc + pltpu.roll(t, LANES - LANES // 4, 1) * ts


def _gelu(z):
    return 0.5 * z * (1.0 + lax.erf(z * np.float32(math.sqrt(0.5))))


def _mod_spec(mod, ts):
    d = mod.shape[-1]
    if mod.shape[1] == 1:
        return pl.BlockSpec((None, 1, d), lambda g, s, *_: (g, 0, 0))
    return pl.BlockSpec((None, ts, d), lambda g, s, *_: (g, s, 0))


def _ada_kernel(c_ref, w_ref, b_ref, o_ref):
    c = c_ref[...]
    s = (c * jax.nn.sigmoid(c)).astype(BF16)
    o_ref[...] = jnp.dot(s, w_ref[...].astype(BF16), preferred_element_type=F32) + b_ref[...]


def ada_project(c, w_ada, b_ada, tn=1024):
    r, d = c.shape
    n = w_ada.shape[1]
    return pl.pallas_call(
        _ada_kernel,
        grid=(n // tn,),
        in_specs=[pl.BlockSpec((r, d), lambda j: (0, 0)),
                  pl.BlockSpec((d, tn), lambda j: (0, j)),
                  pl.BlockSpec((1, tn), lambda j: (0, j))],
        out_specs=pl.BlockSpec((r, tn), lambda j: (0, j)),
        out_shape=jax.ShapeDtypeStruct((r, n), F32),
        compiler_params=_params("arbitrary"),
        name="ada_project",
    )(c, w_ada, b_ada.reshape(1, n))


def _prenorm_kernel(x_ref, g_ref, sc_ref, sh_ref, o_ref):
    y = _rms(x_ref[...], g_ref[...])
    o_ref[...] = (y * (1.0 + sc_ref[...]) + sh_ref[...]).astype(o_ref.dtype)


def prenorm(x, g, sc, sh, ts):
    G, S, D = x.shape
    return pl.pallas_call(
        _prenorm_kernel,
        grid=(G, S // ts),
        in_specs=[pl.BlockSpec((None, ts, D), lambda g_, s: (g_, s, 0)),
                  pl.BlockSpec((1, D), lambda g_, s: (0, 0)),
                  _mod_spec(sc, ts), _mod_spec(sh, ts)],
        out_specs=pl.BlockSpec((None, ts, D), lambda g_, s: (g_, s, 0)),
        out_shape=jax.ShapeDtypeStruct((G, S, D), BF16),
        compiler_params=_params("arbitrary", "arbitrary"),
        name="prenorm",
    )(x, g.reshape(1, D), sc, sh)


def _proj_act_kernel(h_ref, w_ref, o_ref, *, act):
    z = jnp.dot(h_ref[...], w_ref[...], preferred_element_type=F32)
    if act == "gelu":
        a = _gelu(z)
    else:
        a = jax.nn.sigmoid(z)
    o_ref[...] = a.astype(o_ref.dtype)


def proj_act(h, w, act, ts, tn):
    G, S, D = h.shape
    n = w.shape[1]
    return pl.pallas_call(
        functools.partial(_proj_act_kernel, act=act),
        grid=(G, S // ts, n // tn),
        in_specs=[pl.BlockSpec((None, ts, D), lambda g, s, j: (g, s, 0)),
                  pl.BlockSpec((D, tn), lambda g, s, j: (0, j))],
        out_specs=pl.BlockSpec((None, ts, tn), lambda g, s, j: (g, s, j)),
        out_shape=jax.ShapeDtypeStruct((G, S, n), BF16),
        compiler_params=_params("arbitrary", "arbitrary", "arbitrary"),
        name="proj_" + act,
    )(h, w)


def _proj_gelu_norm_kernel(h_ref, w_ref, g_ref, o_ref):
    z = jnp.dot(h_ref[...], w_ref[...], preferred_element_type=F32)
    o_ref[...] = _rms(_gelu(z), g_ref[...]).astype(o_ref.dtype)


def proj_gelu_norm(h, w, g, ts, out_dtype):
    G, S, D = h.shape
    n = w.shape[1]
    return pl.pallas_call(
        _proj_gelu_norm_kernel,
        grid=(G, S // ts),
        in_specs=[pl.BlockSpec((None, ts, D), lambda g_, s: (g_, s, 0)),
                  pl.BlockSpec((D, n), lambda g_, s: (0, 0)),
                  pl.BlockSpec((1, n), lambda g_, s: (0, 0))],
        out_specs=pl.BlockSpec((None, ts, n), lambda g_, s: (g_, s, 0)),
        out_shape=jax.ShapeDtypeStruct((G, S, n), out_dtype),
        compiler_params=_params("arbitrary", "arbitrary"),
        name="proj_gelu_norm",
    )(h, w, g.reshape(1, n))


def _q_proj_kernel(h_ref, wq_ref, gqa_ref, wuq_ref, gn_ref, gr_ref, tc_ref, ts_ref, q_ref, *, q_scale):
    zq = jnp.dot(h_ref[...], wq_ref[...], preferred_element_type=F32)
    zn = _rms(zq, gqa_ref[...]).astype(BF16)
    tc, ts = tc_ref[...], ts_ref[...]
    gn, gr = gn_ref[...], gr_ref[...]
    for hh in range(q_ref.shape[0]):
        blk = jnp.dot(zn, wuq_ref[:, hh * HEAD_PAD:(hh + 1) * HEAD_PAD], preferred_element_type=F32)
        nope = _rms(blk[:, :LANES], gn)
        rope = _rope_group(_rms_rope_group(blk[:, LANES:], gr), tc, ts)
        q_ref[hh, :, :LANES] = (nope * q_scale).astype(q_ref.dtype)
        q_ref[hh, :, LANES:] = (rope * q_scale).astype(q_ref.dtype)


def q_project(h, w_q, g_q_a, w_uq_pad, g_nope, g_rope_pad, tabs, ts, q_scale):
    G, S, D = h.shape
    ql = w_q.shape[1]
    H = w_uq_pad.shape[1] // HEAD_PAD
    const = lambda g, s: (0, 0)
    tab_spec = pl.BlockSpec((ts, LANES), lambda g, s: (s, 0))
    return pl.pallas_call(
        functools.partial(_q_proj_kernel, q_scale=q_scale),
        grid=(G, S // ts),
        in_specs=[pl.BlockSpec((None, ts, D), lambda g, s: (g, s, 0)),
                  pl.BlockSpec((D, ql), const),
                  pl.BlockSpec((1, ql), const),
                  pl.BlockSpec((ql, H * HEAD_PAD), const),
                  pl.BlockSpec((1, LANES), const),
                  pl.BlockSpec((1, LANES), const),
                  tab_spec, tab_spec],
        out_specs=pl.BlockSpec((None, H, ts, HEAD_PAD), lambda g, s: (g, 0, s, 0)),
        out_shape=jax.ShapeDtypeStruct((G, H, S, HEAD_PAD), BF16),
        compiler_params=_params("arbitrary", "arbitrary"),
        name="q_project",
    )(h, w_q, g_q_a.reshape(1, ql), w_uq_pad, g_nope.reshape(1, LANES), g_rope_pad, *tabs)


def _kv_proj_kernel(h_ref, wkv_ref, gkva_ref, gkr_ref, tc_ref, ts_ref, *rest, n_rope, expand):
    if expand:
        wuk_ref, wuv_ref, gkn_ref, c_ref, kr_ref, k_ref, v_ref = rest
    else:
        c_ref, kr_ref, krp_ref = rest
    lat = c_ref.shape[-1]
    z = jnp.dot(h_ref[...], wkv_ref[...], preferred_element_type=F32)
    c = _rms(z[:, :lat], gkva_ref[...])
    c_ref[...] = c
    rope = _rope_group(_rms_rope_group(z[:, lat:], gkr_ref[...]), tc_ref[...], ts_ref[...])
    kr_ref[...] = rope[:, :n_rope]
    rope_b = rope.astype(BF16)
    if not expand:
        krp_ref[...] = rope_b
        return
    cb = c.astype(BF16)
    gkn = gkn_ref[...]
    for pair in range(k_ref.shape[0] // 2):
        cols = slice(pair * 2 * LANES, (pair + 1) * 2 * LANES)
        kn2 = jnp.dot(cb, wuk_ref[:, cols], preferred_element_type=F32)
        v2 = jnp.dot(cb, wuv_ref[:, cols], preferred_element_type=F32)
        for sub in range(2):
            hh = 2 * pair + sub
            lanes = slice(sub * LANES, (sub + 1) * LANES)
            k_ref[hh, :, :LANES] = _rms(kn2[:, lanes], gkn).astype(k_ref.dtype)
            k_ref[hh, :, LANES:] = rope_b
            v_ref[hh] = v2[:, lanes].astype(v_ref.dtype)


def kv_project(h, w_kv, g_kv_a, g_k_rope_pad, tabs, ts, n_rope, expand_weights=None):
    G, S, D = h.shape
    lat = g_kv_a.shape[-1]
    const = lambda g, s: (0, 0)
    tab_spec = pl.BlockSpec((ts, LANES), lambda g, s: (s, 0))
    in_specs = [pl.BlockSpec((None, ts, D), lambda g, s: (g, s, 0)),
                pl.BlockSpec((D, lat + LANES), const),
                pl.BlockSpec((1, lat), const),
                pl.BlockSpec((1, LANES), const),
                tab_spec, tab_spec]
    args = [h, w_kv, g_kv_a.reshape(1, lat), g_k_rope_pad, *tabs]
    out_specs = [pl.BlockSpec((None, ts, lat), lambda g, s: (g, s, 0)),
                 pl.BlockSpec((None, ts, n_rope), lambda g, s: (g, s, 0))]
    out_shape = [jax.ShapeDtypeStruct((G, S, lat), F32),
                 jax.ShapeDtypeStruct((G, S, n_rope), F32)]
    expand = expand_weights is not None
    if expand:
        w_uk, w_uv, g_k_nope = expand_weights
        H = w_uk.shape[1] // LANES
        in_specs += [pl.BlockSpec(w_uk.shape, const), pl.BlockSpec(w_uv.shape, const),
                     pl.BlockSpec((1, LANES), const)]
        args += [w_uk, w_uv, g_k_nope.reshape(1, LANES)]
        out_specs += [pl.BlockSpec((None, H, ts, HEAD_PAD), lambda g, s: (g, 0, s, 0)),
                      pl.BlockSpec((None, H, ts, LANES), lambda g, s: (g, 0, s, 0))]
        out_shape += [jax.ShapeDtypeStruct((G, H, S, HEAD_PAD), BF16),
                      jax.ShapeDtypeStruct((G, H, S, LANES), BF16)]
    else:
        out_specs.append(pl.BlockSpec((None, ts, LANES), lambda g, s: (g, s, 0)))
        out_shape.append(jax.ShapeDtypeStruct((G, S, LANES), BF16))
    return pl.pallas_call(
        functools.partial(_kv_proj_kernel, n_rope=n_rope, expand=expand),
        grid=(G, S // ts),
        in_specs=in_specs,
        out_specs=out_specs,
        out_shape=out_shape,
        compiler_params=_params("arbitrary", "arbitrary"),
        name="kv_project",
    )(*args)


def _spatial_gate_kernel(u_ref, v_ref, w_ref, b_ref, o_ref, wm_sc, *, n_chunks):
    row = lax.broadcasted_iota(jnp.int32, (SG_CHUNK, SG_CHUNK), 0)
    col = lax.broadcasted_iota(jnp.int32, (SG_CHUNK, SG_CHUNK), 1)
    tril = col <= row
    n_groups = w_ref.shape[0]
    for g in range(n_groups):
        wm_sc[g] = jnp.where(tril, w_ref[g], 0.0).astype(BF16)

    def chunk_body(n, carry):
        rows = pl.ds(pl.multiple_of(n * SG_CHUNK, SG_CHUNK), SG_CHUNK)
        for g in range(n_groups):
            lanes = slice(g * LANES, (g + 1) * LANES)
            mix = jnp.dot(wm_sc[g], v_ref[rows, lanes], preferred_element_type=F32) + b_ref[:, lanes]
            o_ref[rows, lanes] = (u_ref[rows, lanes].astype(F32) * mix).astype(o_ref.dtype)
        return carry

    lax.fori_loop(0, n_chunks, chunk_body, 0)


def spatial_gate(u, v, w_s, bias_rows, ts):
    G, S, W = u.shape
    blk = pl.BlockSpec((None, ts, W), lambda g, s: (g, s, 0))
    return pl.pallas_call(
        functools.partial(_spatial_gate_kernel, n_chunks=ts // SG_CHUNK),
        grid=(G, S // ts),
        in_specs=[blk, blk,
                  pl.BlockSpec(w_s.shape, lambda g, s: (0, 0, 0)),
                  pl.BlockSpec(bias_rows.shape, lambda g, s: (0, 0))],
        out_specs=blk,
        out_shape=jax.ShapeDtypeStruct((G, S, W), BF16),
        scratch_shapes=[pltpu.VMEM(w_s.shape, BF16)],
        compiler_params=_params("arbitrary", "arbitrary"),
        name="spatial_gate",
    )(u, v, w_s, bias_rows)


def _spatial_gate_open_kernel(u_ref, v_ref, wl_ref, b_ref, o_ref):
    T = u_ref.shape[1]
    for i in range(T):
        acc = b_ref[i:i + 1, :] + wl_ref[i, 0:1, :] * v_ref[:, 0, :]
        for j in range(1, i + 1):
            acc = acc + wl_ref[i, j:j + 1, :] * v_ref[:, j, :]
        o_ref[:, i, :] = (u_ref[:, i, :].astype(F32) * acc).astype(o_ref.dtype)


def spatial_gate_open(u, v, w_lanes, bias_rows):
    B, T, W = u.shape
    full = lambda a: pl.BlockSpec(a.shape, lambda i: (0,) * a.ndim)
    return pl.pallas_call(
        _spatial_gate_open_kernel,
        grid=(1,),
        in_specs=[full(u), full(v), full(w_lanes), full(bias_rows)],
        out_specs=pl.BlockSpec((B, T, W), lambda i: (0, 0, 0)),
        out_shape=jax.ShapeDtypeStruct((B, T, W), BF16),
        compiler_params=_params("arbitrary"),
        name="spatial_gate_open",
    )(u, v, w_lanes, bias_rows)


def _attn_kernel(q_ref, k_ref, v_ref, o_ref, m_sc, acc_sc, *, tq, tk, td):
    i = pl.program_id(2)
    q = q_ref[...]
    dv = v_ref.shape[-1]
    unroll = tq // tk
    nt = (((1,), (1,)), ((), ()))
    ones = jnp.ones((tk, LANES), BF16)

    def scores(k0):
        return lax.dot_general(q, k_ref[pl.ds(k0, tk), :], nt, preferred_element_type=F32)

    def step(s, k0, m_old, acc_old):
        rows, keys = s.shape
        row_max = jnp.max(s, axis=-1, keepdims=True)
        m_new = jnp.broadcast_to(row_max, (rows, LANES)) if m_old is None else jnp.maximum(m_old, row_max)
        p = jnp.concatenate([jnp.exp2(s[:, c * LANES:(c + 1) * LANES] - m_new)
                             for c in range(keys // LANES)], axis=1).astype(BF16)
        v1 = jnp.concatenate([v_ref[pl.ds(k0, keys), :], ones[:keys]], axis=1)
        pv = jnp.dot(p, v1, preferred_element_type=F32)
        if m_old is None:
            return m_new, pv
        alpha = jnp.exp2(m_old - m_new)
        return m_new, jnp.concatenate([alpha] * ((dv + LANES) // LANES), axis=1) * acc_old + pv

    d0 = pl.multiple_of(i * tq, tq)
    chunk_mask = (lax.broadcasted_iota(jnp.int32, (td, td), 1) // CHUNK
                  <= lax.broadcasted_iota(jnp.int32, (td, td), 0) // CHUNK)
    for c in range(tq // td):
        r0 = c * td
        s = lax.dot_general(q_ref[r0:, :], k_ref[pl.ds(d0 + r0, td), :], nt, preferred_element_type=F32)
        top = jnp.where(chunk_mask, s[:td], -jnp.inf)
        s = top if r0 + td == tq else jnp.concatenate([top, s[td:]], axis=0)
        if c == 0:
            m, acc = step(s, d0, None, None)
        else:
            m, acc = step(s, d0 + r0, m_sc[r0:, :], acc_sc[r0:, :])
        m_sc[r0:, :] = m
        acc_sc[r0:, :] = acc

    def body(j, carry):
        k0 = pl.multiple_of(j * tq, tq)
        ss = [scores(k0 + u * tk) for u in range(unroll)]
        m, acc = m_sc[...], acc_sc[...]
        for u in range(unroll):
            m, acc = step(ss[u], k0 + u * tk, m, acc)
        m_sc[...] = m
        acc_sc[...] = acc
        return carry

    lax.fori_loop(0, i, body, 0)
    acc = acc_sc[...]
    o_ref[...] = (acc[:, :dv] / acc[:, dv:]).astype(o_ref.dtype)


def attention_prompt(q, k, v, tq, tk, td):
    G, H, S, _ = q.shape
    dv = v.shape[-1]
    assert dv == LANES and tq % tk == 0 and tq % td == 0 and td % LANES == 0 and td <= tk
    return pl.pallas_call(
        functools.partial(_attn_kernel, tq=tq, tk=tk, td=td),
        grid=(G, H, S // tq),
        in_specs=[pl.BlockSpec((None, None, tq, HEAD_PAD), lambda g, h, i: (g, h, i, 0)),
                  pl.BlockSpec((None, None, S, HEAD_PAD), lambda g, h, i: (g, h, 0, 0)),
                  pl.BlockSpec((None, None, S, dv), lambda g, h, i: (g, h, 0, 0))],
        out_specs=pl.BlockSpec((None, tq, dv), lambda g, h, i: (g, i, h)),
        out_shape=jax.ShapeDtypeStruct((G, S, H * dv), BF16),
        scratch_shapes=[pltpu.VMEM((tq, LANES), F32), pltpu.VMEM((tq, dv + LANES), F32)],
        compiler_params=_params("arbitrary", "arbitrary", "arbitrary"),
        name="attention_prompt",
    )(q, k, v)


def _attn_cached_kernel(q_ref, cache_ref, ckr_ref, cnew_ref, krnew_ref, wuk_ref, wuv_ref, gkn_ref,
                        o_ref, call_sc, krall_sc, p_sc, *, past, n_new):
    H = q_ref.shape[0]
    L = past + n_new
    Lp = call_sc.shape[0]
    lat = call_sc.shape[1]
    call_sc[0:past, :] = cache_ref[...].astype(BF16)
    call_sc[past:L, :] = cnew_ref[...].astype(BF16)
    call_sc[L:Lp, :] = jnp.zeros((Lp - L, lat), BF16)
    krall_sc[0:past, :] = ckr_ref[...].astype(BF16)
    krall_sc[past:L, :] = krnew_ref[...]
    krall_sc[L:Lp, :] = jnp.zeros((Lp - L, LANES), BF16)
    call = call_sc[...]
    krall = krall_sc[...]
    gkn = gkn_ref[...]
    valid = lax.broadcasted_iota(jnp.int32, (n_new, Lp), 1) < L
    nt = (((1,), (1,)), ((), ()))
    for pair in range(H // 2):
        cols = slice(pair * 2 * LANES, (pair + 1) * 2 * LANES)
        kn2 = jnp.dot(call, wuk_ref[:, cols], preferred_element_type=F32)
        for sub in range(2):
            hh = 2 * pair + sub
            kn = _rms(kn2[:, sub * LANES:(sub + 1) * LANES], gkn).astype(BF16)
            kh = jnp.concatenate([kn, krall], axis=1)
            s = lax.dot_general(q_ref[hh], kh, nt, preferred_element_type=F32)
            s = jnp.where(valid, s, -jnp.inf)
            m = jnp.max(s, axis=-1, keepdims=True)
            p = jnp.exp2(s - m)
            p = p / jnp.sum(p, axis=-1, keepdims=True)
            p_sc[hh * n_new:(hh + 1) * n_new, :] = p.astype(BF16)
    o_lat = jnp.dot(p_sc[...], call, preferred_element_type=F32).astype(BF16)
    for hh in range(H):
        o_ref[:, hh * LANES:(hh + 1) * LANES] = jnp.dot(
            o_lat[hh * n_new:(hh + 1) * n_new, :], wuv_ref[:, hh * LANES:(hh + 1) * LANES],
            preferred_element_type=F32).astype(o_ref.dtype)


def attention_cached(q, cache_lat, cache_kr_pad, c_new, kr_new_pad, w_uk, w_uv, g_k_nope, n_new):
    _, H, BT, _ = q.shape
    B, past, lat = cache_lat.shape
    L = past + n_new
    Lp = -(-L // LANES) * LANES
    const = lambda b: (0, 0)
    return pl.pallas_call(
        functools.partial(_attn_cached_kernel, past=past, n_new=n_new),
        grid=(B,),
        in_specs=[pl.BlockSpec((None, H, n_new, HEAD_PAD), lambda b: (0, 0, b, 0)),
                  pl.BlockSpec((None, past, lat), lambda b: (b, 0, 0)),
                  pl.BlockSpec((None, past, LANES), lambda b: (b, 0, 0)),
                  pl.BlockSpec((None, n_new, lat), lambda b: (0, b, 0)),
                  pl.BlockSpec((None, n_new, LANES), lambda b: (0, b, 0)),
                  pl.BlockSpec(w_uk.shape, const), pl.BlockSpec(w_uv.shape, const),
                  pl.BlockSpec((1, LANES), const)],
        out_specs=pl.BlockSpec((None, n_new, H * LANES), lambda b: (0, b, 0)),
        out_shape=jax.ShapeDtypeStruct((1, BT, H * LANES), BF16),
        scratch_shapes=[pltpu.VMEM((Lp, lat), BF16), pltpu.VMEM((Lp, LANES), BF16),
                        pltpu.VMEM((H * n_new, Lp), BF16)],
        compiler_params=_params("arbitrary"),
        name="attention_cached",
    )(q, cache_lat, cache_kr_pad, c_new, kr_new_pad, w_uk, w_uv, g_k_nope.reshape(1, LANES))


def _merge_kernel(a_ref, b_ref, ga_ref, gb_ref, wpa_ref, wpb_ref, o_ref):
    pa = jnp.dot(a_ref[...], wpa_ref[...], preferred_element_type=F32)
    pb = jnp.dot(b_ref[...], wpb_ref[...], preferred_element_type=F32)
    o_ref[...] = (ga_ref[...].astype(F32) * pa + gb_ref[...].astype(F32) * pb).astype(o_ref.dtype)


def merge_branches(o_sg, o_mla, gates, w_pa, w_pb, ts, tn):
    G, S, W = o_sg.shape
    D = w_pa.shape[1]
    nj = D // tn
    row = lambda g, s, j: (g, s, 0)
    return pl.pallas_call(
        _merge_kernel,
        grid=(G, S // ts, nj),
        in_specs=[pl.BlockSpec((None, ts, W), row),
                  pl.BlockSpec((None, ts, o_mla.shape[-1]), row),
                  pl.BlockSpec((None, ts, tn), lambda g, s, j: (g, s, j)),
                  pl.BlockSpec((None, ts, tn), lambda g, s, j: (g, s, j + nj)),
                  pl.BlockSpec((W, tn), lambda g, s, j: (0, j)),
                  pl.BlockSpec((o_mla.shape[-1], tn), lambda g, s, j: (0, j))],
        out_specs=pl.BlockSpec((None, ts, tn), lambda g, s, j: (g, s, j)),
        out_shape=jax.ShapeDtypeStruct((G, S, D), BF16),
        compiler_params=_params("arbitrary", "arbitrary", "arbitrary"),
        name="merge_branches",
    )(o_sg, o_mla, gates, gates, w_pa, w_pb)


def _out_proj_kernel(m_ref, wo_ref, x_ref, g1_ref, gn_ref, sc_ref, sh_ref, x1_ref, h2_ref):
    y = jnp.dot(m_ref[...], wo_ref[...], preferred_element_type=F32)
    x1 = x_ref[...] + g1_ref[...] * y
    x1_ref[...] = x1
    h2_ref[...] = (_rms(x1, gn_ref[...]) * (1.0 + sc_ref[...]) + sh_ref[...]).astype(h2_ref.dtype)


def out_project(m, w_o, x, g1, g_norm2, sc2, sh2, ts):
    G, S, D = x.shape
    row = pl.BlockSpec((None, ts, D), lambda g, s: (g, s, 0))
    return pl.pallas_call(
        _out_proj_kernel,
        grid=(G, S // ts),
        in_specs=[row, pl.BlockSpec(w_o.shape, lambda g, s: (0, 0)), row,
                  _mod_spec(g1, ts), pl.BlockSpec((1, D), lambda g, s: (0, 0)),
                  _mod_spec(sc2, ts), _mod_spec(sh2, ts)],
        out_specs=[row, row],
        out_shape=[jax.ShapeDtypeStruct((G, S, D), F32), jax.ShapeDtypeStruct((G, S, D), BF16)],
        compiler_params=_params("arbitrary", "arbitrary"),
        name="out_project",
    )(m, w_o, x, g1, g_norm2.reshape(1, D), sc2, sh2)


def _mlp_kernel(h_ref, wup_ref, wdn_ref, x1_ref, g2_ref, o_ref):
    j = pl.program_id(2)
    hid = jnp.dot(h_ref[...], wup_ref[...], preferred_element_type=F32)
    hid = jnp.square(jnp.maximum(hid, 0.0)).astype(BF16)
    part = jnp.dot(hid, wdn_ref[...], preferred_element_type=F32)

    @pl.when(j == 0)
    def _():
        o_ref[...] = part

    @pl.when(j > 0)
    def _():
        o_ref[...] += part

    @pl.when(j == pl.num_programs(2) - 1)
    def _():
        o_ref[...] = x1_ref[...] + g2_ref[...] * o_ref[...]


def mlp_residual(h2, w_up, w_down, x1, g2, ts, th):
    G, S, D = x1.shape
    hidden = w_up.shape[1]
    row = pl.BlockSpec((None, ts, D), lambda g, s, j: (g, s, 0))
    return pl.pallas_call(
        _mlp_kernel,
        grid=(G, S // ts, hidden // th),
        in_specs=[row,
                  pl.BlockSpec((D, th), lambda g, s, j: (0, j)),
                  pl.BlockSpec((th, D), lambda g, s, j: (j, 0)),
                  row, _mod_spec(g2, ts)],
        out_specs=row,
        out_shape=jax.ShapeDtypeStruct((G, S, D), F32),
        compiler_params=_params("arbitrary", "arbitrary", "arbitrary"),
        name="mlp_residual",
    )(h2, w_up, w_down, x1, g2)


def _rope_tables(pos, n_rope):
    inv = jnp.float32(ROPE_BASE) ** (-jnp.arange(0, n_rope, 2, dtype=F32) / n_rope)
    ang = pos.astype(F32)[:, None] * inv[None, :]
    cos, sin = jnp.cos(ang), jnp.sin(ang)
    pad = jnp.zeros((pos.shape[0], LANES - n_rope), F32)
    return (jnp.concatenate([cos, cos, pad], axis=1), jnp.concatenate([-sin, sin, pad], axis=1))


def _dup_lanes(g):
    return jnp.concatenate([g.astype(F32), g.astype(F32)]).reshape(1, 2 * g.shape[0])


def _layer(x, mods, pos, P, ts, *, cache=None):
    sh1, sc1, g1, sh2, sc2, g2 = mods
    n_rope = P["n_rope"]
    tabs = _rope_tables(pos, n_rope)
    h = prenorm(x, P["g_norm1"], sc1, sh1, ts)
    u = proj_act(h, P["w_u"], "gelu", ts, P["w_u"].shape[1])
    gates = proj_act(h, P["w_g"], "sigmoid", ts, P["w_g"].shape[1] // 2)
    q = q_project(h, P["w_q"], P["g_q_a"], P["w_uq"], P["g_q_nope"], P["g_q_rope"], tabs, ts,
                  P["q_scale"])
    if cache is None:
        v = proj_gelu_norm(h, P["w_v"], P["g_sg"], ts, BF16)
        o_sg = spatial_gate(u, v, P["w_s"], P["b_rows"], min(1024, x.shape[1]))
        c_kv, k_rope, k, vv = kv_project(h, P["w_kv"], P["g_kv_a"], P["g_k_rope"], tabs, ts, n_rope,
                                         expand_weights=(P["w_uk"], P["w_uv"], P["g_k_nope"]))
        o_mla = attention_prompt(q, k, vv, tq=min(2048, x.shape[1]), tk=min(1024, x.shape[1]),
                                 td=min(512, x.shape[1]))
        extra = ()
    else:
        cache_lat, cache_kr, B, T = cache
        v = proj_gelu_norm(h, P["w_v"], P["g_sg"], ts, F32)
        W = v.shape[-1]
        o_sg = spatial_gate_open(u.reshape(B, T, W), v.reshape(B, T, W), P["w_lanes"][:T, :T],
                                 P["b_rows"][:T]).reshape(1, B * T, W)
        c_kv, k_rope, kr_pad = kv_project(h, P["w_kv"], P["g_kv_a"], P["g_k_rope"], tabs, ts, n_rope)
        cache_kr_pad = jnp.pad(cache_kr, ((0, 0), (0, 0), (0, LANES - n_rope)))
        o_mla = attention_cached(q, cache_lat, cache_kr_pad, c_kv, kr_pad, P["w_uk"], P["w_uv"],
                                 P["g_k_nope"], T)
        extra = (v,)
    m = merge_branches(o_sg, o_mla, gates, P["w_pa"], P["w_pb"], ts, P["w_pa"].shape[1] // 2)
    x1, h2 = out_project(m, P["w_o"], x, g1, P["g_norm2"], sc2, sh2, ts)
    y = mlp_residual(h2, P["w_up"], P["w_down"], x1, g2, ts, min(2048, P["w_up"].shape[1]))
    return (y, c_kv, k_rope) + extra


def kernel(x_prompt, x_sample, cache_kv_latent, cache_k_rope, c_prompt, c_sample, w_ada, b_ada, g_norm1, g_norm2, w_in, g_sg, w_s, b_s, g_q_a, w_uq, g_q_nope, g_q_rope, g_kv_a, g_k_rope, w_uk, g_k_nope, w_uv, w_pa, w_pb, w_o, w_up, w_down):
    depth = w_in.shape[0]
    Bp, S, D = x_prompt.shape
    Bs, T, _ = x_sample.shape
    past = cache_kv_latent.shape[2]
    q_lora = g_q_a.shape[-1]
    lat = g_kv_a.shape[-1]
    n_nope = g_q_nope.shape[-1]
    n_rope = g_q_rope.shape[-1]
    H = w_uk.shape[2]
    sg_w = g_sg.shape[-1]
    off_q = 2 * sg_w
    off_kv = off_q + q_lora
    off_gate = off_kv + lat + n_rope
    assert n_nope == LANES and w_uv.shape[-1] == LANES and 2 * n_rope == LANES
    assert sg_w // SG_GROUPS == LANES and H == N_HEADS

    nb = Bp + Bs
    nb_pad = -(-nb // 8) * 8
    c_all = jnp.concatenate([c_prompt, c_sample, jnp.zeros((nb_pad - nb, D), F32)], axis=0)

    y_p, y_s = x_prompt, x_sample.reshape(1, Bs * T, D)
    outs = [[] for _ in range(5)]
    for l in range(depth):
        wi = w_in[l].astype(BF16)
        w_uq_l = w_uq[l].astype(BF16).reshape(q_lora, H, n_nope + n_rope)
        w_uq_pad = jnp.concatenate([w_uq_l, w_uq_l[:, :, n_nope:]], axis=2)
        P = {
            "n_rope": n_rope,
            "q_scale": float((n_nope + n_rope) ** -0.5 * LOG2E),
            "g_norm1": g_norm1[l], "g_norm2": g_norm2[l], "g_sg": g_sg[l],
            "w_u": wi[:, :sg_w], "w_v": wi[:, sg_w:off_q], "w_q": wi[:, off_q:off_kv],
            "w_kv": jnp.concatenate([wi[:, off_kv:off_gate], wi[:, off_kv + lat:off_gate]], axis=1),
            "w_g": wi[:, off_gate:],
            "g_q_a": g_q_a[l], "w_uq": w_uq_pad.reshape(q_lora, H * HEAD_PAD),
            "g_q_nope": g_q_nope[l], "g_q_rope": _dup_lanes(g_q_rope[l]),
            "g_kv_a": g_kv_a[l], "g_k_rope": _dup_lanes(g_k_rope[l]),
            "w_uk": w_uk[l].astype(BF16).reshape(lat, H * n_nope),
            "w_uv": w_uv[l].astype(BF16).reshape(lat, H * LANES),
            "g_k_nope": g_k_nope[l],
            "w_s": w_s[l],
            "b_rows": jnp.repeat(b_s[l].T, LANES, axis=1),
            "w_lanes": jnp.repeat(w_s[l][:, :T, :T].transpose(1, 2, 0), LANES, axis=2),
            "w_pa": w_pa[l].astype(BF16), "w_pb": w_pb[l].astype(BF16), "w_o": w_o[l].astype(BF16),
            "w_up": w_up[l].astype(BF16), "w_down": w_down[l].astype(BF16),
        }
        mod = ada_project(c_all, w_ada[l], b_ada[l])
        mods_p = [a.reshape(Bp, 1, D) for a in jnp.split(mod[:Bp], 6, axis=-1)]
        mods_s = [jnp.repeat(a, T, axis=0).reshape(1, Bs * T, D)
                  for a in jnp.split(mod[Bp:nb], 6, axis=-1)]

        ts_p = min(512, S)
        y_p, lp, kp = _layer(y_p, mods_p, jnp.arange(S), P, ts_p)
        pos_s = jnp.tile(past + jnp.arange(T), Bs)
        y_s, ls, ks, vs = _layer(y_s, mods_s, pos_s, P, Bs * T,
                                 cache=(cache_kv_latent[l], cache_k_rope[l], Bs, T))
        for lst, a in zip(outs, (lp, kp, ls.reshape(Bs, T, lat), ks.reshape(Bs, T, n_rope),
                                 vs.reshape(Bs, T, sg_w))):
            lst.append(a)
    return (y_p, y_s.reshape(Bs, T, D)) + tuple(jnp.stack(o) for o in outs)
```

```python
import functools
import math

import jax
import jax.numpy as jnp
import numpy as np
from jax import lax
from jax.experimental import pallas as pl
from jax.experimental.pallas import tpu as pltpu

F32 = jnp.float32
BF16 = jnp.bfloat16

EPS = 1e-6
ROPE_BASE = 10000.0
N_HEADS = 16
CHUNK = 64
SG_CHUNK = 128
SG_GROUPS = 16
LANES = 128
HEAD_PAD = 256
LOG2E = math.log2(math.e)

VMEM_LIMIT = 58 * 1024 * 1024


def _params(*sem):
    return pltpu.CompilerParams(dimension_semantics=sem, vmem_limit_bytes=VMEM_LIMIT)


def _rms(x, g):
    ms = jnp.mean(x * x, axis=-1, keepdims=True)
    return x * lax.rsqrt(ms + EPS) * g


def _rms_rope_group(t, g_dup):
    ms = jnp.mean(t * t, axis=-1, keepdims=True)
    return t * lax.rsqrt(ms + EPS) * g_dup


def _rope_group(t, tc, ts):
    return t * tc + pltpu.roll(t, LANES - LANES // 4, 1) * ts


def _gelu(z):
    return 0.5 * z * (1.0 + lax.erf(z * np.float32(math.sqrt(0.5))))


def _mod_spec(mod, ts):
    d = mod.shape[-1]
    if mod.shape[1] == 1:
        return pl.BlockSpec((None, 1, d), lambda g, s, *_: (g, 0, 0))
    return pl.BlockSpec((None, ts, d), lambda g, s, *_: (g, s, 0))


def _ada_kernel(c_ref, w_ref, b_ref, o_ref):
    c = c_ref[...]
    s = (c * jax.nn.sigmoid(c)).astype(BF16)
    o_ref[...] = jnp.dot(s, w_ref[...].astype(BF16), preferred_element_type=F32) + b_ref[...]


def ada_project(c, w_ada, b_ada, tn=1024):
    r, d = c.shape
    n = w_ada.shape[1]
    return pl.pallas_call(
        _ada_kernel,
        grid=(n // tn,),
        in_specs=[pl.BlockSpec((r, d), lambda j: (0, 0)),
                  pl.BlockSpec((d, tn), lambda j: (0, j)),
                  pl.BlockSpec((1, tn), lambda j: (0, j))],
        out_specs=pl.BlockSpec((r, tn), lambda j: (0, j)),
        out_shape=jax.ShapeDtypeStruct((r, n), F32),
        compiler_params=_params("arbitrary"),
        name="ada_project",
    )(c, w_ada, b_ada.reshape(1, n))


def _prenorm_kernel(x_ref, g_ref, sc_ref, sh_ref, o_ref):
    y = _rms(x_ref[...], g_ref[...])
    o_ref[...] = (y * (1.0 + sc_ref[...]) + sh_ref[...]).astype(o_ref.dtype)


def prenorm(x, g, sc, sh, ts):
    G, S, D = x.shape
    return pl.pallas_call(
        _prenorm_kernel,
        grid=(G, S // ts),
        in_specs=[pl.BlockSpec((None, ts, D), lambda g_, s: (g_, s, 0)),
                  pl.BlockSpec((1, D), lambda g_, s: (0, 0)),
                  _mod_spec(sc, ts), _mod_spec(sh, ts)],
        out_specs=pl.BlockSpec((None, ts, D), lambda g_, s: (g_, s, 0)),
        out_shape=jax.ShapeDtypeStruct((G, S, D), BF16),
        compiler_params=_params("arbitrary", "arbitrary"),
        name="prenorm",
    )(x, g.reshape(1, D), sc, sh)


def _proj_act_kernel(h_ref, w_ref, o_ref, *, act):
    z = jnp.dot(h_ref[...], w_ref[...], preferred_element_type=F32)
    if act == "gelu":
        a = _gelu(z)
    else:
        a = jax.nn.sigmoid(z)
    o_ref[...] = a.astype(o_ref.dtype)


def proj_act(h, w, act, ts, tn):
    G, S, D = h.shape
    n = w.shape[1]
    return pl.pallas_call(
        functools.partial(_proj_act_kernel, act=act),
        grid=(G, S // ts, n // tn),
        in_specs=[pl.BlockSpec((None, ts, D), lambda g, s, j: (g, s, 0)),
                  pl.BlockSpec((D, tn), lambda g, s, j: (0, j))],
        out_specs=pl.BlockSpec((None, ts, tn), lambda g, s, j: (g, s, j)),
        out_shape=jax.ShapeDtypeStruct((G, S, n), BF16),
        compiler_params=_params("arbitrary", "arbitrary", "arbitrary"),
        name="proj_" + act,
    )(h, w)


def _proj_gelu_norm_kernel(h_ref, w_ref, g_ref, o_ref):
    z = jnp.dot(h_ref[...], w_ref[...], preferred_element_type=F32)
    o_ref[...] = _rms(_gelu(z), g_ref[...]).astype(o_ref.dtype)


def proj_gelu_norm(h, w, g, ts, out_dtype):
    G, S, D = h.shape
    n = w.shape[1]
    return pl.pallas_call(
        _proj_gelu_norm_kernel,
        grid=(G, S // ts),
        in_specs=[pl.BlockSpec((None, ts, D), lambda g_, s: (g_, s, 0)),
                  pl.BlockSpec((D, n), lambda g_, s: (0, 0)),
                  pl.BlockSpec((1, n), lambda g_, s: (0, 0))],
        out_specs=pl.BlockSpec((None, ts, n), lambda g_, s: (g_, s, 0)),
        out_shape=jax.ShapeDtypeStruct((G, S, n), out_dtype),
        compiler_params=_params("arbitrary", "arbitrary"),
        name="proj_gelu_norm",
    )(h, w, g.reshape(1, n))


def _q_proj_kernel(h_ref, wq_ref, gqa_ref, wuq_ref, gn_ref, gr_ref, tc_ref, ts_ref, q_ref, *, q_scale):
    zq = jnp.dot(h_ref[...], wq_ref[...], preferred_element_type=F32)
    zn = _rms(zq, gqa_ref[...]).astype(BF16)
    tc, ts = tc_ref[...], ts_ref[...]
    gains = jnp.concatenate([gn_ref[...], gr_ref[...]], axis=1) * q_scale
    same_tile = (lax.broadcasted_iota(jnp.int32, (HEAD_PAD, HEAD_PAD), 0) // LANES
                 == lax.broadcasted_iota(jnp.int32, (HEAD_PAD, HEAD_PAD), 1) // LANES)
    tile_mean = jnp.where(same_tile, 1.0 / LANES, 0.0).astype(BF16)
    for pair in range(q_ref.shape[0] // 2):
        blk2 = jnp.dot(zn, wuq_ref[:, pair * 2 * HEAD_PAD:(pair + 1) * 2 * HEAD_PAD],
                       preferred_element_type=F32)
        for sub in range(2):
            hh = 2 * pair + sub
            blk = blk2[:, sub * HEAD_PAD:(sub + 1) * HEAD_PAD]
            ms = jnp.dot((blk * blk).astype(BF16), tile_mean, preferred_element_type=F32)
            y = blk * lax.rsqrt(ms + EPS) * gains
            q_ref[hh, :, :LANES] = y[:, :LANES].astype(q_ref.dtype)
            q_ref[hh, :, LANES:] = _rope_group(y[:, LANES:], tc, ts).astype(q_ref.dtype)


def q_project(h, w_q, g_q_a, w_uq_pad, g_nope, g_rope_pad, tabs, ts, q_scale):
    G, S, D = h.shape
    ql = w_q.shape[1]
    H = w_uq_pad.shape[1] // HEAD_PAD
    const = lambda g, s: (0, 0)
    tab_spec = pl.BlockSpec((ts, LANES), lambda g, s: (s, 0))
    return pl.pallas_call(
        functools.partial(_q_proj_kernel, q_scale=q_scale),
        grid=(G, S // ts),
        in_specs=[pl.BlockSpec((None, ts, D), lambda g, s: (g, s, 0)),
                  pl.BlockSpec((D, ql), const),
                  pl.BlockSpec((1, ql), const),
                  pl.BlockSpec((ql, H * HEAD_PAD), const),
                  pl.BlockSpec((1, LANES), const),
                  pl.BlockSpec((1, LANES), const),
                  tab_spec, tab_spec],
        out_specs=pl.BlockSpec((None, H, ts, HEAD_PAD), lambda g, s: (g, 0, s, 0)),
        out_shape=jax.ShapeDtypeStruct((G, H, S, HEAD_PAD), BF16),
        compiler_params=_params("arbitrary", "arbitrary"),
        name="q_project",
    )(h, w_q, g_q_a.reshape(1, ql), w_uq_pad, g_nope.reshape(1, LANES), g_rope_pad, *tabs)


def _kv_proj_kernel(h_ref, wkv_ref, gkva_ref, gkr_ref, tc_ref, ts_ref, *rest, n_rope, expand):
    if expand:
        wuk_ref, wuv_ref, gkn_ref, c_ref, kr_ref, k_ref, v_ref = rest
    else:
        c_ref, kr_ref, krp_ref = rest
    lat = c_ref.shape[-1]
    z = jnp.dot(h_ref[...], wkv_ref[...], preferred_element_type=F32)
    c = _rms(z[:, :lat], gkva_ref[...])
    c_ref[...] = c
    rope = _rope_group(_rms_rope_group(z[:, lat:], gkr_ref[...]), tc_ref[...], ts_ref[...])
    kr_ref[...] = rope[:, :n_rope]
    rope_b = rope.astype(BF16)
    if not expand:
        krp_ref[...] = rope_b
        return
    cb = c.astype(BF16)
    gkn = gkn_ref[...]
    for pair in range(k_ref.shape[0] // 2):
        cols = slice(pair * 2 * LANES, (pair + 1) * 2 * LANES)
        kn2 = jnp.dot(cb, wuk_ref[:, cols], preferred_element_type=F32)
        v2 = jnp.dot(cb, wuv_ref[:, cols], preferred_element_type=F32)
        for sub in range(2):
            hh = 2 * pair + sub
            lanes = slice(sub * LANES, (sub + 1) * LANES)
            k_ref[hh, :, :LANES] = _rms(kn2[:, lanes], gkn).astype(k_ref.dtype)
            k_ref[hh, :, LANES:] = rope_b
            v_ref[hh] = v2[:, lanes].astype(v_ref.dtype)


def kv_project(h, w_kv, g_kv_a, g_k_rope_pad, tabs, ts, n_rope, expand_weights=None):
    G, S, D = h.shape
    lat = g_kv_a.shape[-1]
    const = lambda g, s: (0, 0)
    tab_spec = pl.BlockSpec((ts, LANES), lambda g, s: (s, 0))
    in_specs = [pl.BlockSpec((None, ts, D), lambda g, s: (g, s, 0)),
                pl.BlockSpec((D, lat + LANES), const),
                pl.BlockSpec((1, lat), const),
                pl.BlockSpec((1, LANES), const),
                tab_spec, tab_spec]
    args = [h, w_kv, g_kv_a.reshape(1, lat), g_k_rope_pad, *tabs]
    out_specs = [pl.BlockSpec((None, ts, lat), lambda g, s: (g, s, 0)),
                 pl.BlockSpec((None, ts, n_rope), lambda g, s: (g, s, 0))]
    out_shape = [jax.ShapeDtypeStruct((G, S, lat), F32),
                 jax.ShapeDtypeStruct((G, S, n_rope), F32)]
    expand = expand_weights is not None
    if expand:
        w_uk, w_uv, g_k_nope = expand_weights
        H = w_uk.shape[1] // LANES
        in_specs += [pl.BlockSpec(w_uk.shape, const), pl.BlockSpec(w_uv.shape, const),
                     pl.BlockSpec((1, LANES), const)]
        args += [w_uk, w_uv, g_k_nope.reshape(1, LANES)]
        out_specs += [pl.BlockSpec((None, H, ts, HEAD_PAD), lambda g, s: (g, 0, s, 0)),
                      pl.BlockSpec((None, H, ts, LANES), lambda g, s: (g, 0, s, 0))]
        out_shape += [jax.ShapeDtypeStruct((G, H, S, HEAD_PAD), BF16),
                      jax.ShapeDtypeStruct((G, H, S, LANES), BF16)]
    else:
        out_specs.append(pl.BlockSpec((None, ts, LANES), lambda g, s: (g, s, 0)))
        out_shape.append(jax.ShapeDtypeStruct((G, S, LANES), BF16))
    return pl.pallas_call(
        functools.partial(_kv_proj_kernel, n_rope=n_rope, expand=expand),
        grid=(G, S // ts),
        in_specs=in_specs,
        out_specs=out_specs,
        out_shape=out_shape,
        compiler_params=_params("arbitrary", "arbitrary"),
        name="kv_project",
    )(*args)


def _spatial_gate_kernel(u_ref, v_ref, w_ref, b_ref, o_ref, wm_sc, *, n_chunks):
    row = lax.broadcasted_iota(jnp.int32, (SG_CHUNK, SG_CHUNK), 0)
    col = lax.broadcasted_iota(jnp.int32, (SG_CHUNK, SG_CHUNK), 1)
    tril = col <= row
    n_groups = w_ref.shape[0]
    for g in range(n_groups):
        wm_sc[g] = jnp.where(tril, w_ref[g], 0.0).astype(BF16)

    def chunk_body(n, carry):
        rows = pl.ds(pl.multiple_of(n * SG_CHUNK, SG_CHUNK), SG_CHUNK)
        for g in range(n_groups):
            lanes = slice(g * LANES, (g + 1) * LANES)
            mix = jnp.dot(wm_sc[g], v_ref[rows, lanes], preferred_element_type=F32) + b_ref[:, lanes]
            o_ref[rows, lanes] = (u_ref[rows, lanes].astype(F32) * mix).astype(o_ref.dtype)
        return carry

    lax.fori_loop(0, n_chunks, chunk_body, 0)


def spatial_gate(u, v, w_s, bias_rows, ts):
    G, S, W = u.shape
    blk = pl.BlockSpec((None, ts, W), lambda g, s: (g, s, 0))
    return pl.pallas_call(
        functools.partial(_spatial_gate_kernel, n_chunks=ts // SG_CHUNK),
        grid=(G, S // ts),
        in_specs=[blk, blk,
                  pl.BlockSpec(w_s.shape, lambda g, s: (0, 0, 0)),
                  pl.BlockSpec(bias_rows.shape, lambda g, s: (0, 0))],
        out_specs=blk,
        out_shape=jax.ShapeDtypeStruct((G, S, W), BF16),
        scratch_shapes=[pltpu.VMEM(w_s.shape, BF16)],
        compiler_params=_params("arbitrary", "arbitrary"),
        name="spatial_gate",
    )(u, v, w_s, bias_rows)


def _spatial_gate_open_kernel(u_ref, v_ref, wl_ref, b_ref, o_ref):
    T = u_ref.shape[1]
    for i in range(T):
        acc = b_ref[i:i + 1, :] + wl_ref[i, 0:1, :] * v_ref[:, 0, :]
        for j in range(1, i + 1):
            acc = acc + wl_ref[i, j:j + 1, :] * v_ref[:, j, :]
        o_ref[:, i, :] = (u_ref[:, i, :].astype(F32) * acc).astype(o_ref.dtype)


def spatial_gate_open(u, v, w_lanes, bias_rows):
    B, T, W = u.shape
    full = lambda a: pl.BlockSpec(a.shape, lambda i: (0,) * a.ndim)
    return pl.pallas_call(
        _spatial_gate_open_kernel,
        grid=(1,),
        in_specs=[full(u), full(v), full(w_lanes), full(bias_rows)],
        out_specs=pl.BlockSpec((B, T, W), lambda i: (0, 0, 0)),
        out_shape=jax.ShapeDtypeStruct((B, T, W), BF16),
        compiler_params=_params("arbitrary"),
        name="spatial_gate_open",
    )(u, v, w_lanes, bias_rows)


def _attn_kernel(q_ref, k_ref, v_ref, o_ref, m_sc, acc_sc, *, tq, tk, td):
    i = pl.program_id(2)
    q = q_ref[...]
    dv = v_ref.shape[-1]
    unroll = tq // tk
    nt = (((1,), (1,)), ((), ()))
    ones = jnp.ones((tk, LANES), BF16)

    def scores(k0):
        return lax.dot_general(q, k_ref[pl.ds(k0, tk), :], nt, preferred_element_type=F32)

    def step(s, k0, m_old, acc_old):
        rows, keys = s.shape
        row_max = jnp.max(s, axis=-1, keepdims=True)
        m_new = jnp.broadcast_to(row_max, (rows, LANES)) if m_old is None else jnp.maximum(m_old, row_max)
        p = jnp.concatenate([jnp.exp2(s[:, c * LANES:(c + 1) * LANES] - m_new)
                             for c in range(keys // LANES)], axis=1).astype(BF16)
        v1 = jnp.concatenate([v_ref[pl.ds(k0, keys), :], ones[:keys]], axis=1)
        pv = jnp.dot(p, v1, preferred_element_type=F32)
        if m_old is None:
            return m_new, pv
        alpha = jnp.exp2(m_old - m_new)
        return m_new, jnp.concatenate([alpha] * ((dv + LANES) // LANES), axis=1) * acc_old + pv

    d0 = pl.multiple_of(i * tq, tq)
    chunk_mask = (lax.broadcasted_iota(jnp.int32, (td, td), 1) // CHUNK
                  <= lax.broadcasted_iota(jnp.int32, (td, td), 0) // CHUNK)
    for c in range(tq // td):
        r0 = c * td
        s = lax.dot_general(q_ref[r0:, :], k_ref[pl.ds(d0 + r0, td), :], nt, preferred_element_type=F32)
        top = jnp.where(chunk_mask, s[:td], -jnp.inf)
        s = top if r0 + td == tq else jnp.concatenate([top, s[td:]], axis=0)
        if c == 0:
            m, acc = step(s, d0, None, None)
        else:
            m, acc = step(s, d0 + r0, m_sc[r0:, :], acc_sc[r0:, :])
        m_sc[r0:, :] = m
        acc_sc[r0:, :] = acc

    def body(j, carry):
        k0 = pl.multiple_of(j * tq, tq)
        ss = [scores(k0 + u * tk) for u in range(unroll)]
        m, acc = m_sc[...], acc_sc[...]
        for u in range(unroll):
            m, acc = step(ss[u], k0 + u * tk, m, acc)
        m_sc[...] = m
        acc_sc[...] = acc
        return carry

    lax.fori_loop(0, i, body, 0)
    acc = acc_sc[...]
    o_ref[...] = (acc[:, :dv] / acc[:, dv:]).astype(o_ref.dtype)


def attention_prompt(q, k, v, tq, tk, td):
    G, H, S, _ = q.shape
    dv = v.shape[-1]
    assert dv == LANES and tq % tk == 0 and tq % td == 0 and td % LANES == 0 and td <= tk
    return pl.pallas_call(
        functools.partial(_attn_kernel, tq=tq, tk=tk, td=td),
        grid=(G, H, S // tq),
        in_specs=[pl.BlockSpec((None, None, tq, HEAD_PAD), lambda g, h, i: (g, h, i, 0)),
                  pl.BlockSpec((None, None, S, HEAD_PAD), lambda g, h, i: (g, h, 0, 0)),
                  pl.BlockSpec((None, None, S, dv), lambda g, h, i: (g, h, 0, 0))],
        out_specs=pl.BlockSpec((None, tq, dv), lambda g, h, i: (g, i, h)),
        out_shape=jax.ShapeDtypeStruct((G, S, H * dv), BF16),
        scratch_shapes=[pltpu.VMEM((tq, LANES), F32), pltpu.VMEM((tq, dv + LANES), F32)],
        compiler_params=_params("arbitrary", "arbitrary", "arbitrary"),
        name="attention_prompt",
    )(q, k, v)


def _attn_cached_kernel(q_ref, cache_ref, ckr_ref, cnew_ref, krnew_ref, wuk_ref, wuv_ref, gkn_ref,
                        o_ref, call_sc, krall_sc, p_sc, *, past, n_new):
    H = q_ref.shape[0]
    L = past + n_new
    Lp = call_sc.shape[0]
    lat = call_sc.shape[1]
    call_sc[0:past, :] = cache_ref[...].astype(BF16)
    call_sc[past:L, :] = cnew_ref[...].astype(BF16)
    call_sc[L:Lp, :] = jnp.zeros((Lp - L, lat), BF16)
    krall_sc[0:past, :] = ckr_ref[...].astype(BF16)
    krall_sc[past:L, :] = krnew_ref[...]
    krall_sc[L:Lp, :] = jnp.zeros((Lp - L, LANES), BF16)
    call = call_sc[...]
    krall = krall_sc[...]
    gkn = gkn_ref[...]
    valid = lax.broadcasted_iota(jnp.int32, (n_new, Lp), 1) < L
    nt = (((1,), (1,)), ((), ()))
    group = 4
    for grp in range(H // group):
        cols = slice(grp * group * LANES, (grp + 1) * group * LANES)
        kn_g = jnp.dot(call, wuk_ref[:, cols], preferred_element_type=F32)
        for sub in range(group):
            hh = group * grp + sub
            kn = _rms(kn_g[:, sub * LANES:(sub + 1) * LANES], gkn).astype(BF16)
            kh = jnp.concatenate([kn, krall], axis=1)
            s = lax.dot_general(q_ref[hh], kh, nt, preferred_element_type=F32)
            s = jnp.where(valid, s, -jnp.inf)
            m = jnp.max(s, axis=-1, keepdims=True)
            p = jnp.exp2(s - m)
            p = p / jnp.sum(p, axis=-1, keepdims=True)
            p_sc[hh * n_new:(hh + 1) * n_new, :] = p.astype(BF16)
    o_lat = jnp.dot(p_sc[...], call, preferred_element_type=F32).astype(BF16)
    for hh in range(H):
        o_ref[:, hh * LANES:(hh + 1) * LANES] = jnp.dot(
            o_lat[hh * n_new:(hh + 1) * n_new, :], wuv_ref[:, hh * LANES:(hh + 1) * LANES],
            preferred_element_type=F32).astype(o_ref.dtype)


def attention_cached(q, cache_lat, cache_kr_pad, c_new, kr_new_pad, w_uk, w_uv, g_k_nope, n_new):
    _, H, BT, _ = q.shape
    B, past, lat = cache_lat.shape
    L = past + n_new
    Lp = -(-L // LANES) * LANES
    const = lambda b: (0, 0)
    return pl.pallas_call(
        functools.partial(_attn_cached_kernel, past=past, n_new=n_new),
        grid=(B,),
        in_specs=[pl.BlockSpec((None, H, n_new, HEAD_PAD), lambda b: (0, 0, b, 0)),
                  pl.BlockSpec((None, past, lat), lambda b: (b, 0, 0)),
                  pl.BlockSpec((None, past, LANES), lambda b: (b, 0, 0)),
                  pl.BlockSpec((None, n_new, lat), lambda b: (0, b, 0)),
                  pl.BlockSpec((None, n_new, LANES), lambda b: (0, b, 0)),
                  pl.BlockSpec(w_uk.shape, const), pl.BlockSpec(w_uv.shape, const),
                  pl.BlockSpec((1, LANES), const)],
        out_specs=pl.BlockSpec((None, n_new, H * LANES), lambda b: (0, b, 0)),
        out_shape=jax.ShapeDtypeStruct((1, BT, H * LANES), BF16),
        scratch_shapes=[pltpu.VMEM((Lp, lat), BF16), pltpu.VMEM((Lp, LANES), BF16),
                        pltpu.VMEM((H * n_new, Lp), BF16)],
        compiler_params=_params("arbitrary"),
        name="attention_cached",
    )(q, cache_lat, cache_kr_pad, c_new, kr_new_pad, w_uk, w_uv, g_k_nope.reshape(1, LANES))


def _merge_kernel(a_ref, b_ref, ga_ref, gb_ref, wpa_ref, wpb_ref, o_ref):
    pa = jnp.dot(a_ref[...], wpa_ref[...], preferred_element_type=F32)
    pb = jnp.dot(b_ref[...], wpb_ref[...], preferred_element_type=F32)
    o_ref[...] = (ga_ref[...].astype(F32) * pa + gb_ref[...].astype(F32) * pb).astype(o_ref.dtype)


def merge_branches(o_sg, o_mla, gates, w_pa, w_pb, ts, tn):
    G, S, W = o_sg.shape
    D = w_pa.shape[1]
    nj = D // tn
    row = lambda g, s, j: (g, s, 0)
    return pl.pallas_call(
        _merge_kernel,
        grid=(G, S // ts, nj),
        in_specs=[pl.BlockSpec((None, ts, W), row),
                  pl.BlockSpec((None, ts, o_mla.shape[-1]), row),
                  pl.BlockSpec((None, ts, tn), lambda g, s, j: (g, s, j)),
                  pl.BlockSpec((None, ts, tn), lambda g, s, j: (g, s, j + nj)),
                  pl.BlockSpec((W, tn), lambda g, s, j: (0, j)),
                  pl.BlockSpec((o_mla.shape[-1], tn), lambda g, s, j: (0, j))],
        out_specs=pl.BlockSpec((None, ts, tn), lambda g, s, j: (g, s, j)),
        out_shape=jax.ShapeDtypeStruct((G, S, D), BF16),
        compiler_params=_params("arbitrary", "arbitrary", "arbitrary"),
        name="merge_branches",
    )(o_sg, o_mla, gates, gates, w_pa, w_pb)


def _out_proj_kernel(m_ref, wo_ref, x_ref, g1_ref, gn_ref, sc_ref, sh_ref, x1_ref, h2_ref):
    y = jnp.dot(m_ref[...], wo_ref[...], preferred_element_type=F32)
    x1 = x_ref[...] + g1_ref[...] * y
    x1_ref[...] = x1
    h2_ref[...] = (_rms(x1, gn_ref[...]) * (1.0 + sc_ref[...]) + sh_ref[...]).astype(h2_ref.dtype)


def out_project(m, w_o, x, g1, g_norm2, sc2, sh2, ts):
    G, S, D = x.shape
    row = pl.BlockSpec((None, ts, D), lambda g, s: (g, s, 0))
    return pl.pallas_call(
        _out_proj_kernel,
        grid=(G, S // ts),
        in_specs=[row, pl.BlockSpec(w_o.shape, lambda g, s: (0, 0)), row,
                  _mod_spec(g1, ts), pl.BlockSpec((1, D), lambda g, s: (0, 0)),
                  _mod_spec(sc2, ts), _mod_spec(sh2, ts)],
        out_specs=[row, row],
        out_shape=[jax.ShapeDtypeStruct((G, S, D), F32), jax.ShapeDtypeStruct((G, S, D), BF16)],
        compiler_params=_params("arbitrary", "arbitrary"),
        name="out_project",
    )(m, w_o, x, g1, g_norm2.reshape(1, D), sc2, sh2)


def _mlp_kernel(h_ref, wup_ref, wdn_ref, x1_ref, g2_ref, o_ref):
    j = pl.program_id(2)
    hid = jnp.dot(h_ref[...], wup_ref[...], preferred_element_type=F32)
    hid = jnp.square(jnp.maximum(hid, 0.0)).astype(BF16)
    part = jnp.dot(hid, wdn_ref[...], preferred_element_type=F32)

    @pl.when(j == 0)
    def _():
        o_ref[...] = part

    @pl.when(j > 0)
    def _():
        o_ref[...] += part

    @pl.when(j == pl.num_programs(2) - 1)
    def _():
        o_ref[...] = x1_ref[...] + g2_ref[...] * o_ref[...]


def mlp_residual(h2, w_up, w_down, x1, g2, ts, th):
    G, S, D = x1.shape
    hidden = w_up.shape[1]
    row = pl.BlockSpec((None, ts, D), lambda g, s, j: (g, s, 0))
    return pl.pallas_call(
        _mlp_kernel,
        grid=(G, S // ts, hidden // th),
        in_specs=[row,
                  pl.BlockSpec((D, th), lambda g, s, j: (0, j)),
                  pl.BlockSpec((th, D), lambda g, s, j: (j, 0)),
                  row, _mod_spec(g2, ts)],
        out_specs=row,
        out_shape=jax.ShapeDtypeStruct((G, S, D), F32),
        compiler_params=_params("arbitrary", "arbitrary", "arbitrary"),
        name="mlp_residual",
    )(h2, w_up, w_down, x1, g2)


def _rope_tables(pos, n_rope):
    inv = jnp.float32(ROPE_BASE) ** (-jnp.arange(0, n_rope, 2, dtype=F32) / n_rope)
    ang = pos.astype(F32)[:, None] * inv[None, :]
    cos, sin = jnp.cos(ang), jnp.sin(ang)
    pad = jnp.zeros((pos.shape[0], LANES - n_rope), F32)
    return (jnp.concatenate([cos, cos, pad], axis=1), jnp.concatenate([-sin, sin, pad], axis=1))


def _dup_lanes(g):
    return jnp.concatenate([g.astype(F32), g.astype(F32)]).reshape(1, 2 * g.shape[0])


def _layer(x, mods, pos, P, ts, *, cache=None):
    sh1, sc1, g1, sh2, sc2, g2 = mods
    n_rope = P["n_rope"]
    tabs = _rope_tables(pos, n_rope)
    h = prenorm(x, P["g_norm1"], sc1, sh1, ts)
    u = proj_act(h, P["w_u"], "gelu", ts, P["w_u"].shape[1])
    gates = proj_act(h, P["w_g"], "sigmoid", ts, P["w_g"].shape[1] // 2)
    q = q_project(h, P["w_q"], P["g_q_a"], P["w_uq"], P["g_q_nope"], P["g_q_rope"], tabs, ts,
                  P["q_scale"])
    if cache is None:
        v = proj_gelu_norm(h, P["w_v"], P["g_sg"], ts, BF16)
        o_sg = spatial_gate(u, v, P["w_s"], P["b_rows"], min(1024, x.shape[1]))
        c_kv, k_rope, k, vv = kv_project(h, P["w_kv"], P["g_kv_a"], P["g_k_rope"], tabs, ts, n_rope,
                                         expand_weights=(P["w_uk"], P["w_uv"], P["g_k_nope"]))
        o_mla = attention_prompt(q, k, vv, tq=min(2048, x.shape[1]), tk=min(1024, x.shape[1]),
                                 td=min(512, x.shape[1]))
        extra = ()
    else:
        cache_lat, cache_kr, B, T = cache
        v = proj_gelu_norm(h, P["w_v"], P["g_sg"], ts, F32)
        W = v.shape[-1]
        o_sg = spatial_gate_open(u.reshape(B, T, W), v.reshape(B, T, W), P["w_lanes"][:T, :T],
                                 P["b_rows"][:T]).reshape(1, B * T, W)
        c_kv, k_rope, kr_pad = kv_project(h, P["w_kv"], P["g_kv_a"], P["g_k_rope"], tabs, ts, n_rope)
        cache_kr_pad = jnp.pad(cache_kr, ((0, 0), (0, 0), (0, LANES - n_rope)))
        o_mla = attention_cached(q, cache_lat, cache_kr_pad, c_kv, kr_pad, P["w_uk"], P["w_uv"],
                                 P["g_k_nope"], T)
        extra = (v,)
    m = merge_branches(o_sg, o_mla, gates, P["w_pa"], P["w_pb"], ts, P["w_pa"].shape[1] // 2)
    x1, h2 = out_project(m, P["w_o"], x, g1, P["g_norm2"], sc2, sh2, ts)
    y = mlp_residual(h2, P["w_up"], P["w_down"], x1, g2, ts, min(2048, P["w_up"].shape[1]))
    return (y, c_kv, k_rope) + extra


def kernel(x_prompt, x_sample, cache_kv_latent, cache_k_rope, c_prompt, c_sample, w_ada, b_ada, g_norm1, g_norm2, w_in, g_sg, w_s, b_s, g_q_a, w_uq, g_q_nope, g_q_rope, g_kv_a, g_k_rope, w_uk, g_k_nope, w_uv, w_pa, w_pb, w_o, w_up, w_down):
    depth = w_in.shape[0]
    Bp, S, D = x_prompt.shape
    Bs, T, _ = x_sample.shape
    past = cache_kv_latent.shape[2]
    q_lora = g_q_a.shape[-1]
    lat = g_kv_a.shape[-1]
    n_nope = g_q_nope.shape[-1]
    n_rope = g_q_rope.shape[-1]
    H = w_uk.shape[2]
    sg_w = g_sg.shape[-1]
    off_q = 2 * sg_w
    off_kv = off_q + q_lora
    off_gate = off_kv + lat + n_rope
    assert n_nope == LANES and w_uv.shape[-1] == LANES and 2 * n_rope == LANES
    assert sg_w // SG_GROUPS == LANES and H == N_HEADS

    nb = Bp + Bs
    nb_pad = -(-nb // 8) * 8
    c_all = jnp.concatenate([c_prompt, c_sample, jnp.zeros((nb_pad - nb, D), F32)], axis=0)

    y_p, y_s = x_prompt, x_sample.reshape(1, Bs * T, D)
    outs = [[] for _ in range(5)]
    for l in range(depth):
        wi = w_in[l].astype(BF16)
        w_uq_l = w_uq[l].astype(BF16).reshape(q_lora, H, n_nope + n_rope)
        w_uq_pad = jnp.concatenate([w_uq_l, w_uq_l[:, :, n_nope:]], axis=2)
        P = {
            "n_rope": n_rope,
            "q_scale": float((n_nope + n_rope) ** -0.5 * LOG2E),
            "g_norm1": g_norm1[l], "g_norm2": g_norm2[l], "g_sg": g_sg[l],
            "w_u": wi[:, :sg_w], "w_v": wi[:, sg_w:off_q], "w_q": wi[:, off_q:off_kv],
            "w_kv": jnp.concatenate([wi[:, off_kv:off_gate], wi[:, off_kv + lat:off_gate]], axis=1),
            "w_g": wi[:, off_gate:],
            "g_q_a": g_q_a[l], "w_uq": w_uq_pad.reshape(q_lora, H * HEAD_PAD),
            "g_q_nope": g_q_nope[l], "g_q_rope": _dup_lanes(g_q_rope[l]),
            "g_kv_a": g_kv_a[l], "g_k_rope": _dup_lanes(g_k_rope[l]),
            "w_uk": w_uk[l].astype(BF16).reshape(lat, H * n_nope),
            "w_uv": w_uv[l].astype(BF16).reshape(lat, H * LANES),
            "g_k_nope": g_k_nope[l],
            "w_s": w_s[l],
            "b_rows": jnp.repeat(b_s[l].T, LANES, axis=1),
            "w_lanes": jnp.repeat(w_s[l][:, :T, :T].transpose(1, 2, 0), LANES, axis=2),
            "w_pa": w_pa[l].astype(BF16), "w_pb": w_pb[l].astype(BF16), "w_o": w_o[l].astype(BF16),
            "w_up": w_up[l].astype(BF16), "w_down": w_down[l].astype(BF16),
        }
        mod = ada_project(c_all, w_ada[l], b_ada[l])
        mods_p = [a.reshape(Bp, 1, D) for a in jnp.split(mod[:Bp], 6, axis=-1)]
        mods_s = [jnp.repeat(a, T, axis=0).reshape(1, Bs * T, D)
                  for a in jnp.split(mod[Bp:nb], 6, axis=-1)]

        ts_p = min(512, S)
        y_p, lp, kp = _layer(y_p, mods_p, jnp.arange(S), P, ts_p)
        pos_s = jnp.tile(past + jnp.arange(T), Bs)
        y_s, ls, ks, vs = _layer(y_s, mods_s, pos_s, P, Bs * T,
                                 cache=(cache_kv_latent[l], cache_k_rope[l], Bs, T))
        for lst, a in zip(outs, (lp, kp, ls.reshape(Bs, T, lat), ks.reshape(Bs, T, n_rope),
                                 vs.reshape(Bs, T, sg_w))):
            lst.append(a)
    return (y_p, y_s.reshape(Bs, T, D)) + tuple(jnp.stack(o) for o in outs)
```

```python
import functools
import math

import jax
import jax.numpy as jnp
import numpy as np
from jax import lax
from jax.experimental import pallas as pl
from jax.experimental.pallas import tpu as pltpu

F32 = jnp.float32
BF16 = jnp.bfloat16

EPS = 1e-6
ROPE_BASE = 10000.0
N_HEADS = 16
CHUNK = 64
SG_CHUNK = 128
SG_GROUPS = 16
LANES = 128
HEAD_PAD = 256
LOG2E = math.log2(math.e)
BOUND_MAX = 50.0

VMEM_LIMIT = 58 * 1024 * 1024


def _params(*sem):
    return pltpu.CompilerParams(dimension_semantics=sem, vmem_limit_bytes=VMEM_LIMIT)


def _rms(x, g):
    ms = jnp.mean(x * x, axis=-1, keepdims=True)
    return x * lax.rsqrt(ms + EPS) * g


def _rms_rope_group(t, g_dup):
    ms = jnp.mean(t * t, axis=-1, keepdims=True)
    return t * lax.rsqrt(ms + EPS) * g_dup


def _rope_group(t, tc, ts):
    return t * tc + pltpu.roll(t, LANES - LANES // 4, 1) * ts


def _gelu(z):
    return 0.5 * z * (1.0 + lax.erf(z * np.float32(math.sqrt(0.5))))


def _mod_spec(mod, ts):
    d = mod.shape[-1]
    if mod.shape[1] == 1:
        return pl.BlockSpec((None, 1, d), lambda g, s, *_: (g, 0, 0))
    return pl.BlockSpec((None, ts, d), lambda g, s, *_: (g, s, 0))


def _ada_kernel(c_ref, w_ref, b_ref, o_ref):
    c = c_ref[...]
    s = (c * jax.nn.sigmoid(c)).astype(BF16)
    o_ref[...] = jnp.dot(s, w_ref[...].astype(BF16), preferred_element_type=F32) + b_ref[...]


def ada_project(c, w_ada, b_ada, tn=1024):
    r, d = c.shape
    n = w_ada.shape[1]
    return pl.pallas_call(
        _ada_kernel,
        grid=(n // tn,),
        in_specs=[pl.BlockSpec((r, d), lambda j: (0, 0)),
                  pl.BlockSpec((d, tn), lambda j: (0, j)),
                  pl.BlockSpec((1, tn), lambda j: (0, j))],
        out_specs=pl.BlockSpec((r, tn), lambda j: (0, j)),
        out_shape=jax.ShapeDtypeStruct((r, n), F32),
        compiler_params=_params("arbitrary"),
        name="ada_project",
    )(c, w_ada, b_ada.reshape(1, n))


def _prenorm_kernel(x_ref, g_ref, sc_ref, sh_ref, o_ref):
    y = _rms(x_ref[...], g_ref[...])
    o_ref[...] = (y * (1.0 + sc_ref[...]) + sh_ref[...]).astype(o_ref.dtype)


def prenorm(x, g, sc, sh, ts):
    G, S, D = x.shape
    return pl.pallas_call(
        _prenorm_kernel,
        grid=(G, S // ts),
        in_specs=[pl.BlockSpec((None, ts, D), lambda g_, s: (g_, s, 0)),
                  pl.BlockSpec((1, D), lambda g_, s: (0, 0)),
                  _mod_spec(sc, ts), _mod_spec(sh, ts)],
        out_specs=pl.BlockSpec((None, ts, D), lambda g_, s: (g_, s, 0)),
        out_shape=jax.ShapeDtypeStruct((G, S, D), BF16),
        compiler_params=_params("arbitrary", "arbitrary"),
        name="prenorm",
    )(x, g.reshape(1, D), sc, sh)


def _proj_act_kernel(h_ref, w_ref, o_ref, *, act):
    z = jnp.dot(h_ref[...], w_ref[...], preferred_element_type=F32)
    if act == "gelu":
        a = _gelu(z)
    else:
        a = jax.nn.sigmoid(z)
    o_ref[...] = a.astype(o_ref.dtype)


def proj_act(h, w, act, ts, tn):
    G, S, D = h.shape
    n = w.shape[1]
    return pl.pallas_call(
        functools.partial(_proj_act_kernel, act=act),
        grid=(G, S // ts, n // tn),
        in_specs=[pl.BlockSpec((None, ts, D), lambda g, s, j: (g, s, 0)),
                  pl.BlockSpec((D, tn), lambda g, s, j: (0, j))],
        out_specs=pl.BlockSpec((None, ts, tn), lambda g, s, j: (g, s, j)),
        out_shape=jax.ShapeDtypeStruct((G, S, n), BF16),
        compiler_params=_params("arbitrary", "arbitrary", "arbitrary"),
        name="proj_" + act,
    )(h, w)


def _proj_gelu_norm_kernel(h_ref, w_ref, g_ref, o_ref):
    z = jnp.dot(h_ref[...], w_ref[...], preferred_element_type=F32)
    o_ref[...] = _rms(_gelu(z), g_ref[...]).astype(o_ref.dtype)


def proj_gelu_norm(h, w, g, ts, out_dtype):
    G, S, D = h.shape
    n = w.shape[1]
    return pl.pallas_call(
        _proj_gelu_norm_kernel,
        grid=(G, S // ts),
        in_specs=[pl.BlockSpec((None, ts, D), lambda g_, s: (g_, s, 0)),
                  pl.BlockSpec((D, n), lambda g_, s: (0, 0)),
                  pl.BlockSpec((1, n), lambda g_, s: (0, 0))],
        out_specs=pl.BlockSpec((None, ts, n), lambda g_, s: (g_, s, 0)),
        out_shape=jax.ShapeDtypeStruct((G, S, n), out_dtype),
        compiler_params=_params("arbitrary", "arbitrary"),
        name="proj_gelu_norm",
    )(h, w, g.reshape(1, n))


def _q_proj_kernel(h_ref, wq_ref, gqa_ref, wuq_ref, gn_ref, gr_ref, tc_ref, ts_ref, q_ref, *, q_scale):
    zq = jnp.dot(h_ref[...], wq_ref[...], preferred_element_type=F32)
    zn = _rms(zq, gqa_ref[...]).astype(BF16)
    tc, ts = tc_ref[...], ts_ref[...]
    gains = jnp.concatenate([gn_ref[...], gr_ref[...]], axis=1) * q_scale
    same_tile = (lax.broadcasted_iota(jnp.int32, (HEAD_PAD, HEAD_PAD), 0) // LANES
                 == lax.broadcasted_iota(jnp.int32, (HEAD_PAD, HEAD_PAD), 1) // LANES)
    tile_mean = jnp.where(same_tile, 1.0 / LANES, 0.0).astype(BF16)
    shift_lane = jnp.where(lax.broadcasted_iota(jnp.int32, (1, LANES), 1) == LANES // 2, 1.0, 0.0)
    for pair in range(q_ref.shape[0] // 2):
        blk2 = jnp.dot(zn, wuq_ref[:, pair * 2 * HEAD_PAD:(pair + 1) * 2 * HEAD_PAD],
                       preferred_element_type=F32)
        for sub in range(2):
            hh = 2 * pair + sub
            blk = blk2[:, sub * HEAD_PAD:(sub + 1) * HEAD_PAD]
            ms = jnp.dot((blk * blk).astype(BF16), tile_mean, preferred_element_type=F32)
            y = blk * lax.rsqrt(ms + EPS) * gains
            q_ref[hh, :, :LANES] = y[:, :LANES].astype(q_ref.dtype)
            q_ref[hh, :, LANES:] = (_rope_group(y[:, LANES:], tc, ts) + shift_lane).astype(q_ref.dtype)


def q_project(h, w_q, g_q_a, w_uq_pad, g_nope, g_rope_pad, tabs, ts, q_scale):
    G, S, D = h.shape
    ql = w_q.shape[1]
    H = w_uq_pad.shape[1] // HEAD_PAD
    const = lambda g, s: (0, 0)
    tab_spec = pl.BlockSpec((ts, LANES), lambda g, s: (s, 0))
    return pl.pallas_call(
        functools.partial(_q_proj_kernel, q_scale=q_scale),
        grid=(G, S // ts),
        in_specs=[pl.BlockSpec((None, ts, D), lambda g, s: (g, s, 0)),
                  pl.BlockSpec((D, ql), const),
                  pl.BlockSpec((1, ql), const),
                  pl.BlockSpec((ql, H * HEAD_PAD), const),
                  pl.BlockSpec((1, LANES), const),
                  pl.BlockSpec((1, LANES), const),
                  tab_spec, tab_spec],
        out_specs=pl.BlockSpec((None, H, ts, HEAD_PAD), lambda g, s: (g, 0, s, 0)),
        out_shape=jax.ShapeDtypeStruct((G, H, S, HEAD_PAD), BF16),
        compiler_params=_params("arbitrary", "arbitrary"),
        name="q_project",
    )(h, w_q, g_q_a.reshape(1, ql), w_uq_pad, g_nope.reshape(1, LANES), g_rope_pad, *tabs)


def _kv_proj_kernel(h_ref, wkv_ref, gkva_ref, gkr_ref, tc_ref, ts_ref, *rest, n_rope, expand):
    if expand:
        wuk_ref, wuv_ref, gkn_ref, kshift_ref, c_ref, kr_ref, k_ref, v_ref = rest
    else:
        c_ref, kr_ref, krp_ref = rest
    lat = c_ref.shape[-1]
    z = jnp.dot(h_ref[...], wkv_ref[...], preferred_element_type=F32)
    c = _rms(z[:, :lat], gkva_ref[...])
    c_ref[...] = c
    rope = _rope_group(_rms_rope_group(z[:, lat:], gkr_ref[...]), tc_ref[...], ts_ref[...])
    kr_ref[...] = rope[:, :n_rope]
    rope_b = rope.astype(BF16)
    if not expand:
        krp_ref[...] = rope_b
        return
    cb = c.astype(BF16)
    gkn = gkn_ref[...]
    rope_b = (rope + kshift_ref[...]).astype(BF16)
    for pair in range(k_ref.shape[0] // 2):
        cols = slice(pair * 2 * LANES, (pair + 1) * 2 * LANES)
        kn2 = jnp.dot(cb, wuk_ref[:, cols], preferred_element_type=F32)
        v2 = jnp.dot(cb, wuv_ref[:, cols], preferred_element_type=F32)
        for sub in range(2):
            hh = 2 * pair + sub
            lanes = slice(sub * LANES, (sub + 1) * LANES)
            k_ref[hh, :, :LANES] = _rms(kn2[:, lanes], gkn).astype(k_ref.dtype)
            k_ref[hh, :, LANES:] = rope_b
            v_ref[hh] = v2[:, lanes].astype(v_ref.dtype)


def kv_project(h, w_kv, g_kv_a, g_k_rope_pad, tabs, ts, n_rope, expand_weights=None):
    G, S, D = h.shape
    lat = g_kv_a.shape[-1]
    const = lambda g, s: (0, 0)
    tab_spec = pl.BlockSpec((ts, LANES), lambda g, s: (s, 0))
    in_specs = [pl.BlockSpec((None, ts, D), lambda g, s: (g, s, 0)),
                pl.BlockSpec((D, lat + LANES), const),
                pl.BlockSpec((1, lat), const),
                pl.BlockSpec((1, LANES), const),
                tab_spec, tab_spec]
    args = [h, w_kv, g_kv_a.reshape(1, lat), g_k_rope_pad, *tabs]
    out_specs = [pl.BlockSpec((None, ts, lat), lambda g, s: (g, s, 0)),
                 pl.BlockSpec((None, ts, n_rope), lambda g, s: (g, s, 0))]
    out_shape = [jax.ShapeDtypeStruct((G, S, lat), F32),
                 jax.ShapeDtypeStruct((G, S, n_rope), F32)]
    expand = expand_weights is not None
    if expand:
        w_uk, w_uv, g_k_nope, k_shift = expand_weights
        H = w_uk.shape[1] // LANES
        in_specs += [pl.BlockSpec(w_uk.shape, const), pl.BlockSpec(w_uv.shape, const),
                     pl.BlockSpec((1, LANES), const), pl.BlockSpec((1, LANES), const)]
        args += [w_uk, w_uv, g_k_nope.reshape(1, LANES), k_shift]
        out_specs += [pl.BlockSpec((None, H, ts, HEAD_PAD), lambda g, s: (g, 0, s, 0)),
                      pl.BlockSpec((None, H, ts, LANES), lambda g, s: (g, 0, s, 0))]
        out_shape += [jax.ShapeDtypeStruct((G, H, S, HEAD_PAD), BF16),
                      jax.ShapeDtypeStruct((G, H, S, LANES), BF16)]
    else:
        out_specs.append(pl.BlockSpec((None, ts, LANES), lambda g, s: (g, s, 0)))
        out_shape.append(jax.ShapeDtypeStruct((G, S, LANES), BF16))
    return pl.pallas_call(
        functools.partial(_kv_proj_kernel, n_rope=n_rope, expand=expand),
        grid=(G, S // ts),
        in_specs=in_specs,
        out_specs=out_specs,
        out_shape=out_shape,
        compiler_params=_params("arbitrary", "arbitrary"),
        name="kv_project",
    )(*args)


def _spatial_gate_kernel(u_ref, v_ref, w_ref, b_ref, o_ref, wm_sc, *, n_chunks):
    row = lax.broadcasted_iota(jnp.int32, (SG_CHUNK, SG_CHUNK), 0)
    col = lax.broadcasted_iota(jnp.int32, (SG_CHUNK, SG_CHUNK), 1)
    tril = col <= row
    n_groups = w_ref.shape[0]
    for g in range(n_groups):
        wm_sc[g] = jnp.where(tril, w_ref[g], 0.0).astype(BF16)

    def chunk_body(n, carry):
        rows = pl.ds(pl.multiple_of(n * SG_CHUNK, SG_CHUNK), SG_CHUNK)
        for g in range(n_groups):
            lanes = slice(g * LANES, (g + 1) * LANES)
            mix = jnp.dot(wm_sc[g], v_ref[rows, lanes], preferred_element_type=F32) + b_ref[:, lanes]
            o_ref[rows, lanes] = (u_ref[rows, lanes].astype(F32) * mix).astype(o_ref.dtype)
        return carry

    lax.fori_loop(0, n_chunks, chunk_body, 0)


def spatial_gate(u, v, w_s, bias_rows, ts):
    G, S, W = u.shape
    blk = pl.BlockSpec((None, ts, W), lambda g, s: (g, s, 0))
    return pl.pallas_call(
        functools.partial(_spatial_gate_kernel, n_chunks=ts // SG_CHUNK),
        grid=(G, S // ts),
        in_specs=[blk, blk,
                  pl.BlockSpec(w_s.shape, lambda g, s: (0, 0, 0)),
                  pl.BlockSpec(bias_rows.shape, lambda g, s: (0, 0))],
        out_specs=blk,
        out_shape=jax.ShapeDtypeStruct((G, S, W), BF16),
        scratch_shapes=[pltpu.VMEM(w_s.shape, BF16)],
        compiler_params=_params("arbitrary", "arbitrary"),
        name="spatial_gate",
    )(u, v, w_s, bias_rows)


def _spatial_gate_open_kernel(u_ref, v_ref, wl_ref, b_ref, o_ref):
    T = u_ref.shape[1]
    for i in range(T):
        acc = b_ref[i:i + 1, :] + wl_ref[i, 0:1, :] * v_ref[:, 0, :]
        for j in range(1, i + 1):
            acc = acc + wl_ref[i, j:j + 1, :] * v_ref[:, j, :]
        o_ref[:, i, :] = (u_ref[:, i, :].astype(F32) * acc).astype(o_ref.dtype)


def spatial_gate_open(u, v, w_lanes, bias_rows):
    B, T, W = u.shape
    full = lambda a: pl.BlockSpec(a.shape, lambda i: (0,) * a.ndim)
    return pl.pallas_call(
        _spatial_gate_open_kernel,
        grid=(1,),
        in_specs=[full(u), full(v), full(w_lanes), full(bias_rows)],
        out_specs=pl.BlockSpec((B, T, W), lambda i: (0, 0, 0)),
        out_shape=jax.ShapeDtypeStruct((B, T, W), BF16),
        compiler_params=_params("arbitrary"),
        name="spatial_gate_open",
    )(u, v, w_lanes, bias_rows)


def _attn_kernel(bounded_ref, q_ref, k_ref, v_ref, o_ref, m_sc, acc_sc, *, tq, tk, td):
    i = pl.program_id(2)
    dv = v_ref.shape[-1]
    unroll = tq // tk
    nt = (((1,), (1,)), ((), ()))
    ones = jnp.ones((tk, LANES), BF16)
    d0 = pl.multiple_of(i * tq, tq)
    chunk_mask = (lax.broadcasted_iota(jnp.int32, (td, td), 1) // CHUNK
                  <= lax.broadcasted_iota(jnp.int32, (td, td), 0) // CHUNK)

    def scores(rows0, k0, keys):
        return lax.dot_general(q_ref[rows0:, :], k_ref[pl.ds(k0, keys), :], nt, preferred_element_type=F32)

    def diag_scores(c):
        r0 = c * td
        s = scores(r0, d0 + r0, td)
        top = jnp.where(chunk_mask, s[:td], -jnp.inf)
        return top if r0 + td == tq else jnp.concatenate([top, s[td:]], axis=0)

    def values(k0, keys):
        return jnp.concatenate([v_ref[pl.ds(k0, keys), :], ones[:keys]], axis=1)

    def finish():
        acc = acc_sc[...]
        o_ref[...] = (acc[:, :dv] / acc[:, dv:]).astype(o_ref.dtype)

    @pl.when(bounded_ref[0] == 1)
    def _():
        for c in range(tq // td):
            r0 = c * td
            pv = jnp.dot(jnp.exp2(diag_scores(c)).astype(BF16), values(d0 + r0, td),
                         preferred_element_type=F32)
            if c == 0:
                acc_sc[...] = pv
            else:
                acc_sc[r0:, :] += pv

        def body(j, carry):
            k0 = pl.multiple_of(j * tq, tq)
            acc = acc_sc[...]
            for u in range(unroll):
                p = jnp.exp2(scores(0, k0 + u * tk, tk)).astype(BF16)
                acc = acc + jnp.dot(p, values(k0 + u * tk, tk), preferred_element_type=F32)
            acc_sc[...] = acc
            return carry

        lax.fori_loop(0, i, body, 0)
        finish()

    @pl.when(bounded_ref[0] != 1)
    def _():
        def step(s, k0, m_old, acc_old):
            rows, keys = s.shape
            row_max = jnp.max(s, axis=-1, keepdims=True)
            m_new = (jnp.broadcast_to(row_max, (rows, LANES)) if m_old is None
                     else jnp.maximum(m_old, row_max))
            p = jnp.concatenate([jnp.exp2(s[:, c * LANES:(c + 1) * LANES] - m_new)
                                 for c in range(keys // LANES)], axis=1).astype(BF16)
            pv = jnp.dot(p, values(k0, keys), preferred_element_type=F32)
            if m_old is None:
                return m_new, pv
            alpha = jnp.exp2(m_old - m_new)
            return m_new, jnp.concatenate([alpha] * ((dv + LANES) // LANES), axis=1) * acc_old + pv

        for c in range(tq // td):
            r0 = c * td
            if c == 0:
                m, acc = step(diag_scores(c), d0, None, None)
            else:
                m, acc = step(diag_scores(c), d0 + r0, m_sc[r0:, :], acc_sc[r0:, :])
            m_sc[r0:, :] = m
            acc_sc[r0:, :] = acc

        def body(j, carry):
            k0 = pl.multiple_of(j * tq, tq)
            ss = [scores(0, k0 + u * tk, tk) for u in range(unroll)]
            m, acc = m_sc[...], acc_sc[...]
            for u in range(unroll):
                m, acc = step(ss[u], k0 + u * tk, m, acc)
            m_sc[...] = m
            acc_sc[...] = acc
            return carry

        lax.fori_loop(0, i, body, 0)
        finish()


def attention_prompt(bounded, q, k, v, tq, tk, td):
    G, H, S, _ = q.shape
    dv = v.shape[-1]
    assert dv == LANES and tq % tk == 0 and tq % td == 0 and td % LANES == 0 and td <= tk
    return pl.pallas_call(
        functools.partial(_attn_kernel, tq=tq, tk=tk, td=td),
        grid=(G, H, S // tq),
        in_specs=[pl.BlockSpec(memory_space=pltpu.SMEM),
                  pl.BlockSpec((None, None, tq, HEAD_PAD), lambda g, h, i: (g, h, i, 0)),
                  pl.BlockSpec((None, None, S, HEAD_PAD), lambda g, h, i: (g, h, 0, 0)),
                  pl.BlockSpec((None, None, S, dv), lambda g, h, i: (g, h, 0, 0))],
        out_specs=pl.BlockSpec((None, tq, dv), lambda g, h, i: (g, i, h)),
        out_shape=jax.ShapeDtypeStruct((G, S, H * dv), BF16),
        scratch_shapes=[pltpu.VMEM((tq, LANES), F32), pltpu.VMEM((tq, dv + LANES), F32)],
        compiler_params=_params("arbitrary", "arbitrary", "arbitrary"),
        name="attention_prompt",
    )(bounded, q, k, v)


def _attn_cached_kernel(q_ref, cache_ref, ckr_ref, cnew_ref, krnew_ref, wuk_ref, wuv_ref, gkn_ref,
                        o_ref, call_sc, krall_sc, p_sc, *, past, n_new):
    H = q_ref.shape[0]
    L = past + n_new
    Lp = call_sc.shape[0]
    lat = call_sc.shape[1]
    call_sc[0:past, :] = cache_ref[...].astype(BF16)
    call_sc[past:L, :] = cnew_ref[...].astype(BF16)
    call_sc[L:Lp, :] = jnp.zeros((Lp - L, lat), BF16)
    krall_sc[0:past, :] = ckr_ref[...].astype(BF16)
    krall_sc[past:L, :] = krnew_ref[...]
    krall_sc[L:Lp, :] = jnp.zeros((Lp - L, LANES), BF16)
    call = call_sc[...]
    krall = krall_sc[...]
    gkn = gkn_ref[...]
    valid = lax.broadcasted_iota(jnp.int32, (n_new, Lp), 1) < L
    nt = (((1,), (1,)), ((), ()))
    group = 4
    for grp in range(H // group):
        cols = slice(grp * group * LANES, (grp + 1) * group * LANES)
        kn_g = jnp.dot(call, wuk_ref[:, cols], preferred_element_type=F32)
        for sub in range(group):
            hh = group * grp + sub
            kn = _rms(kn_g[:, sub * LANES:(sub + 1) * LANES], gkn).astype(BF16)
            kh = jnp.concatenate([kn, krall], axis=1)
            s = lax.dot_general(q_ref[hh], kh, nt, preferred_element_type=F32)
            s = jnp.where(valid, s, -jnp.inf)
            m = jnp.max(s, axis=-1, keepdims=True)
            p = jnp.exp2(s - m)
            p = p / jnp.sum(p, axis=-1, keepdims=True)
            p_sc[hh * n_new:(hh + 1) * n_new, :] = p.astype(BF16)
    o_lat = jnp.dot(p_sc[...], call, preferred_element_type=F32).astype(BF16)
    for hh in range(H):
        o_ref[:, hh * LANES:(hh + 1) * LANES] = jnp.dot(
            o_lat[hh * n_new:(hh + 1) * n_new, :], wuv_ref[:, hh * LANES:(hh + 1) * LANES],
            preferred_element_type=F32).astype(o_ref.dtype)


def attention_cached(q, cache_lat, cache_kr_pad, c_new, kr_new_pad, w_uk, w_uv, g_k_nope, n_new):
    _, H, BT, _ = q.shape
    B, past, lat = cache_lat.shape
    L = past + n_new
    Lp = -(-L // LANES) * LANES
    const = lambda b: (0, 0)
    return pl.pallas_call(
        functools.partial(_attn_cached_kernel, past=past, n_new=n_new),
        grid=(B,),
        in_specs=[pl.BlockSpec((None, H, n_new, HEAD_PAD), lambda b: (0, 0, b, 0)),
                  pl.BlockSpec((None, past, lat), lambda b: (b, 0, 0)),
                  pl.BlockSpec((None, past, LANES), lambda b: (b, 0, 0)),
                  pl.BlockSpec((None, n_new, lat), lambda b: (0, b, 0)),
                  pl.BlockSpec((None, n_new, LANES), lambda b: (0, b, 0)),
                  pl.BlockSpec(w_uk.shape, const), pl.BlockSpec(w_uv.shape, const),
                  pl.BlockSpec((1, LANES), const)],
        out_specs=pl.BlockSpec((None, n_new, H * LANES), lambda b: (0, b, 0)),
        out_shape=jax.ShapeDtypeStruct((1, BT, H * LANES), BF16),
        scratch_shapes=[pltpu.VMEM((Lp, lat), BF16), pltpu.VMEM((Lp, LANES), BF16),
                        pltpu.VMEM((H * n_new, Lp), BF16)],
        compiler_params=_params("arbitrary"),
        name="attention_cached",
    )(q, cache_lat, cache_kr_pad, c_new, kr_new_pad, w_uk, w_uv, g_k_nope.reshape(1, LANES))


def _merge_kernel(a_ref, b_ref, ga_ref, gb_ref, wpa_ref, wpb_ref, o_ref):
    pa = jnp.dot(a_ref[...], wpa_ref[...], preferred_element_type=F32)
    pb = jnp.dot(b_ref[...], wpb_ref[...], preferred_element_type=F32)
    o_ref[...] = (ga_ref[...].astype(F32) * pa + gb_ref[...].astype(F32) * pb).astype(o_ref.dtype)


def merge_branches(o_sg, o_mla, gates, w_pa, w_pb, ts, tn):
    G, S, W = o_sg.shape
    D = w_pa.shape[1]
    nj = D // tn
    row = lambda g, s, j: (g, s, 0)
    return pl.pallas_call(
        _merge_kernel,
        grid=(G, S // ts, nj),
        in_specs=[pl.BlockSpec((None, ts, W), row),
                  pl.BlockSpec((None, ts, o_mla.shape[-1]), row),
                  pl.BlockSpec((None, ts, tn), lambda g, s, j: (g, s, j)),
                  pl.BlockSpec((None, ts, tn), lambda g, s, j: (g, s, j + nj)),
                  pl.BlockSpec((W, tn), lambda g, s, j: (0, j)),
                  pl.BlockSpec((o_mla.shape[-1], tn), lambda g, s, j: (0, j))],
        out_specs=pl.BlockSpec((None, ts, tn), lambda g, s, j: (g, s, j)),
        out_shape=jax.ShapeDtypeStruct((G, S, D), BF16),
        compiler_params=_params("arbitrary", "arbitrary", "arbitrary"),
        name="merge_branches",
    )(o_sg, o_mla, gates, gates, w_pa, w_pb)


def _out_proj_kernel(m_ref, wo_ref, x_ref, g1_ref, gn_ref, sc_ref, sh_ref, x1_ref, h2_ref):
    y = jnp.dot(m_ref[...], wo_ref[...], preferred_element_type=F32)
    x1 = x_ref[...] + g1_ref[...] * y
    x1_ref[...] = x1
    h2_ref[...] = (_rms(x1, gn_ref[...]) * (1.0 + sc_ref[...]) + sh_ref[...]).astype(h2_ref.dtype)


def out_project(m, w_o, x, g1, g_norm2, sc2, sh2, ts):
    G, S, D = x.shape
    row = pl.BlockSpec((None, ts, D), lambda g, s: (g, s, 0))
    return pl.pallas_call(
        _out_proj_kernel,
        grid=(G, S // ts),
        in_specs=[row, pl.BlockSpec(w_o.shape, lambda g, s: (0, 0)), row,
                  _mod_spec(g1, ts), pl.BlockSpec((1, D), lambda g, s: (0, 0)),
                  _mod_spec(sc2, ts), _mod_spec(sh2, ts)],
        out_specs=[row, row],
        out_shape=[jax.ShapeDtypeStruct((G, S, D), F32), jax.ShapeDtypeStruct((G, S, D), BF16)],
        compiler_params=_params("arbitrary", "arbitrary"),
        name="out_project",
    )(m, w_o, x, g1, g_norm2.reshape(1, D), sc2, sh2)


def _mlp_kernel(h_ref, wup_ref, wdn_ref, x1_ref, g2_ref, o_ref):
    j = pl.program_id(2)
    hid = jnp.dot(h_ref[...], wup_ref[...], preferred_element_type=F32)
    hid = jnp.square(jnp.maximum(hid, 0.0)).astype(BF16)
    part = jnp.dot(hid, wdn_ref[...], preferred_element_type=F32)

    @pl.when(j == 0)
    def _():
        o_ref[...] = part

    @pl.when(j > 0)
    def _():
        o_ref[...] += part

    @pl.when(j == pl.num_programs(2) - 1)
    def _():
        o_ref[...] = x1_ref[...] + g2_ref[...] * o_ref[...]


def mlp_residual(h2, w_up, w_down, x1, g2, ts, th):
    G, S, D = x1.shape
    hidden = w_up.shape[1]
    row = pl.BlockSpec((None, ts, D), lambda g, s, j: (g, s, 0))
    return pl.pallas_call(
        _mlp_kernel,
        grid=(G, S // ts, hidden // th),
        in_specs=[row,
                  pl.BlockSpec((D, th), lambda g, s, j: (0, j)),
                  pl.BlockSpec((th, D), lambda g, s, j: (j, 0)),
                  row, _mod_spec(g2, ts)],
        out_specs=row,
        out_shape=jax.ShapeDtypeStruct((G, S, D), F32),
        compiler_params=_params("arbitrary", "arbitrary", "arbitrary"),
        name="mlp_residual",
    )(h2, w_up, w_down, x1, g2)


def _rope_tables(pos, n_rope):
    inv = jnp.float32(ROPE_BASE) ** (-jnp.arange(0, n_rope, 2, dtype=F32) / n_rope)
    ang = pos.astype(F32)[:, None] * inv[None, :]
    cos, sin = jnp.cos(ang), jnp.sin(ang)
    pad = jnp.zeros((pos.shape[0], LANES - n_rope), F32)
    return (jnp.concatenate([cos, cos, pad], axis=1), jnp.concatenate([-sin, sin, pad], axis=1))


def _dup_lanes(g):
    return jnp.concatenate([g.astype(F32), g.astype(F32)]).reshape(1, 2 * g.shape[0])


def _layer(x, mods, pos, P, ts, *, cache=None):
    sh1, sc1, g1, sh2, sc2, g2 = mods
    n_rope = P["n_rope"]
    tabs = _rope_tables(pos, n_rope)
    h = prenorm(x, P["g_norm1"], sc1, sh1, ts)
    u = proj_act(h, P["w_u"], "gelu", ts, P["w_u"].shape[1])
    gates = proj_act(h, P["w_g"], "sigmoid", ts, P["w_g"].shape[1] // 2)
    q = q_project(h, P["w_q"], P["g_q_a"], P["w_uq"], P["g_q_nope"], P["g_q_rope"], tabs, ts,
                  P["q_scale"])
    if cache is None:
        v = proj_gelu_norm(h, P["w_v"], P["g_sg"], ts, BF16)
        o_sg = spatial_gate(u, v, P["w_s"], P["b_rows"], min(1024, x.shape[1]))
        c_kv, k_rope, k, vv = kv_project(h, P["w_kv"], P["g_kv_a"], P["g_k_rope"], tabs, ts, n_rope,
                                         expand_weights=(P["w_uk"], P["w_uv"], P["g_k_nope"], P["k_shift"]))
        o_mla = attention_prompt(P["bounded"], q, k, vv, tq=min(2048, x.shape[1]),
                                 tk=min(1024, x.shape[1]), td=min(512, x.shape[1]))
        extra = ()
    else:
        cache_lat, cache_kr, B, T = cache
        v = proj_gelu_norm(h, P["w_v"], P["g_sg"], ts, F32)
        W = v.shape[-1]
        o_sg = spatial_gate_open(u.reshape(B, T, W), v.reshape(B, T, W), P["w_lanes"][:T, :T],
                                 P["b_rows"][:T]).reshape(1, B * T, W)
        c_kv, k_rope, kr_pad = kv_project(h, P["w_kv"], P["g_kv_a"], P["g_k_rope"], tabs, ts, n_rope)
        cache_kr_pad = jnp.pad(cache_kr, ((0, 0), (0, 0), (0, LANES - n_rope)))
        o_mla = attention_cached(q, cache_lat, cache_kr_pad, c_kv, kr_pad, P["w_uk"], P["w_uv"],
                                 P["g_k_nope"], T)
        extra = (v,)
    m = merge_branches(o_sg, o_mla, gates, P["w_pa"], P["w_pb"], ts, P["w_pa"].shape[1] // 2)
    x1, h2 = out_project(m, P["w_o"], x, g1, P["g_norm2"], sc2, sh2, ts)
    y = mlp_residual(h2, P["w_up"], P["w_down"], x1, g2, ts, min(2048, P["w_up"].shape[1]))
    return (y, c_kv, k_rope) + extra


def kernel(x_prompt, x_sample, cache_kv_latent, cache_k_rope, c_prompt, c_sample, w_ada, b_ada, g_norm1, g_norm2, w_in, g_sg, w_s, b_s, g_q_a, w_uq, g_q_nope, g_q_rope, g_kv_a, g_k_rope, w_uk, g_k_nope, w_uv, w_pa, w_pb, w_o, w_up, w_down):
    depth = w_in.shape[0]
    Bp, S, D = x_prompt.shape
    Bs, T, _ = x_sample.shape
    past = cache_kv_latent.shape[2]
    q_lora = g_q_a.shape[-1]
    lat = g_kv_a.shape[-1]
    n_nope = g_q_nope.shape[-1]
    n_rope = g_q_rope.shape[-1]
    H = w_uk.shape[2]
    sg_w = g_sg.shape[-1]
    off_q = 2 * sg_w
    off_kv = off_q + q_lora
    off_gate = off_kv + lat + n_rope
    assert n_nope == LANES and w_uv.shape[-1] == LANES and 2 * n_rope == LANES
    assert sg_w // SG_GROUPS == LANES and H == N_HEADS

    nb = Bp + Bs
    nb_pad = -(-nb // 8) * 8
    c_all = jnp.concatenate([c_prompt, c_sample, jnp.zeros((nb_pad - nb, D), F32)], axis=0)

    y_p, y_s = x_prompt, x_sample.reshape(1, Bs * T, D)
    outs = [[] for _ in range(5)]
    for l in range(depth):
        wi = w_in[l].astype(BF16)
        w_uq_l = w_uq[l].astype(BF16).reshape(q_lora, H, n_nope + n_rope)
        w_uq_pad = jnp.concatenate([w_uq_l, w_uq_l[:, :, n_nope:]], axis=2)
        q_scale = float((n_nope + n_rope) ** -0.5 * LOG2E)
        def sq_norm_bound(g_nope, g_rope):
            return n_nope * jnp.max(jnp.square(g_nope)) + n_rope * jnp.max(jnp.square(g_rope))
        score_bound = q_scale * jnp.sqrt(sq_norm_bound(g_q_nope[l], g_q_rope[l])
                                         * sq_norm_bound(g_k_nope[l], g_k_rope[l]))
        bounded = (score_bound <= BOUND_MAX).astype(jnp.int32).reshape(1)
        k_shift = jnp.where(jnp.arange(LANES) == LANES // 2, -score_bound, 0.0).astype(F32).reshape(1, LANES)
        P = {
            "n_rope": n_rope,
            "q_scale": q_scale, "k_shift": k_shift, "bounded": bounded,
            "g_norm1": g_norm1[l], "g_norm2": g_norm2[l], "g_sg": g_sg[l],
            "w_u": wi[:, :sg_w], "w_v": wi[:, sg_w:off_q], "w_q": wi[:, off_q:off_kv],
            "w_kv": jnp.concatenate([wi[:, off_kv:off_gate], wi[:, off_kv + lat:off_gate]], axis=1),
            "w_g": wi[:, off_gate:],
            "g_q_a": g_q_a[l], "w_uq": w_uq_pad.reshape(q_lora, H * HEAD_PAD),
            "g_q_nope": g_q_nope[l], "g_q_rope": _dup_lanes(g_q_rope[l]),
            "g_kv_a": g_kv_a[l], "g_k_rope": _dup_lanes(g_k_rope[l]),
            "w_uk": w_uk[l].astype(BF16).reshape(lat, H * n_nope),
            "w_uv": w_uv[l].astype(BF16).reshape(lat, H * LANES),
            "g_k_nope": g_k_nope[l],
            "w_s": w_s[l],
            "b_rows": jnp.repeat(b_s[l].T, LANES, axis=1),
            "w_lanes": jnp.repeat(w_s[l][:, :T, :T].transpose(1, 2, 0), LANES, axis=2),
            "w_pa": w_pa[l].astype(BF16), "w_pb": w_pb[l].astype(BF16), "w_o": w_o[l].astype(BF16),
            "w_up": w_up[l].astype(BF16), "w_down": w_down[l].astype(BF16),
        }
        mod = ada_project(c_all, w_ada[l], b_ada[l])
        mods_p = [a.reshape(Bp, 1, D) for a in jnp.split(mod[:Bp], 6, axis=-1)]
        mods_s = [jnp.repeat(a, T, axis=0).reshape(1, Bs * T, D)
                  for a in jnp.split(mod[Bp:nb], 6, axis=-1)]

        ts_p = min(512, S)
        y_p, lp, kp = _layer(y_p, mods_p, jnp.arange(S), P, ts_p)
        pos_s = jnp.tile(past + jnp.arange(T), Bs)
        y_s, ls, ks, vs = _layer(y_s, mods_s, pos_s, P, Bs * T,
                                 cache=(cache_kv_latent[l], cache_k_rope[l], Bs, T))
        for lst, a in zip(outs, (lp, kp, ls.reshape(Bs, T, lat), ks.reshape(Bs, T, n_rope),
                                 vs.reshape(Bs, T, sg_w))):
            lst.append(a)
    return (y_p, y_s.reshape(Bs, T, D)) + tuple(jnp.stack(o) for o in outs)
```

```python
import functools
import math

import jax
import jax.numpy as jnp
import numpy as np
from jax import lax
from jax.experimental import pallas as pl
from jax.experimental.pallas import tpu as pltpu

F32 = jnp.float32
BF16 = jnp.bfloat16

EPS = 1e-6
ROPE_BASE = 10000.0
N_HEADS = 16
CHUNK = 64
SG_CHUNK = 128
SG_GROUPS = 16
LANES = 128
HEAD_PAD = 256
LOG2E = math.log2(math.e)
BOUND_MAX = 50.0

VMEM_LIMIT = 58 * 1024 * 1024


def _params(*sem):
    return pltpu.CompilerParams(dimension_semantics=sem, vmem_limit_bytes=VMEM_LIMIT)


def _rms(x, g):
    ms = jnp.mean(x * x, axis=-1, keepdims=True)
    return x * lax.rsqrt(ms + EPS) * g


def _rms_rope_group(t, g_dup):
    ms = jnp.mean(t * t, axis=-1, keepdims=True)
    return t * lax.rsqrt(ms + EPS) * g_dup


def _rope_group(t, tc, ts):
    return t * tc + pltpu.roll(t, LANES - LANES // 4, 1) * ts


def _gelu(z):
    return 0.5 * z * (1.0 + lax.erf(z * np.float32(math.sqrt(0.5))))


def _mod_spec(mod, ts):
    d = mod.shape[-1]
    if mod.shape[1] == 1:
        return pl.BlockSpec((None, 1, d), lambda g, s, *_: (g, 0, 0))
    return pl.BlockSpec((None, ts, d), lambda g, s, *_: (g, s, 0))


def _ada_kernel(c_ref, w_ref, b_ref, o_ref):
    c = c_ref[...]
    s = (c * jax.nn.sigmoid(c)).astype(BF16)
    o_ref[...] = jnp.dot(s, w_ref[...].astype(BF16), preferred_element_type=F32) + b_ref[...]


def ada_project(c, w_ada, b_ada, tn=1024):
    r, d = c.shape
    n = w_ada.shape[1]
    return pl.pallas_call(
        _ada_kernel,
        grid=(n // tn,),
        in_specs=[pl.BlockSpec((r, d), lambda j: (0, 0)),
                  pl.BlockSpec((d, tn), lambda j: (0, j)),
                  pl.BlockSpec((1, tn), lambda j: (0, j))],
        out_specs=pl.BlockSpec((r, tn), lambda j: (0, j)),
        out_shape=jax.ShapeDtypeStruct((r, n), F32),
        compiler_params=_params("arbitrary"),
        name="ada_project",
    )(c, w_ada, b_ada.reshape(1, n))


def _proj_act_kernel(h_ref, w_ref, o_ref, *, act):
    z = jnp.dot(h_ref[...], w_ref[...], preferred_element_type=F32)
    if act == "gelu":
        a = _gelu(z)
    else:
        a = jax.nn.sigmoid(z)
    o_ref[...] = a.astype(o_ref.dtype)


def proj_act(h, w, act, ts, tn):
    G, S, D = h.shape
    n = w.shape[1]
    return pl.pallas_call(
        functools.partial(_proj_act_kernel, act=act),
        grid=(G, S // ts, n // tn),
        in_specs=[pl.BlockSpec((None, ts, D), lambda g, s, j: (g, s, 0)),
                  pl.BlockSpec((D, tn), lambda g, s, j: (0, j))],
        out_specs=pl.BlockSpec((None, ts, tn), lambda g, s, j: (g, s, j)),
        out_shape=jax.ShapeDtypeStruct((G, S, n), BF16),
        compiler_params=_params("arbitrary", "arbitrary", "arbitrary"),
        name="proj_" + act,
    )(h, w)


def _proj_gelu_norm_kernel(h_ref, w_ref, g_ref, o_ref):
    z = jnp.dot(h_ref[...], w_ref[...], preferred_element_type=F32)
    o_ref[...] = _rms(_gelu(z), g_ref[...]).astype(o_ref.dtype)


def proj_gelu_norm(h, w, g, ts, out_dtype):
    G, S, D = h.shape
    n = w.shape[1]
    return pl.pallas_call(
        _proj_gelu_norm_kernel,
        grid=(G, S // ts),
        in_specs=[pl.BlockSpec((None, ts, D), lambda g_, s: (g_, s, 0)),
                  pl.BlockSpec((D, n), lambda g_, s: (0, 0)),
                  pl.BlockSpec((1, n), lambda g_, s: (0, 0))],
        out_specs=pl.BlockSpec((None, ts, n), lambda g_, s: (g_, s, 0)),
        out_shape=jax.ShapeDtypeStruct((G, S, n), out_dtype),
        compiler_params=_params("arbitrary", "arbitrary"),
        name="proj_gelu_norm",
    )(h, w, g.reshape(1, n))


def _q_proj_kernel(x_ref, g1_ref, sc_ref, sh_ref, wq_ref, gqa_ref, wuq_ref, gn_ref, gr_ref, tc_ref, ts_ref,
                   h_ref, q_ref, *, q_scale):
    h = (_rms(x_ref[...], g1_ref[...]) * (1.0 + sc_ref[...]) + sh_ref[...]).astype(h_ref.dtype)
    h_ref[...] = h
    zq = jnp.dot(h, wq_ref[...], preferred_element_type=F32)
    zn = _rms(zq, gqa_ref[...]).astype(BF16)
    tc, ts = tc_ref[...], ts_ref[...]
    gains = jnp.concatenate([gn_ref[...], gr_ref[...]], axis=1) * q_scale
    same_tile = (lax.broadcasted_iota(jnp.int32, (HEAD_PAD, HEAD_PAD), 0) // LANES
                 == lax.broadcasted_iota(jnp.int32, (HEAD_PAD, HEAD_PAD), 1) // LANES)
    tile_mean = jnp.where(same_tile, 1.0 / LANES, 0.0).astype(BF16)
    shift_lane = jnp.where(lax.broadcasted_iota(jnp.int32, (1, LANES), 1) == LANES // 2, 1.0, 0.0)
    for pair in range(q_ref.shape[0] // 2):
        blk2 = jnp.dot(zn, wuq_ref[:, pair * 2 * HEAD_PAD:(pair + 1) * 2 * HEAD_PAD],
                       preferred_element_type=F32)
        for sub in range(2):
            hh = 2 * pair + sub
            blk = blk2[:, sub * HEAD_PAD:(sub + 1) * HEAD_PAD]
            ms = jnp.dot((blk * blk).astype(BF16), tile_mean, preferred_element_type=F32)
            y = blk * lax.rsqrt(ms + EPS) * gains
            q_ref[hh, :, :LANES] = y[:, :LANES].astype(q_ref.dtype)
            q_ref[hh, :, LANES:] = (_rope_group(y[:, LANES:], tc, ts) + shift_lane).astype(q_ref.dtype)


def prenorm_q_project(x, g_norm, sc, sh, w_q, g_q_a, w_uq_pad, g_nope, g_rope_pad, tabs, ts, q_scale):
    G, S, D = x.shape
    ql = w_q.shape[1]
    H = w_uq_pad.shape[1] // HEAD_PAD
    const = lambda g, s: (0, 0)
    tab_spec = pl.BlockSpec((ts, LANES), lambda g, s: (s, 0))
    row = pl.BlockSpec((None, ts, D), lambda g, s: (g, s, 0))
    return pl.pallas_call(
        functools.partial(_q_proj_kernel, q_scale=q_scale),
        grid=(G, S // ts),
        in_specs=[row, pl.BlockSpec((1, D), const), _mod_spec(sc, ts), _mod_spec(sh, ts),
                  pl.BlockSpec((D, ql), const),
                  pl.BlockSpec((1, ql), const),
                  pl.BlockSpec((ql, H * HEAD_PAD), const),
                  pl.BlockSpec((1, LANES), const),
                  pl.BlockSpec((1, LANES), const),
                  tab_spec, tab_spec],
        out_specs=[row, pl.BlockSpec((None, H, ts, HEAD_PAD), lambda g, s: (g, 0, s, 0))],
        out_shape=[jax.ShapeDtypeStruct((G, S, D), BF16),
                   jax.ShapeDtypeStruct((G, H, S, HEAD_PAD), BF16)],
        compiler_params=_params("arbitrary", "arbitrary"),
        name="prenorm_q_project",
    )(x, g_norm.reshape(1, D), sc, sh, w_q, g_q_a.reshape(1, ql), w_uq_pad, g_nope.reshape(1, LANES),
      g_rope_pad, *tabs)


def _kv_proj_kernel(h_ref, wkv_ref, gkva_ref, gkr_ref, tc_ref, ts_ref, *rest, n_rope, expand):
    if expand:
        wuk_ref, wuv_ref, gkn_ref, kshift_ref, c_ref, kr_ref, k_ref, v_ref = rest
    else:
        c_ref, kr_ref, krp_ref = rest
    lat = c_ref.shape[-1]
    z = jnp.dot(h_ref[...], wkv_ref[...], preferred_element_type=F32)
    c = _rms(z[:, :lat], gkva_ref[...])
    c_ref[...] = c
    rope = _rope_group(_rms_rope_group(z[:, lat:], gkr_ref[...]), tc_ref[...], ts_ref[...])
    kr_ref[...] = rope[:, :n_rope]
    rope_b = rope.astype(BF16)
    if not expand:
        krp_ref[...] = rope_b
        return
    cb = c.astype(BF16)
    gkn = gkn_ref[...]
    rope_b = (rope + kshift_ref[...]).astype(BF16)
    for pair in range(k_ref.shape[0] // 2):
        cols = slice(pair * 2 * LANES, (pair + 1) * 2 * LANES)
        kn2 = jnp.dot(cb, wuk_ref[:, cols], preferred_element_type=F32)
        v2 = jnp.dot(cb, wuv_ref[:, cols], preferred_element_type=F32)
        for sub in range(2):
            hh = 2 * pair + sub
            lanes = slice(sub * LANES, (sub + 1) * LANES)
            k_ref[hh, :, :LANES] = _rms(kn2[:, lanes], gkn).astype(k_ref.dtype)
            k_ref[hh, :, LANES:] = rope_b
            v_ref[hh] = v2[:, lanes].astype(v_ref.dtype)


def kv_project(h, w_kv, g_kv_a, g_k_rope_pad, tabs, ts, n_rope, expand_weights=None):
    G, S, D = h.shape
    lat = g_kv_a.shape[-1]
    const = lambda g, s: (0, 0)
    tab_spec = pl.BlockSpec((ts, LANES), lambda g, s: (s, 0))
    in_specs = [pl.BlockSpec((None, ts, D), lambda g, s: (g, s, 0)),
                pl.BlockSpec((D, lat + LANES), const),
                pl.BlockSpec((1, lat), const),
                pl.BlockSpec((1, LANES), const),
                tab_spec, tab_spec]
    args = [h, w_kv, g_kv_a.reshape(1, lat), g_k_rope_pad, *tabs]
    out_specs = [pl.BlockSpec((None, ts, lat), lambda g, s: (g, s, 0)),
                 pl.BlockSpec((None, ts, n_rope), lambda g, s: (g, s, 0))]
    out_shape = [jax.ShapeDtypeStruct((G, S, lat), F32),
                 jax.ShapeDtypeStruct((G, S, n_rope), F32)]
    expand = expand_weights is not None
    if expand:
        w_uk, w_uv, g_k_nope, k_shift = expand_weights
        H = w_uk.shape[1] // LANES
        in_specs += [pl.BlockSpec(w_uk.shape, const), pl.BlockSpec(w_uv.shape, const),
                     pl.BlockSpec((1, LANES), const), pl.BlockSpec((1, LANES), const)]
        args += [w_uk, w_uv, g_k_nope.reshape(1, LANES), k_shift]
        out_specs += [pl.BlockSpec((None, H, ts, HEAD_PAD), lambda g, s: (g, 0, s, 0)),
                      pl.BlockSpec((None, H, ts, LANES), lambda g, s: (g, 0, s, 0))]
        out_shape += [jax.ShapeDtypeStruct((G, H, S, HEAD_PAD), BF16),
                      jax.ShapeDtypeStruct((G, H, S, LANES), BF16)]
    else:
        out_specs.append(pl.BlockSpec((None, ts, LANES), lambda g, s: (g, s, 0)))
        out_shape.append(jax.ShapeDtypeStruct((G, S, LANES), BF16))
    return pl.pallas_call(
        functools.partial(_kv_proj_kernel, n_rope=n_rope, expand=expand),
        grid=(G, S // ts),
        in_specs=in_specs,
        out_specs=out_specs,
        out_shape=out_shape,
        compiler_params=_params("arbitrary", "arbitrary"),
        name="kv_project",
    )(*args)


def _spatial_gate_kernel(u_ref, v_ref, w_ref, b_ref, o_ref, wm_sc, *, n_chunks):
    row = lax.broadcasted_iota(jnp.int32, (SG_CHUNK, SG_CHUNK), 0)
    col = lax.broadcasted_iota(jnp.int32, (SG_CHUNK, SG_CHUNK), 1)
    tril = col <= row
    n_groups = w_ref.shape[0]
    for g in range(n_groups):
        wm_sc[g] = jnp.where(tril, w_ref[g], 0.0).astype(BF16)

    def chunk_body(n, carry):
        rows = pl.ds(pl.multiple_of(n * SG_CHUNK, SG_CHUNK), SG_CHUNK)
        for g in range(n_groups):
            lanes = slice(g * LANES, (g + 1) * LANES)
            mix = jnp.dot(wm_sc[g], v_ref[rows, lanes], preferred_element_type=F32) + b_ref[:, lanes]
            o_ref[rows, lanes] = (u_ref[rows, lanes].astype(F32) * mix).astype(o_ref.dtype)
        return carry

    lax.fori_loop(0, n_chunks, chunk_body, 0)


def spatial_gate(u, v, w_s, bias_rows, ts):
    G, S, W = u.shape
    blk = pl.BlockSpec((None, ts, W), lambda g, s: (g, s, 0))
    return pl.pallas_call(
        functools.partial(_spatial_gate_kernel, n_chunks=ts // SG_CHUNK),
        grid=(G, S // ts),
        in_specs=[blk, blk,
                  pl.BlockSpec(w_s.shape, lambda g, s: (0, 0, 0)),
                  pl.BlockSpec(bias_rows.shape, lambda g, s: (0, 0))],
        out_specs=blk,
        out_shape=jax.ShapeDtypeStruct((G, S, W), BF16),
        scratch_shapes=[pltpu.VMEM(w_s.shape, BF16)],
        compiler_params=_params("arbitrary", "arbitrary"),
        name="spatial_gate",
    )(u, v, w_s, bias_rows)


def _spatial_gate_open_kernel(u_ref, v_ref, wl_ref, b_ref, o_ref):
    T = u_ref.shape[1]
    for i in range(T):
        acc = b_ref[i:i + 1, :] + wl_ref[i, 0:1, :] * v_ref[:, 0, :]
        for j in range(1, i + 1):
            acc = acc + wl_ref[i, j:j + 1, :] * v_ref[:, j, :]
        o_ref[:, i, :] = (u_ref[:, i, :].astype(F32) * acc).astype(o_ref.dtype)


def spatial_gate_open(u, v, w_lanes, bias_rows):
    B, T, W = u.shape
    full = lambda a: pl.BlockSpec(a.shape, lambda i: (0,) * a.ndim)
    return pl.pallas_call(
        _spatial_gate_open_kernel,
        grid=(1,),
        in_specs=[full(u), full(v), full(w_lanes), full(bias_rows)],
        out_specs=pl.BlockSpec((B, T, W), lambda i: (0, 0, 0)),
        out_shape=jax.ShapeDtypeStruct((B, T, W), BF16),
        compiler_params=_params("arbitrary"),
        name="spatial_gate_open",
    )(u, v, w_lanes, bias_rows)


def _attn_kernel(bounded_ref, q_ref, k_ref, v_ref, o_ref, m_sc, acc_sc, *, tq, tk, td):
    i = pl.program_id(2)
    dv = v_ref.shape[-1]
    unroll = tq // tk
    nt = (((1,), (1,)), ((), ()))
    ones = jnp.ones((tk, LANES), BF16)
    d0 = pl.multiple_of(i * tq, tq)
    chunk_mask = (lax.broadcasted_iota(jnp.int32, (td, td), 1) // CHUNK
                  <= lax.broadcasted_iota(jnp.int32, (td, td), 0) // CHUNK)

    def scores(rows0, k0, keys):
        return lax.dot_general(q_ref[rows0:, :], k_ref[pl.ds(k0, keys), :], nt, preferred_element_type=F32)

    def diag_scores(c):
        r0 = c * td
        s = scores(r0, d0 + r0, td)
        top = jnp.where(chunk_mask, s[:td], -jnp.inf)
        return top if r0 + td == tq else jnp.concatenate([top, s[td:]], axis=0)

    def values(k0, keys):
        return jnp.concatenate([v_ref[pl.ds(k0, keys), :], ones[:keys]], axis=1)

    def finish():
        acc = acc_sc[...]
        o_ref[...] = (acc[:, :dv] / acc[:, dv:]).astype(o_ref.dtype)

    @pl.when(bounded_ref[0] == 1)
    def _():
        for c in range(tq // td):
            r0 = c * td
            pv = jnp.dot(jnp.exp2(diag_scores(c)).astype(BF16), values(d0 + r0, td),
                         preferred_element_type=F32)
            if c == 0:
                acc_sc[...] = pv
            else:
                acc_sc[r0:, :] += pv

        def body(j, carry):
            k0 = pl.multiple_of(j * tq, tq)
            acc = acc_sc[...]
            for u in range(unroll):
                p = jnp.exp2(scores(0, k0 + u * tk, tk)).astype(BF16)
                acc = acc + jnp.dot(p, values(k0 + u * tk, tk), preferred_element_type=F32)
            acc_sc[...] = acc
            return carry

        lax.fori_loop(0, i, body, 0)
        finish()

    @pl.when(bounded_ref[0] != 1)
    def _():
        def step(s, k0, m_old, acc_old):
            rows, keys = s.shape
            row_max = jnp.max(s, axis=-1, keepdims=True)
            m_new = (jnp.broadcast_to(row_max, (rows, LANES)) if m_old is None
                     else jnp.maximum(m_old, row_max))
            p = jnp.concatenate([jnp.exp2(s[:, c * LANES:(c + 1) * LANES] - m_new)
                                 for c in range(keys // LANES)], axis=1).astype(BF16)
            pv = jnp.dot(p, values(k0, keys), preferred_element_type=F32)
            if m_old is None:
                return m_new, pv
            alpha = jnp.exp2(m_old - m_new)
            return m_new, jnp.concatenate([alpha] * ((dv + LANES) // LANES), axis=1) * acc_old + pv

        for c in range(tq // td):
            r0 = c * td
            if c == 0:
                m, acc = step(diag_scores(c), d0, None, None)
            else:
                m, acc = step(diag_scores(c), d0 + r0, m_sc[r0:, :], acc_sc[r0:, :])
            m_sc[r0:, :] = m
            acc_sc[r0:, :] = acc

        def body(j, carry):
            k0 = pl.multiple_of(j * tq, tq)
            ss = [scores(0, k0 + u * tk, tk) for u in range(unroll)]
            m, acc = m_sc[...], acc_sc[...]
            for u in range(unroll):
                m, acc = step(ss[u], k0 + u * tk, m, acc)
            m_sc[...] = m
            acc_sc[...] = acc
            return carry

        lax.fori_loop(0, i, body, 0)
        finish()


def attention_prompt(bounded, q, k, v, tq, tk, td):
    G, H, S, _ = q.shape
    dv = v.shape[-1]
    assert dv == LANES and tq % tk == 0 and tq % td == 0 and td % LANES == 0 and td <= tk
    return pl.pallas_call(
        functools.partial(_attn_kernel, tq=tq, tk=tk, td=td),
        grid=(G, H, S // tq),
        in_specs=[pl.BlockSpec(memory_space=pltpu.SMEM),
                  pl.BlockSpec((None, None, tq, HEAD_PAD), lambda g, h, i: (g, h, i, 0)),
                  pl.BlockSpec((None, None, S, HEAD_PAD), lambda g, h, i: (g, h, 0, 0)),
                  pl.BlockSpec((None, None, S, dv), lambda g, h, i: (g, h, 0, 0))],
        out_specs=pl.BlockSpec((None, tq, dv), lambda g, h, i: (g, i, h)),
        out_shape=jax.ShapeDtypeStruct((G, S, H * dv), BF16),
        scratch_shapes=[pltpu.VMEM((tq, LANES), F32), pltpu.VMEM((tq, dv + LANES), F32)],
        compiler_params=_params("arbitrary", "arbitrary", "arbitrary"),
        name="attention_prompt",
    )(bounded, q, k, v)


def _attn_cached_kernel(q_ref, cache_ref, ckr_ref, cnew_ref, krnew_ref, wuk_ref, wuv_ref, gkn_ref,
                        o_ref, call_sc, krall_sc, p_sc, *, past, n_new):
    H = q_ref.shape[0]
    L = past + n_new
    Lp = call_sc.shape[0]
    lat = call_sc.shape[1]
    call_sc[0:past, :] = cache_ref[...].astype(BF16)
    call_sc[past:L, :] = cnew_ref[...].astype(BF16)
    call_sc[L:Lp, :] = jnp.zeros((Lp - L, lat), BF16)
    n_rope = ckr_ref.shape[-1]
    krall_sc[0:past, n_rope:] = jnp.zeros((past, LANES - n_rope), BF16)
    krall_sc[0:past, :n_rope] = ckr_ref[...].astype(BF16)
    krall_sc[past:L, :] = krnew_ref[...]
    krall_sc[L:Lp, :] = jnp.zeros((Lp - L, LANES), BF16)
    call = call_sc[...]
    krall = krall_sc[...]
    gkn = gkn_ref[...]
    valid = lax.broadcasted_iota(jnp.int32, (n_new, Lp), 1) < L
    nt = (((1,), (1,)), ((), ()))
    group = 4
    for grp in range(H // group):
        cols = slice(grp * group * LANES, (grp + 1) * group * LANES)
        kn_g = jnp.dot(call, wuk_ref[:, cols], preferred_element_type=F32)
        for sub in range(group):
            hh = group * grp + sub
            kn = _rms(kn_g[:, sub * LANES:(sub + 1) * LANES], gkn).astype(BF16)
            kh = jnp.concatenate([kn, krall], axis=1)
            s = lax.dot_general(q_ref[hh], kh, nt, preferred_element_type=F32)
            s = jnp.where(valid, s, -jnp.inf)
            m = jnp.max(s, axis=-1, keepdims=True)
            p = jnp.exp2(s - m)
            p = p / jnp.sum(p, axis=-1, keepdims=True)
            p_sc[hh * n_new:(hh + 1) * n_new, :] = p.astype(BF16)
    o_lat = jnp.dot(p_sc[...], call, preferred_element_type=F32).astype(BF16)
    for hh in range(H):
        o_ref[:, hh * LANES:(hh + 1) * LANES] = jnp.dot(
            o_lat[hh * n_new:(hh + 1) * n_new, :], wuv_ref[:, hh * LANES:(hh + 1) * LANES],
            preferred_element_type=F32).astype(o_ref.dtype)


def attention_cached(q, cache_lat, cache_kr, layer, c_new, kr_new_pad, w_uk, w_uv, g_k_nope, n_new):
    _, H, BT, _ = q.shape
    _, B, past, lat = cache_lat.shape
    n_rope = cache_kr.shape[-1]
    L = past + n_new
    Lp = -(-L // LANES) * LANES
    const = lambda b: (0, 0)
    return pl.pallas_call(
        functools.partial(_attn_cached_kernel, past=past, n_new=n_new),
        grid=(B,),
        in_specs=[pl.BlockSpec((None, H, n_new, HEAD_PAD), lambda b: (0, 0, b, 0)),
                  pl.BlockSpec((None, None, past, lat), lambda b: (layer, b, 0, 0)),
                  pl.BlockSpec((None, None, past, n_rope), lambda b: (layer, b, 0, 0)),
                  pl.BlockSpec((None, n_new, lat), lambda b: (0, b, 0)),
                  pl.BlockSpec((None, n_new, LANES), lambda b: (0, b, 0)),
                  pl.BlockSpec(w_uk.shape, const), pl.BlockSpec(w_uv.shape, const),
                  pl.BlockSpec((1, LANES), const)],
        out_specs=pl.BlockSpec((None, n_new, H * LANES), lambda b: (0, b, 0)),
        out_shape=jax.ShapeDtypeStruct((1, BT, H * LANES), BF16),
        scratch_shapes=[pltpu.VMEM((Lp, lat), BF16), pltpu.VMEM((Lp, LANES), BF16),
                        pltpu.VMEM((H * n_new, Lp), BF16)],
        compiler_params=_params("arbitrary"),
        name="attention_cached",
    )(q, cache_lat, cache_kr, c_new, kr_new_pad, w_uk, w_uv, g_k_nope.reshape(1, LANES))


def _merge_kernel(a_ref, b_ref, ga_ref, gb_ref, wpa_ref, wpb_ref, o_ref):
    pa = jnp.dot(a_ref[...], wpa_ref[...], preferred_element_type=F32)
    pb = jnp.dot(b_ref[...], wpb_ref[...], preferred_element_type=F32)
    o_ref[...] = (ga_ref[...].astype(F32) * pa + gb_ref[...].astype(F32) * pb).astype(o_ref.dtype)


def merge_branches(o_sg, o_mla, gates, w_pa, w_pb, ts, tn):
    G, S, W = o_sg.shape
    D = w_pa.shape[1]
    nj = D // tn
    row = lambda g, s, j: (g, s, 0)
    return pl.pallas_call(
        _merge_kernel,
        grid=(G, S // ts, nj),
        in_specs=[pl.BlockSpec((None, ts, W), row),
                  pl.BlockSpec((None, ts, o_mla.shape[-1]), row),
                  pl.BlockSpec((None, ts, tn), lambda g, s, j: (g, s, j)),
                  pl.BlockSpec((None, ts, tn), lambda g, s, j: (g, s, j + nj)),
                  pl.BlockSpec((W, tn), lambda g, s, j: (0, j)),
                  pl.BlockSpec((o_mla.shape[-1], tn), lambda g, s, j: (0, j))],
        out_specs=pl.BlockSpec((None, ts, tn), lambda g, s, j: (g, s, j)),
        out_shape=jax.ShapeDtypeStruct((G, S, D), BF16),
        compiler_params=_params("arbitrary", "arbitrary", "arbitrary"),
        name="merge_branches",
    )(o_sg, o_mla, gates, gates, w_pa, w_pb)


def _out_proj_kernel(m_ref, wo_ref, x_ref, g1_ref, gn_ref, sc_ref, sh_ref, x1_ref, h2_ref):
    y = jnp.dot(m_ref[...], wo_ref[...], preferred_element_type=F32)
    x1 = x_ref[...] + g1_ref[...] * y
    x1_ref[...] = x1
    h2_ref[...] = (_rms(x1, gn_ref[...]) * (1.0 + sc_ref[...]) + sh_ref[...]).astype(h2_ref.dtype)


def out_project(m, w_o, x, g1, g_norm2, sc2, sh2, ts):
    G, S, D = x.shape
    row = pl.BlockSpec((None, ts, D), lambda g, s: (g, s, 0))
    return pl.pallas_call(
        _out_proj_kernel,
        grid=(G, S // ts),
        in_specs=[row, pl.BlockSpec(w_o.shape, lambda g, s: (0, 0)), row,
                  _mod_spec(g1, ts), pl.BlockSpec((1, D), lambda g, s: (0, 0)),
                  _mod_spec(sc2, ts), _mod_spec(sh2, ts)],
        out_specs=[row, row],
        out_shape=[jax.ShapeDtypeStruct((G, S, D), F32), jax.ShapeDtypeStruct((G, S, D), BF16)],
        compiler_params=_params("arbitrary", "arbitrary"),
        name="out_project",
    )(m, w_o, x, g1, g_norm2.reshape(1, D), sc2, sh2)


def _mlp_kernel(h_ref, wup_ref, wdn_ref, x1_ref, g2_ref, o_ref):
    j = pl.program_id(2)
    hid = jnp.dot(h_ref[...], wup_ref[...], preferred_element_type=F32)
    hid = jnp.square(jnp.maximum(hid, 0.0)).astype(BF16)
    part = jnp.dot(hid, wdn_ref[...], preferred_element_type=F32)

    @pl.when(j == 0)
    def _():
        o_ref[...] = part

    @pl.when(j > 0)
    def _():
        o_ref[...] += part

    @pl.when(j == pl.num_programs(2) - 1)
    def _():
        o_ref[...] = x1_ref[...] + g2_ref[...] * o_ref[...]


def mlp_residual(h2, w_up, w_down, x1, g2, ts, th):
    G, S, D = x1.shape
    hidden = w_up.shape[1]
    row = pl.BlockSpec((None, ts, D), lambda g, s, j: (g, s, 0))
    return pl.pallas_call(
        _mlp_kernel,
        grid=(G, S // ts, hidden // th),
        in_specs=[row,
                  pl.BlockSpec((D, th), lambda g, s, j: (0, j)),
                  pl.BlockSpec((th, D), lambda g, s, j: (j, 0)),
                  row, _mod_spec(g2, ts)],
        out_specs=row,
        out_shape=jax.ShapeDtypeStruct((G, S, D), F32),
        compiler_params=_params("arbitrary", "arbitrary", "arbitrary"),
        name="mlp_residual",
    )(h2, w_up, w_down, x1, g2)


def _rope_tables(pos, n_rope):
    inv = jnp.float32(ROPE_BASE) ** (-jnp.arange(0, n_rope, 2, dtype=F32) / n_rope)
    ang = pos.astype(F32)[:, None] * inv[None, :]
    cos, sin = lax.optimization_barrier((jnp.cos(ang), jnp.sin(ang)))
    pad = jnp.zeros((pos.shape[0], LANES - n_rope), F32)
    return (jnp.concatenate([cos, cos, pad], axis=1), jnp.concatenate([-sin, sin, pad], axis=1))


def _dup_lanes(g):
    return jnp.concatenate([g.astype(F32), g.astype(F32)]).reshape(1, 2 * g.shape[0])


def _layer(x, mods, pos, P, ts, *, cache=None):
    sh1, sc1, g1, sh2, sc2, g2 = mods
    n_rope = P["n_rope"]
    tabs = _rope_tables(pos, n_rope)
    h, q = prenorm_q_project(x, P["g_norm1"], sc1, sh1, P["w_q"], P["g_q_a"], P["w_uq"], P["g_q_nope"],
                             P["g_q_rope"], tabs, ts, P["q_scale"])
    u = proj_act(h, P["w_u"], "gelu", ts, P["w_u"].shape[1])
    gates = proj_act(h, P["w_g"], "sigmoid", ts, P["w_g"].shape[1] // 2)
    if cache is None:
        v = proj_gelu_norm(h, P["w_v"], P["g_sg"], ts, BF16)
        o_sg = spatial_gate(u, v, P["w_s"], P["b_rows"], min(1024, x.shape[1]))
        c_kv, k_rope, k, vv = kv_project(h, P["w_kv"], P["g_kv_a"], P["g_k_rope"], tabs, ts, n_rope,
                                         expand_weights=(P["w_uk"], P["w_uv"], P["g_k_nope"], P["k_shift"]))
        o_mla = attention_prompt(P["bounded"], q, k, vv, tq=min(2048, x.shape[1]),
                                 tk=min(1024, x.shape[1]), td=min(512, x.shape[1]))
        extra = ()
    else:
        cache_lat, cache_kr, layer, B, T = cache
        v = proj_gelu_norm(h, P["w_v"], P["g_sg"], ts, F32)
        W = v.shape[-1]
        o_sg = spatial_gate_open(u.reshape(B, T, W), v.reshape(B, T, W), P["w_lanes"][:T, :T],
                                 P["b_rows"][:T]).reshape(1, B * T, W)
        c_kv, k_rope, kr_pad = kv_project(h, P["w_kv"], P["g_kv_a"], P["g_k_rope"], tabs, ts, n_rope)
        o_mla = attention_cached(q, cache_lat, cache_kr, layer, c_kv, kr_pad, P["w_uk"], P["w_uv"],
                                 P["g_k_nope"], T)
        extra = (v,)
    m = merge_branches(o_sg, o_mla, gates, P["w_pa"], P["w_pb"], ts, P["w_pa"].shape[1] // 2)
    x1, h2 = out_project(m, P["w_o"], x, g1, P["g_norm2"], sc2, sh2, ts)
    y = mlp_residual(h2, P["w_up"], P["w_down"], x1, g2, ts, min(2048, P["w_up"].shape[1]))
    return (y, c_kv, k_rope) + extra


def kernel(x_prompt, x_sample, cache_kv_latent, cache_k_rope, c_prompt, c_sample, w_ada, b_ada, g_norm1, g_norm2, w_in, g_sg, w_s, b_s, g_q_a, w_uq, g_q_nope, g_q_rope, g_kv_a, g_k_rope, w_uk, g_k_nope, w_uv, w_pa, w_pb, w_o, w_up, w_down):
    depth = w_in.shape[0]
    Bp, S, D = x_prompt.shape
    Bs, T, _ = x_sample.shape
    past = cache_kv_latent.shape[2]
    q_lora = g_q_a.shape[-1]
    lat = g_kv_a.shape[-1]
    n_nope = g_q_nope.shape[-1]
    n_rope = g_q_rope.shape[-1]
    H = w_uk.shape[2]
    sg_w = g_sg.shape[-1]
    off_q = 2 * sg_w
    off_kv = off_q + q_lora
    off_gate = off_kv + lat + n_rope
    assert n_nope == LANES and w_uv.shape[-1] == LANES and 2 * n_rope == LANES
    assert sg_w // SG_GROUPS == LANES and H == N_HEADS

    nb = Bp + Bs
    nb_pad = -(-nb // 8) * 8
    c_all = jnp.concatenate([c_prompt, c_sample, jnp.zeros((nb_pad - nb, D), F32)], axis=0)

    y_p, y_s = x_prompt, x_sample.reshape(1, Bs * T, D)
    outs = [[] for _ in range(5)]
    for l in range(depth):
        wi = w_in[l].astype(BF16)
        w_uq_l = w_uq[l].astype(BF16).reshape(q_lora, H, n_nope + n_rope)
        w_uq_pad = jnp.concatenate([w_uq_l, w_uq_l[:, :, n_nope:]], axis=2)
        q_scale = float((n_nope + n_rope) ** -0.5 * LOG2E)
        def sq_norm_bound(g_nope, g_rope):
            return n_nope * jnp.max(jnp.square(g_nope)) + n_rope * jnp.max(jnp.square(g_rope))
        score_bound = q_scale * jnp.sqrt(sq_norm_bound(g_q_nope[l], g_q_rope[l])
                                         * sq_norm_bound(g_k_nope[l], g_k_rope[l]))
        bounded = (score_bound <= BOUND_MAX).astype(jnp.int32).reshape(1)
        k_shift = jnp.where(jnp.arange(LANES) == LANES // 2, -score_bound, 0.0).astype(F32).reshape(1, LANES)
        P = {
            "n_rope": n_rope,
            "q_scale": q_scale, "k_shift": k_shift, "bounded": bounded,
            "g_norm1": g_norm1[l], "g_norm2": g_norm2[l], "g_sg": g_sg[l],
            "w_u": wi[:, :sg_w], "w_v": wi[:, sg_w:off_q], "w_q": wi[:, off_q:off_kv],
            "w_kv": jnp.concatenate([wi[:, off_kv:off_gate], wi[:, off_kv + lat:off_gate]], axis=1),
            "w_g": wi[:, off_gate:],
            "g_q_a": g_q_a[l], "w_uq": w_uq_pad.reshape(q_lora, H * HEAD_PAD),
            "g_q_nope": g_q_nope[l], "g_q_rope": _dup_lanes(g_q_rope[l]),
            "g_kv_a": g_kv_a[l], "g_k_rope": _dup_lanes(g_k_rope[l]),
            "w_uk": w_uk[l].astype(BF16).reshape(lat, H * n_nope),
            "w_uv": w_uv[l].astype(BF16).reshape(lat, H * LANES),
            "g_k_nope": g_k_nope[l],
            "w_s": w_s[l],
            "b_rows": jnp.repeat(b_s[l].T, LANES, axis=1),
            "w_lanes": jnp.repeat(w_s[l][:, :T, :T].transpose(1, 2, 0), LANES, axis=2),
            "w_pa": w_pa[l].astype(BF16), "w_pb": w_pb[l].astype(BF16), "w_o": w_o[l].astype(BF16),
            "w_up": w_up[l].astype(BF16), "w_down": w_down[l].astype(BF16),
        }
        mod = ada_project(c_all, w_ada[l], b_ada[l])
        mods_p = [a.reshape(Bp, 1, D) for a in jnp.split(mod[:Bp], 6, axis=-1)]
        mods_s = [jnp.repeat(a, T, axis=0).reshape(1, Bs * T, D)
                  for a in jnp.split(mod[Bp:nb], 6, axis=-1)]

        ts_p = min(512, S)
        y_p, lp, kp = _layer(y_p, mods_p, jnp.arange(S), P, ts_p)
        pos_s = jnp.tile(past + jnp.arange(T), Bs)
        y_s, ls, ks, vs = _layer(y_s, mods_s, pos_s, P, Bs * T,
                                 cache=(cache_kv_latent, cache_k_rope, l, Bs, T))
        for lst, a in zip(outs, (lp, kp, ls.reshape(Bs, T, lat), ks.reshape(Bs, T, n_rope),
                                 vs.reshape(Bs, T, sg_w))):
            lst.append(a)
    return (y_p, y_s.reshape(Bs, T, D)) + tuple(jnp.stack(o) for o in outs)
```

```python
import functools
import math

import jax
import jax.numpy as jnp
import numpy as np
from jax import lax
from jax.experimental import pallas as pl
from jax.experimental.pallas import tpu as pltpu

F32 = jnp.float32
BF16 = jnp.bfloat16

EPS = 1e-6
ROPE_BASE = 10000.0
N_HEADS = 16
CHUNK = 64
SG_CHUNK = 128
SG_GROUPS = 16
LANES = 128
HEAD_PAD = 256
LOG2E = math.log2(math.e)
BOUND_MAX = 50.0

VMEM_LIMIT = 58 * 1024 * 1024


def _params(*sem):
    return pltpu.CompilerParams(dimension_semantics=sem, vmem_limit_bytes=VMEM_LIMIT)


def _rms(x, g):
    ms = jnp.mean(x * x, axis=-1, keepdims=True)
    return x * lax.rsqrt(ms + EPS) * g


def _rms_rope_group(t, g_dup):
    ms = jnp.mean(t * t, axis=-1, keepdims=True)
    return t * lax.rsqrt(ms + EPS) * g_dup


def _rope_group(t, tc, ts):
    return t * tc + pltpu.roll(t, LANES - LANES // 4, 1) * ts


def _gelu(z):
    return 0.5 * z * (1.0 + lax.erf(z * np.float32(math.sqrt(0.5))))


def _mod_spec(mod, ts):
    d = mod.shape[-1]
    if mod.shape[1] == 1:
        return pl.BlockSpec((None, 1, d), lambda g, s, *_: (g, 0, 0))
    return pl.BlockSpec((None, ts, d), lambda g, s, *_: (g, s, 0))


def _ada_kernel(c_ref, w_ref, b_ref, o_ref):
    c = c_ref[...]
    s = (c * jax.nn.sigmoid(c)).astype(BF16)
    o_ref[...] = jnp.dot(s, w_ref[...].astype(BF16), preferred_element_type=F32) + b_ref[...]


def ada_project(c, w_ada, b_ada, tn=1024):
    r, d = c.shape
    n = w_ada.shape[1]
    return pl.pallas_call(
        _ada_kernel,
        grid=(n // tn,),
        in_specs=[pl.BlockSpec((r, d), lambda j: (0, 0)),
                  pl.BlockSpec((d, tn), lambda j: (0, j)),
                  pl.BlockSpec((1, tn), lambda j: (0, j))],
        out_specs=pl.BlockSpec((r, tn), lambda j: (0, j)),
        out_shape=jax.ShapeDtypeStruct((r, n), F32),
        compiler_params=_params("arbitrary"),
        name="ada_project",
    )(c, w_ada, b_ada.reshape(1, n))


def _proj_act_kernel(h_ref, w_ref, o_ref, *, act):
    z = jnp.dot(h_ref[...], w_ref[...], preferred_element_type=F32)
    if act == "gelu":
        a = _gelu(z)
    else:
        a = jax.nn.sigmoid(z)
    o_ref[...] = a.astype(o_ref.dtype)


def proj_act(h, w, act, ts, tn):
    G, S, D = h.shape
    n = w.shape[1]
    return pl.pallas_call(
        functools.partial(_proj_act_kernel, act=act),
        grid=(G, S // ts, n // tn),
        in_specs=[pl.BlockSpec((None, ts, D), lambda g, s, j: (g, s, 0)),
                  pl.BlockSpec((D, tn), lambda g, s, j: (0, j))],
        out_specs=pl.BlockSpec((None, ts, tn), lambda g, s, j: (g, s, j)),
        out_shape=jax.ShapeDtypeStruct((G, S, n), BF16),
        compiler_params=_params("arbitrary", "arbitrary", "arbitrary"),
        name="proj_" + act,
    )(h, w)


def _proj_gelu_norm_kernel(h_ref, w_ref, g_ref, o_ref):
    z = jnp.dot(h_ref[...], w_ref[...], preferred_element_type=F32)
    o_ref[...] = _rms(_gelu(z), g_ref[...]).astype(o_ref.dtype)


def proj_gelu_norm(h, w, g, ts, out_dtype):
    G, S, D = h.shape
    n = w.shape[1]
    return pl.pallas_call(
        _proj_gelu_norm_kernel,
        grid=(G, S // ts),
        in_specs=[pl.BlockSpec((None, ts, D), lambda g_, s: (g_, s, 0)),
                  pl.BlockSpec((D, n), lambda g_, s: (0, 0)),
                  pl.BlockSpec((1, n), lambda g_, s: (0, 0))],
        out_specs=pl.BlockSpec((None, ts, n), lambda g_, s: (g_, s, 0)),
        out_shape=jax.ShapeDtypeStruct((G, S, n), out_dtype),
        compiler_params=_params("arbitrary", "arbitrary"),
        name="proj_gelu_norm",
    )(h, w, g.reshape(1, n))


def _q_proj_kernel(x_ref, g1_ref, sc_ref, sh_ref, wq_ref, gqa_ref, wuq_ref, gn_ref, gr_ref, tc_ref, ts_ref,
                   h_ref, q_ref, *, q_scale):
    h = (_rms(x_ref[...], g1_ref[...]) * (1.0 + sc_ref[...]) + sh_ref[...]).astype(h_ref.dtype)
    h_ref[...] = h
    zq = jnp.dot(h, wq_ref[...], preferred_element_type=F32)
    zn = _rms(zq, gqa_ref[...]).astype(BF16)
    tc, ts = tc_ref[...], ts_ref[...]
    gains = jnp.concatenate([gn_ref[...], gr_ref[...]], axis=1) * q_scale
    same_tile = (lax.broadcasted_iota(jnp.int32, (HEAD_PAD, HEAD_PAD), 0) // LANES
                 == lax.broadcasted_iota(jnp.int32, (HEAD_PAD, HEAD_PAD), 1) // LANES)
    tile_mean = jnp.where(same_tile, 1.0 / LANES, 0.0).astype(BF16)
    shift_lane = jnp.where(lax.broadcasted_iota(jnp.int32, (1, LANES), 1) == LANES // 2, 1.0, 0.0)
    for pair in range(q_ref.shape[0] // 2):
        blk2 = jnp.dot(zn, wuq_ref[:, pair * 2 * HEAD_PAD:(pair + 1) * 2 * HEAD_PAD],
                       preferred_element_type=F32)
        for sub in range(2):
            hh = 2 * pair + sub
            blk = blk2[:, sub * HEAD_PAD:(sub + 1) * HEAD_PAD]
            ms = jnp.dot((blk * blk).astype(BF16), tile_mean, preferred_element_type=F32)
            y = blk * lax.rsqrt(ms + EPS) * gains
            q_ref[hh, :, :LANES] = y[:, :LANES].astype(q_ref.dtype)
            q_ref[hh, :, LANES:] = (_rope_group(y[:, LANES:], tc, ts) + shift_lane).astype(q_ref.dtype)


def prenorm_q_project(x, g_norm, sc, sh, w_q, g_q_a, w_uq_pad, g_nope, g_rope_pad, tabs, ts, q_scale):
    G, S, D = x.shape
    ql = w_q.shape[1]
    H = w_uq_pad.shape[1] // HEAD_PAD
    const = lambda g, s: (0, 0)
    tab_spec = pl.BlockSpec((ts, LANES), lambda g, s: (s, 0))
    row = pl.BlockSpec((None, ts, D), lambda g, s: (g, s, 0))
    return pl.pallas_call(
        functools.partial(_q_proj_kernel, q_scale=q_scale),
        grid=(G, S // ts),
        in_specs=[row, pl.BlockSpec((1, D), const), _mod_spec(sc, ts), _mod_spec(sh, ts),
                  pl.BlockSpec((D, ql), const),
                  pl.BlockSpec((1, ql), const),
                  pl.BlockSpec((ql, H * HEAD_PAD), const),
                  pl.BlockSpec((1, LANES), const),
                  pl.BlockSpec((1, LANES), const),
                  tab_spec, tab_spec],
        out_specs=[row, pl.BlockSpec((None, H, ts, HEAD_PAD), lambda g, s: (g, 0, s, 0))],
        out_shape=[jax.ShapeDtypeStruct((G, S, D), BF16),
                   jax.ShapeDtypeStruct((G, H, S, HEAD_PAD), BF16)],
        compiler_params=_params("arbitrary", "arbitrary"),
        name="prenorm_q_project",
    )(x, g_norm.reshape(1, D), sc, sh, w_q, g_q_a.reshape(1, ql), w_uq_pad, g_nope.reshape(1, LANES),
      g_rope_pad, *tabs)


def _kv_proj_kernel(h_ref, wkv_ref, gkva_ref, gkr_ref, tc_ref, ts_ref, *rest, n_rope, expand):
    if expand:
        wuk_ref, wuv_ref, gkn_ref, kshift_ref, c_ref, kr_ref, k_ref, v_ref = rest
    else:
        c_ref, kr_ref, krp_ref = rest
    lat = c_ref.shape[-1]
    z = jnp.dot(h_ref[...], wkv_ref[...], preferred_element_type=F32)
    c = _rms(z[:, :lat], gkva_ref[...])
    c_ref[...] = c
    rope = _rope_group(_rms_rope_group(z[:, lat:], gkr_ref[...]), tc_ref[...], ts_ref[...])
    kr_ref[...] = rope[:, :n_rope]
    rope_b = rope.astype(BF16)
    if not expand:
        krp_ref[...] = rope_b
        return
    cb = c.astype(BF16)
    gkn = gkn_ref[...]
    rope_b = (rope + kshift_ref[...]).astype(BF16)
    for pair in range(k_ref.shape[0] // 2):
        cols = slice(pair * 2 * LANES, (pair + 1) * 2 * LANES)
        kn2 = jnp.dot(cb, wuk_ref[:, cols], preferred_element_type=F32)
        v2 = jnp.dot(cb, wuv_ref[:, cols], preferred_element_type=F32)
        for sub in range(2):
            hh = 2 * pair + sub
            lanes = slice(sub * LANES, (sub + 1) * LANES)
            k_ref[hh, :, :LANES] = _rms(kn2[:, lanes], gkn).astype(k_ref.dtype)
            k_ref[hh, :, LANES:] = rope_b
            v_ref[hh] = v2[:, lanes].astype(v_ref.dtype)


def kv_project(h, w_kv, g_kv_a, g_k_rope_pad, tabs, ts, n_rope, expand_weights=None):
    G, S, D = h.shape
    lat = g_kv_a.shape[-1]
    const = lambda g, s: (0, 0)
    tab_spec = pl.BlockSpec((ts, LANES), lambda g, s: (s, 0))
    in_specs = [pl.BlockSpec((None, ts, D), lambda g, s: (g, s, 0)),
                pl.BlockSpec((D, lat + LANES), const),
                pl.BlockSpec((1, lat), const),
                pl.BlockSpec((1, LANES), const),
                tab_spec, tab_spec]
    args = [h, w_kv, g_kv_a.reshape(1, lat), g_k_rope_pad, *tabs]
    out_specs = [pl.BlockSpec((None, ts, lat), lambda g, s: (g, s, 0)),
                 pl.BlockSpec((None, ts, n_rope), lambda g, s: (g, s, 0))]
    out_shape = [jax.ShapeDtypeStruct((G, S, lat), F32),
                 jax.ShapeDtypeStruct((G, S, n_rope), F32)]
    expand = expand_weights is not None
    if expand:
        w_uk, w_uv, g_k_nope, k_shift = expand_weights
        H = w_uk.shape[1] // LANES
        in_specs += [pl.BlockSpec(w_uk.shape, const), pl.BlockSpec(w_uv.shape, const),
                     pl.BlockSpec((1, LANES), const), pl.BlockSpec((1, LANES), const)]
        args += [w_uk, w_uv, g_k_nope.reshape(1, LANES), k_shift]
        out_specs += [pl.BlockSpec((None, H, ts, HEAD_PAD), lambda g, s: (g, 0, s, 0)),
                      pl.BlockSpec((None, H, ts, LANES), lambda g, s: (g, 0, s, 0))]
        out_shape += [jax.ShapeDtypeStruct((G, H, S, HEAD_PAD), BF16),
                      jax.ShapeDtypeStruct((G, H, S, LANES), BF16)]
    else:
        out_specs.append(pl.BlockSpec((None, ts, LANES), lambda g, s: (g, s, 0)))
        out_shape.append(jax.ShapeDtypeStruct((G, S, LANES), BF16))
    return pl.pallas_call(
        functools.partial(_kv_proj_kernel, n_rope=n_rope, expand=expand),
        grid=(G, S // ts),
        in_specs=in_specs,
        out_specs=out_specs,
        out_shape=out_shape,
        compiler_params=_params("arbitrary", "arbitrary"),
        name="kv_project",
    )(*args)


def _gelu_gate_kernel(h_ref, wu_ref, v_ref, w_ref, b_ref, o_ref, wm_sc, u_sc, *, n_chunks):
    u_sc[...] = _gelu(jnp.dot(h_ref[...], wu_ref[...], preferred_element_type=F32)).astype(u_sc.dtype)
    row = lax.broadcasted_iota(jnp.int32, (SG_CHUNK, SG_CHUNK), 0)
    col = lax.broadcasted_iota(jnp.int32, (SG_CHUNK, SG_CHUNK), 1)
    tril = col <= row
    n_groups = w_ref.shape[0]
    for g in range(n_groups):
        wm_sc[g] = jnp.where(tril, w_ref[g], 0.0).astype(BF16)

    def chunk_body(n, carry):
        rows = pl.ds(pl.multiple_of(n * SG_CHUNK, SG_CHUNK), SG_CHUNK)
        for g in range(n_groups):
            lanes = slice(g * LANES, (g + 1) * LANES)
            mix = jnp.dot(wm_sc[g], v_ref[rows, lanes], preferred_element_type=F32) + b_ref[:, lanes]
            o_ref[rows, lanes] = (u_sc[rows, lanes].astype(F32) * mix).astype(o_ref.dtype)
        return carry

    lax.fori_loop(0, n_chunks, chunk_body, 0)


def gelu_spatial_gate(h, w_u, v, w_s, bias_rows, ts):
    G, S, D = h.shape
    W = w_u.shape[1]
    blk = pl.BlockSpec((None, ts, W), lambda g, s: (g, s, 0))
    return pl.pallas_call(
        functools.partial(_gelu_gate_kernel, n_chunks=ts // SG_CHUNK),
        grid=(G, S // ts),
        in_specs=[pl.BlockSpec((None, ts, D), lambda g, s: (g, s, 0)),
                  pl.BlockSpec(w_u.shape, lambda g, s: (0, 0)),
                  blk,
                  pl.BlockSpec(w_s.shape, lambda g, s: (0, 0, 0)),
                  pl.BlockSpec(bias_rows.shape, lambda g, s: (0, 0))],
        out_specs=blk,
        out_shape=jax.ShapeDtypeStruct((G, S, W), BF16),
        scratch_shapes=[pltpu.VMEM(w_s.shape, BF16), pltpu.VMEM((ts, W), BF16)],
        compiler_params=_params("arbitrary", "arbitrary"),
        name="gelu_spatial_gate",
    )(h, w_u, v, w_s, bias_rows)


def _spatial_gate_open_kernel(u_ref, v_ref, wl_ref, b_ref, o_ref):
    T = u_ref.shape[1]
    for i in range(T):
        acc = b_ref[i:i + 1, :] + wl_ref[i, 0:1, :] * v_ref[:, 0, :]
        for j in range(1, i + 1):
            acc = acc + wl_ref[i, j:j + 1, :] * v_ref[:, j, :]
        o_ref[:, i, :] = (u_ref[:, i, :].astype(F32) * acc).astype(o_ref.dtype)


def spatial_gate_open(u, v, w_lanes, bias_rows):
    B, T, W = u.shape
    full = lambda a: pl.BlockSpec(a.shape, lambda i: (0,) * a.ndim)
    return pl.pallas_call(
        _spatial_gate_open_kernel,
        grid=(1,),
        in_specs=[full(u), full(v), full(w_lanes), full(bias_rows)],
        out_specs=pl.BlockSpec((B, T, W), lambda i: (0, 0, 0)),
        out_shape=jax.ShapeDtypeStruct((B, T, W), BF16),
        compiler_params=_params("arbitrary"),
        name="spatial_gate_open",
    )(u, v, w_lanes, bias_rows)


def _attn_kernel(bounded_ref, q_ref, k_ref, v_ref, o_ref, m_sc, acc_sc, *, tq, tk, td):
    i = pl.program_id(2)
    dv = v_ref.shape[-1]
    unroll = tq // tk
    nt = (((1,), (1,)), ((), ()))
    ones = jnp.ones((tk, LANES), BF16)
    d0 = pl.multiple_of(i * tq, tq)
    chunk_mask = (lax.broadcasted_iota(jnp.int32, (td, td), 1) // CHUNK
                  <= lax.broadcasted_iota(jnp.int32, (td, td), 0) // CHUNK)

    def scores(rows0, k0, keys):
        return lax.dot_general(q_ref[rows0:, :], k_ref[pl.ds(k0, keys), :], nt, preferred_element_type=F32)

    def diag_scores(c):
        r0 = c * td
        s = scores(r0, d0 + r0, td)
        top = jnp.where(chunk_mask, s[:td], -jnp.inf)
        return top if r0 + td == tq else jnp.concatenate([top, s[td:]], axis=0)

    def values(k0, keys):
        return jnp.concatenate([v_ref[pl.ds(k0, keys), :], ones[:keys]], axis=1)

    def finish():
        acc = acc_sc[...]
        o_ref[...] = (acc[:, :dv] / acc[:, dv:]).astype(o_ref.dtype)

    @pl.when(bounded_ref[0] == 1)
    def _():
        for c in range(tq // td):
            r0 = c * td
            pv = jnp.dot(jnp.exp2(diag_scores(c)).astype(BF16), values(d0 + r0, td),
                         preferred_element_type=F32)
            if c == 0:
                acc_sc[...] = pv
            else:
                acc_sc[r0:, :] += pv

        def body(j, carry):
            k0 = pl.multiple_of(j * tq, tq)
            acc = acc_sc[...]
            for u in range(unroll):
                p = jnp.exp2(scores(0, k0 + u * tk, tk)).astype(BF16)
                acc = acc + jnp.dot(p, values(k0 + u * tk, tk), preferred_element_type=F32)
            acc_sc[...] = acc
            return carry

        lax.fori_loop(0, i, body, 0)
        finish()

    @pl.when(bounded_ref[0] != 1)
    def _():
        def step(s, k0, m_old, acc_old):
            rows, keys = s.shape
            row_max = jnp.max(s, axis=-1, keepdims=True)
            m_new = (jnp.broadcast_to(row_max, (rows, LANES)) if m_old is None
                     else jnp.maximum(m_old, row_max))
            p = jnp.concatenate([jnp.exp2(s[:, c * LANES:(c + 1) * LANES] - m_new)
                                 for c in range(keys // LANES)], axis=1).astype(BF16)
            pv = jnp.dot(p, values(k0, keys), preferred_element_type=F32)
            if m_old is None:
                return m_new, pv
            alpha = jnp.exp2(m_old - m_new)
            return m_new, jnp.concatenate([alpha] * ((dv + LANES) // LANES), axis=1) * acc_old + pv

        for c in range(tq // td):
            r0 = c * td
            if c == 0:
                m, acc = step(diag_scores(c), d0, None, None)
            else:
                m, acc = step(diag_scores(c), d0 + r0, m_sc[r0:, :], acc_sc[r0:, :])
            m_sc[r0:, :] = m
            acc_sc[r0:, :] = acc

        def body(j, carry):
            k0 = pl.multiple_of(j * tq, tq)
            ss = [scores(0, k0 + u * tk, tk) for u in range(unroll)]
            m, acc = m_sc[...], acc_sc[...]
            for u in range(unroll):
                m, acc = step(ss[u], k0 + u * tk, m, acc)
            m_sc[...] = m
            acc_sc[...] = acc
            return carry

        lax.fori_loop(0, i, body, 0)
        finish()


def attention_prompt(bounded, q, k, v, tq, tk, td):
    G, H, S, _ = q.shape
    dv = v.shape[-1]
    assert dv == LANES and tq % tk == 0 and tq % td == 0 and td % LANES == 0 and td <= tk
    return pl.pallas_call(
        functools.partial(_attn_kernel, tq=tq, tk=tk, td=td),
        grid=(G, H, S // tq),
        in_specs=[pl.BlockSpec(memory_space=pltpu.SMEM),
                  pl.BlockSpec((None, None, tq, HEAD_PAD), lambda g, h, i: (g, h, i, 0)),
                  pl.BlockSpec((None, None, S, HEAD_PAD), lambda g, h, i: (g, h, 0, 0)),
                  pl.BlockSpec((None, None, S, dv), lambda g, h, i: (g, h, 0, 0))],
        out_specs=pl.BlockSpec((None, tq, dv), lambda g, h, i: (g, i, h)),
        out_shape=jax.ShapeDtypeStruct((G, S, H * dv), BF16),
        scratch_shapes=[pltpu.VMEM((tq, LANES), F32), pltpu.VMEM((tq, dv + LANES), F32)],
        compiler_params=_params("arbitrary", "arbitrary", "arbitrary"),
        name="attention_prompt",
    )(bounded, q, k, v)


def _attn_cached_kernel(q_ref, cache_ref, ckr_ref, cnew_ref, krnew_ref, wuk_ref, wuv_ref, gkn_ref,
                        o_ref, call_sc, krall_sc, p_sc, *, past, n_new):
    H = q_ref.shape[0]
    L = past + n_new
    Lp = call_sc.shape[0]
    lat = call_sc.shape[1]
    call_sc[0:past, :] = cache_ref[...].astype(BF16)
    call_sc[past:L, :] = cnew_ref[...].astype(BF16)
    call_sc[L:Lp, :] = jnp.zeros((Lp - L, lat), BF16)
    n_rope = ckr_ref.shape[-1]
    krall_sc[0:past, n_rope:] = jnp.zeros((past, LANES - n_rope), BF16)
    krall_sc[0:past, :n_rope] = ckr_ref[...].astype(BF16)
    krall_sc[past:L, :] = krnew_ref[...]
    krall_sc[L:Lp, :] = jnp.zeros((Lp - L, LANES), BF16)
    call = call_sc[...]
    krall = krall_sc[...]
    gkn = gkn_ref[...]
    valid = lax.broadcasted_iota(jnp.int32, (n_new, Lp), 1) < L
    nt = (((1,), (1,)), ((), ()))
    group = 4
    for grp in range(H // group):
        cols = slice(grp * group * LANES, (grp + 1) * group * LANES)
        kn_g = jnp.dot(call, wuk_ref[:, cols], preferred_element_type=F32)
        for sub in range(group):
            hh = group * grp + sub
            kn = _rms(kn_g[:, sub * LANES:(sub + 1) * LANES], gkn).astype(BF16)
            kh = jnp.concatenate([kn, krall], axis=1)
            s = lax.dot_general(q_ref[hh], kh, nt, preferred_element_type=F32)
            s = jnp.where(valid, s, -jnp.inf)
            m = jnp.max(s, axis=-1, keepdims=True)
            p = jnp.exp2(s - m)
            p = p / jnp.sum(p, axis=-1, keepdims=True)
            p_sc[hh * n_new:(hh + 1) * n_new, :] = p.astype(BF16)
    o_lat = jnp.dot(p_sc[...], call, preferred_element_type=F32).astype(BF16)
    for hh in range(H):
        o_ref[:, hh * LANES:(hh + 1) * LANES] = jnp.dot(
            o_lat[hh * n_new:(hh + 1) * n_new, :], wuv_ref[:, hh * LANES:(hh + 1) * LANES],
            preferred_element_type=F32).astype(o_ref.dtype)


def attention_cached(q, cache_lat, cache_kr, layer, c_new, kr_new_pad, w_uk, w_uv, g_k_nope, n_new):
    _, H, BT, _ = q.shape
    _, B, past, lat = cache_lat.shape
    n_rope = cache_kr.shape[-1]
    L = past + n_new
    Lp = -(-L // LANES) * LANES
    const = lambda b: (0, 0)
    return pl.pallas_call(
        functools.partial(_attn_cached_kernel, past=past, n_new=n_new),
        grid=(B,),
        in_specs=[pl.BlockSpec((None, H, n_new, HEAD_PAD), lambda b: (0, 0, b, 0)),
                  pl.BlockSpec((None, None, past, lat), lambda b: (layer, b, 0, 0)),
                  pl.BlockSpec((None, None, past, n_rope), lambda b: (layer, b, 0, 0)),
                  pl.BlockSpec((None, n_new, lat), lambda b: (0, b, 0)),
                  pl.BlockSpec((None, n_new, LANES), lambda b: (0, b, 0)),
                  pl.BlockSpec(w_uk.shape, const), pl.BlockSpec(w_uv.shape, const),
                  pl.BlockSpec((1, LANES), const)],
        out_specs=pl.BlockSpec((None, n_new, H * LANES), lambda b: (0, b, 0)),
        out_shape=jax.ShapeDtypeStruct((1, BT, H * LANES), BF16),
        scratch_shapes=[pltpu.VMEM((Lp, lat), BF16), pltpu.VMEM((Lp, LANES), BF16),
                        pltpu.VMEM((H * n_new, Lp), BF16)],
        compiler_params=_params("arbitrary"),
        name="attention_cached",
    )(q, cache_lat, cache_kr, c_new, kr_new_pad, w_uk, w_uv, g_k_nope.reshape(1, LANES))


def _merge_kernel(a_ref, b_ref, ga_ref, gb_ref, wpa_ref, wpb_ref, o_ref):
    pa = jnp.dot(a_ref[...], wpa_ref[...], preferred_element_type=F32)
    pb = jnp.dot(b_ref[...], wpb_ref[...], preferred_element_type=F32)
    o_ref[...] = (ga_ref[...].astype(F32) * pa + gb_ref[...].astype(F32) * pb).astype(o_ref.dtype)


def merge_branches(o_sg, o_mla, gates, w_pa, w_pb, ts, tn):
    G, S, W = o_sg.shape
    D = w_pa.shape[1]
    nj = D // tn
    row = lambda g, s, j: (g, s, 0)
    return pl.pallas_call(
        _merge_kernel,
        grid=(G, S // ts, nj),
        in_specs=[pl.BlockSpec((None, ts, W), row),
                  pl.BlockSpec((None, ts, o_mla.shape[-1]), row),
                  pl.BlockSpec((None, ts, tn), lambda g, s, j: (g, s, j)),
                  pl.BlockSpec((None, ts, tn), lambda g, s, j: (g, s, j + nj)),
                  pl.BlockSpec((W, tn), lambda g, s, j: (0, j)),
                  pl.BlockSpec((o_mla.shape[-1], tn), lambda g, s, j: (0, j))],
        out_specs=pl.BlockSpec((None, ts, tn), lambda g, s, j: (g, s, j)),
        out_shape=jax.ShapeDtypeStruct((G, S, D), BF16),
        compiler_params=_params("arbitrary", "arbitrary", "arbitrary"),
        name="merge_branches",
    )(o_sg, o_mla, gates, gates, w_pa, w_pb)


def _out_proj_kernel(m_ref, wo_ref, x_ref, g1_ref, gn_ref, sc_ref, sh_ref, x1_ref, h2_ref):
    y = jnp.dot(m_ref[...], wo_ref[...], preferred_element_type=F32)
    x1 = x_ref[...] + g1_ref[...] * y
    x1_ref[...] = x1
    h2_ref[...] = (_rms(x1, gn_ref[...]) * (1.0 + sc_ref[...]) + sh_ref[...]).astype(h2_ref.dtype)


def out_project(m, w_o, x, g1, g_norm2, sc2, sh2, ts):
    G, S, D = x.shape
    row = pl.BlockSpec((None, ts, D), lambda g, s: (g, s, 0))
    return pl.pallas_call(
        _out_proj_kernel,
        grid=(G, S // ts),
        in_specs=[row, pl.BlockSpec(w_o.shape, lambda g, s: (0, 0)), row,
                  _mod_spec(g1, ts), pl.BlockSpec((1, D), lambda g, s: (0, 0)),
                  _mod_spec(sc2, ts), _mod_spec(sh2, ts)],
        out_specs=[row, row],
        out_shape=[jax.ShapeDtypeStruct((G, S, D), F32), jax.ShapeDtypeStruct((G, S, D), BF16)],
        compiler_params=_params("arbitrary", "arbitrary"),
        name="out_project",
    )(m, w_o, x, g1, g_norm2.reshape(1, D), sc2, sh2)


def _mlp_kernel(h_ref, wup_ref, wdn_ref, x1_ref, g2_ref, o_ref):
    j = pl.program_id(2)
    hid = jnp.dot(h_ref[...], wup_ref[...], preferred_element_type=F32)
    hid = jnp.square(jnp.maximum(hid, 0.0)).astype(BF16)
    part = jnp.dot(hid, wdn_ref[...], preferred_element_type=F32)

    @pl.when(j == 0)
    def _():
        o_ref[...] = part

    @pl.when(j > 0)
    def _():
        o_ref[...] += part

    @pl.when(j == pl.num_programs(2) - 1)
    def _():
        o_ref[...] = x1_ref[...] + g2_ref[...] * o_ref[...]


def mlp_residual(h2, w_up, w_down, x1, g2, ts, th):
    G, S, D = x1.shape
    hidden = w_up.shape[1]
    row = pl.BlockSpec((None, ts, D), lambda g, s, j: (g, s, 0))
    return pl.pallas_call(
        _mlp_kernel,
        grid=(G, S // ts, hidden // th),
        in_specs=[row,
                  pl.BlockSpec((D, th), lambda g, s, j: (0, j)),
                  pl.BlockSpec((th, D), lambda g, s, j: (j, 0)),
                  row, _mod_spec(g2, ts)],
        out_specs=row,
        out_shape=jax.ShapeDtypeStruct((G, S, D), F32),
        compiler_params=_params("arbitrary", "arbitrary", "arbitrary"),
        name="mlp_residual",
    )(h2, w_up, w_down, x1, g2)


def _rope_tables(pos, n_rope):
    inv = jnp.float32(ROPE_BASE) ** (-jnp.arange(0, n_rope, 2, dtype=F32) / n_rope)
    ang = pos.astype(F32)[:, None] * inv[None, :]
    cos, sin = lax.optimization_barrier((jnp.cos(ang), jnp.sin(ang)))
    pad = jnp.zeros((pos.shape[0], LANES - n_rope), F32)
    return (jnp.concatenate([cos, cos, pad], axis=1), jnp.concatenate([-sin, sin, pad], axis=1))


def _dup_lanes(g):
    return jnp.concatenate([g.astype(F32), g.astype(F32)]).reshape(1, 2 * g.shape[0])


def _layer(x, mods, pos, P, ts, *, cache=None):
    sh1, sc1, g1, sh2, sc2, g2 = mods
    n_rope = P["n_rope"]
    tabs = _rope_tables(pos, n_rope)
    h, q = prenorm_q_project(x, P["g_norm1"], sc1, sh1, P["w_q"], P["g_q_a"], P["w_uq"], P["g_q_nope"],
                             P["g_q_rope"], tabs, ts, P["q_scale"])
    gates = proj_act(h, P["w_g"], "sigmoid", ts, P["w_g"].shape[1] // 2)
    if cache is None:
        v = proj_gelu_norm(h, P["w_v"], P["g_sg"], ts, BF16)
        o_sg = gelu_spatial_gate(h, P["w_u"], v, P["w_s"], P["b_rows"], ts)
        c_kv, k_rope, k, vv = kv_project(h, P["w_kv"], P["g_kv_a"], P["g_k_rope"], tabs, ts, n_rope,
                                         expand_weights=(P["w_uk"], P["w_uv"], P["g_k_nope"], P["k_shift"]))
        o_mla = attention_prompt(P["bounded"], q, k, vv, tq=min(2048, x.shape[1]),
                                 tk=min(1024, x.shape[1]), td=min(512, x.shape[1]))
        extra = ()
    else:
        cache_lat, cache_kr, layer, B, T = cache
        u = proj_act(h, P["w_u"], "gelu", ts, P["w_u"].shape[1])
        v = proj_gelu_norm(h, P["w_v"], P["g_sg"], ts, F32)
        W = v.shape[-1]
        o_sg = spatial_gate_open(u.reshape(B, T, W), v.reshape(B, T, W), P["w_lanes"][:T, :T],
                                 P["b_rows"][:T]).reshape(1, B * T, W)
        c_kv, k_rope, kr_pad = kv_project(h, P["w_kv"], P["g_kv_a"], P["g_k_rope"], tabs, ts, n_rope)
        o_mla = attention_cached(q, cache_lat, cache_kr, layer, c_kv, kr_pad, P["w_uk"], P["w_uv"],
                                 P["g_k_nope"], T)
        extra = (v,)
    m = merge_branches(o_sg, o_mla, gates, P["w_pa"], P["w_pb"], ts, P["w_pa"].shape[1] // 2)
    ts_res = min(ts, 512)
    x1, h2 = out_project(m, P["w_o"], x, g1, P["g_norm2"], sc2, sh2, ts_res)
    y = mlp_residual(h2, P["w_up"], P["w_down"], x1, g2, ts_res, min(2048, P["w_up"].shape[1]))
    return (y, c_kv, k_rope) + extra


def kernel(x_prompt, x_sample, cache_kv_latent, cache_k_rope, c_prompt, c_sample, w_ada, b_ada, g_norm1, g_norm2, w_in, g_sg, w_s, b_s, g_q_a, w_uq, g_q_nope, g_q_rope, g_kv_a, g_k_rope, w_uk, g_k_nope, w_uv, w_pa, w_pb, w_o, w_up, w_down):
    depth = w_in.shape[0]
    Bp, S, D = x_prompt.shape
    Bs, T, _ = x_sample.shape
    past = cache_kv_latent.shape[2]
    q_lora = g_q_a.shape[-1]
    lat = g_kv_a.shape[-1]
    n_nope = g_q_nope.shape[-1]
    n_rope = g_q_rope.shape[-1]
    H = w_uk.shape[2]
    sg_w = g_sg.shape[-1]
    off_q = 2 * sg_w
    off_kv = off_q + q_lora
    off_gate = off_kv + lat + n_rope
    assert n_nope == LANES and w_uv.shape[-1] == LANES and 2 * n_rope == LANES
    assert sg_w // SG_GROUPS == LANES and H == N_HEADS

    nb = Bp + Bs
    nb_pad = -(-nb // 8) * 8
    c_all = jnp.concatenate([c_prompt, c_sample, jnp.zeros((nb_pad - nb, D), F32)], axis=0)

    y_p, y_s = x_prompt, x_sample.reshape(1, Bs * T, D)
    outs = [[] for _ in range(5)]
    for l in range(depth):
        wi = w_in[l].astype(BF16)
        w_uq_l = w_uq[l].astype(BF16).reshape(q_lora, H, n_nope + n_rope)
        w_uq_pad = jnp.concatenate([w_uq_l, w_uq_l[:, :, n_nope:]], axis=2)
        q_scale = float((n_nope + n_rope) ** -0.5 * LOG2E)
        def sq_norm_bound(g_nope, g_rope):
            return n_nope * jnp.max(jnp.square(g_nope)) + n_rope * jnp.max(jnp.square(g_rope))
        score_bound = q_scale * jnp.sqrt(sq_norm_bound(g_q_nope[l], g_q_rope[l])
                                         * sq_norm_bound(g_k_nope[l], g_k_rope[l]))
        bounded = (score_bound <= BOUND_MAX).astype(jnp.int32).reshape(1)
        k_shift = jnp.where(jnp.arange(LANES) == LANES // 2, -score_bound, 0.0).astype(F32).reshape(1, LANES)
        P = {
            "n_rope": n_rope,
            "q_scale": q_scale, "k_shift": k_shift, "bounded": bounded,
            "g_norm1": g_norm1[l], "g_norm2": g_norm2[l], "g_sg": g_sg[l],
            "w_u": wi[:, :sg_w], "w_v": wi[:, sg_w:off_q], "w_q": wi[:, off_q:off_kv],
            "w_kv": jnp.concatenate([wi[:, off_kv:off_gate], wi[:, off_kv + lat:off_gate]], axis=1),
            "w_g": wi[:, off_gate:],
            "g_q_a": g_q_a[l], "w_uq": w_uq_pad.reshape(q_lora, H * HEAD_PAD),
            "g_q_nope": g_q_nope[l], "g_q_rope": _dup_lanes(g_q_rope[l]),
            "g_kv_a": g_kv_a[l], "g_k_rope": _dup_lanes(g_k_rope[l]),
            "w_uk": w_uk[l].astype(BF16).reshape(lat, H * n_nope),
            "w_uv": w_uv[l].astype(BF16).reshape(lat, H * LANES),
            "g_k_nope": g_k_nope[l],
            "w_s": w_s[l],
            "b_rows": jnp.repeat(b_s[l].T, LANES, axis=1),
            "w_lanes": jnp.repeat(w_s[l][:, :T, :T].transpose(1, 2, 0), LANES, axis=2),
            "w_pa": w_pa[l].astype(BF16), "w_pb": w_pb[l].astype(BF16), "w_o": w_o[l].astype(BF16),
            "w_up": w_up[l].astype(BF16), "w_down": w_down[l].astype(BF16),
        }
        mod = ada_project(c_all, w_ada[l], b_ada[l])
        mods_p = [a.reshape(Bp, 1, D) for a in jnp.split(mod[:Bp], 6, axis=-1)]
        mods_s = [jnp.repeat(a, T, axis=0).reshape(1, Bs * T, D)
                  for a in jnp.split(mod[Bp:nb], 6, axis=-1)]

        ts_p = min(1024, S)
        y_p, lp, kp = _layer(y_p, mods_p, jnp.arange(S), P, ts_p)
        pos_s = jnp.tile(past + jnp.arange(T), Bs)
        y_s, ls, ks, vs = _layer(y_s, mods_s, pos_s, P, Bs * T,
                                 cache=(cache_kv_latent, cache_k_rope, l, Bs, T))
        for lst, a in zip(outs, (lp, kp, ls.reshape(Bs, T, lat), ks.reshape(Bs, T, n_rope),
                                 vs.reshape(Bs, T, sg_w))):
            lst.append(a)
    return (y_p, y_s.reshape(Bs, T, D)) + tuple(o[0][None] if depth == 1 else jnp.stack(o) for o in outs)
```

```python
import functools
import math

import jax
import jax.numpy as jnp
import numpy as np
from jax import lax
from jax.experimental import pallas as pl
from jax.experimental.pallas import tpu as pltpu

F32 = jnp.float32
BF16 = jnp.bfloat16

EPS = 1e-6
ROPE_BASE = 10000.0
N_HEADS = 16
CHUNK = 64
SG_CHUNK = 128
SG_GROUPS = 16
LANES = 128
HEAD_PAD = 256
LOG2E = math.log2(math.e)
BOUND_MAX = 50.0

VMEM_LIMIT = 58 * 1024 * 1024


def _params(*sem):
    return pltpu.CompilerParams(dimension_semantics=sem, vmem_limit_bytes=VMEM_LIMIT)


def _rms(x, g):
    ms = jnp.mean(x * x, axis=-1, keepdims=True)
    return x * lax.rsqrt(ms + EPS) * g


def _rms_rope_group(t, g_dup):
    ms = jnp.mean(t * t, axis=-1, keepdims=True)
    return t * lax.rsqrt(ms + EPS) * g_dup


def _rope_group(t, tc, ts):
    return t * tc + pltpu.roll(t, LANES - LANES // 4, 1) * ts


def _gelu(z):
    return 0.5 * z * (1.0 + lax.erf(z * np.float32(math.sqrt(0.5))))


def _mod_spec(mod, ts):
    d = mod.shape[-1]
    if mod.shape[1] == 1:
        return pl.BlockSpec((None, 1, d), lambda g, s, *_: (g, 0, 0))
    return pl.BlockSpec((None, ts, d), lambda g, s, *_: (g, s, 0))


def _ada_kernel(c_ref, w_ref, b_ref, o_ref):
    c = c_ref[...]
    s = (c * jax.nn.sigmoid(c)).astype(BF16)
    o_ref[...] = jnp.dot(s, w_ref[...].astype(BF16), preferred_element_type=F32) + b_ref[...]


def ada_project(c, w_ada, b_ada, tn=1024):
    r, d = c.shape
    n = w_ada.shape[1]
    return pl.pallas_call(
        _ada_kernel,
        grid=(n // tn,),
        in_specs=[pl.BlockSpec((r, d), lambda j: (0, 0)),
                  pl.BlockSpec((d, tn), lambda j: (0, j)),
                  pl.BlockSpec((1, tn), lambda j: (0, j))],
        out_specs=pl.BlockSpec((r, tn), lambda j: (0, j)),
        out_shape=jax.ShapeDtypeStruct((r, n), F32),
        compiler_params=_params("arbitrary"),
        name="ada_project",
    )(c, w_ada, b_ada.reshape(1, n))


def _proj_act_kernel(h_ref, w_ref, o_ref, *, act):
    z = jnp.dot(h_ref[...], w_ref[...], preferred_element_type=F32)
    if act == "gelu":
        a = _gelu(z)
    else:
        a = jax.nn.sigmoid(z)
    o_ref[...] = a.astype(o_ref.dtype)


def proj_act(h, w, act, ts, tn):
    G, S, D = h.shape
    n = w.shape[1]
    return pl.pallas_call(
        functools.partial(_proj_act_kernel, act=act),
        grid=(G, S // ts, n // tn),
        in_specs=[pl.BlockSpec((None, ts, D), lambda g, s, j: (g, s, 0)),
                  pl.BlockSpec((D, tn), lambda g, s, j: (0, j))],
        out_specs=pl.BlockSpec((None, ts, tn), lambda g, s, j: (g, s, j)),
        out_shape=jax.ShapeDtypeStruct((G, S, n), BF16),
        compiler_params=_params("arbitrary", "arbitrary", "arbitrary"),
        name="proj_" + act,
    )(h, w)


def _proj_gelu_norm_kernel(h_ref, w_ref, g_ref, o_ref):
    z = jnp.dot(h_ref[...], w_ref[...], preferred_element_type=F32)
    o_ref[...] = _rms(_gelu(z), g_ref[...]).astype(o_ref.dtype)


def proj_gelu_norm(h, w, g, ts, out_dtype):
    G, S, D = h.shape
    n = w.shape[1]
    return pl.pallas_call(
        _proj_gelu_norm_kernel,
        grid=(G, S // ts),
        in_specs=[pl.BlockSpec((None, ts, D), lambda g_, s: (g_, s, 0)),
                  pl.BlockSpec((D, n), lambda g_, s: (0, 0)),
                  pl.BlockSpec((1, n), lambda g_, s: (0, 0))],
        out_specs=pl.BlockSpec((None, ts, n), lambda g_, s: (g_, s, 0)),
        out_shape=jax.ShapeDtypeStruct((G, S, n), out_dtype),
        compiler_params=_params("arbitrary", "arbitrary"),
        name="proj_gelu_norm",
    )(h, w, g.reshape(1, n))


def _q_proj_kernel(x_ref, g1_ref, sc_ref, sh_ref, wq_ref, gqa_ref, wuq_ref, gn_ref, gr_ref, tc_ref, ts_ref,
                   h_ref, q_ref, *, q_scale):
    h = (_rms(x_ref[...], g1_ref[...]) * (1.0 + sc_ref[...]) + sh_ref[...]).astype(h_ref.dtype)
    h_ref[...] = h
    zq = jnp.dot(h, wq_ref[...], preferred_element_type=F32)
    zn = _rms(zq, gqa_ref[...]).astype(BF16)
    tc, ts = tc_ref[...], ts_ref[...]
    gains = jnp.concatenate([gn_ref[...], gr_ref[...]], axis=1) * q_scale
    same_tile = (lax.broadcasted_iota(jnp.int32, (HEAD_PAD, HEAD_PAD), 0) // LANES
                 == lax.broadcasted_iota(jnp.int32, (HEAD_PAD, HEAD_PAD), 1) // LANES)
    tile_mean = jnp.where(same_tile, 1.0 / LANES, 0.0).astype(BF16)
    shift_lane = jnp.where(lax.broadcasted_iota(jnp.int32, (1, LANES), 1) == LANES // 2, 1.0, 0.0)
    for pair in range(q_ref.shape[0] // 2):
        blk2 = jnp.dot(zn, wuq_ref[:, pair * 2 * HEAD_PAD:(pair + 1) * 2 * HEAD_PAD],
                       preferred_element_type=F32)
        for sub in range(2):
            hh = 2 * pair + sub
            blk = blk2[:, sub * HEAD_PAD:(sub + 1) * HEAD_PAD]
            ms = jnp.dot((blk * blk).astype(BF16), tile_mean, preferred_element_type=F32)
            y = blk * lax.rsqrt(ms + EPS) * gains
            q_ref[hh, :, :LANES] = y[:, :LANES].astype(q_ref.dtype)
            q_ref[hh, :, LANES:] = (_rope_group(y[:, LANES:], tc, ts) + shift_lane).astype(q_ref.dtype)


def prenorm_q_project(x, g_norm, sc, sh, w_q, g_q_a, w_uq_pad, g_nope, g_rope_pad, tabs, ts, q_scale):
    G, S, D = x.shape
    ql = w_q.shape[1]
    H = w_uq_pad.shape[1] // HEAD_PAD
    const = lambda g, s: (0, 0)
    tab_spec = pl.BlockSpec((ts, LANES), lambda g, s: (s, 0))
    row = pl.BlockSpec((None, ts, D), lambda g, s: (g, s, 0))
    return pl.pallas_call(
        functools.partial(_q_proj_kernel, q_scale=q_scale),
        grid=(G, S // ts),
        in_specs=[row, pl.BlockSpec((1, D), const), _mod_spec(sc, ts), _mod_spec(sh, ts),
                  pl.BlockSpec((D, ql), const),
                  pl.BlockSpec((1, ql), const),
                  pl.BlockSpec((ql, H * HEAD_PAD), const),
                  pl.BlockSpec((1, LANES), const),
                  pl.BlockSpec((1, LANES), const),
                  tab_spec, tab_spec],
        out_specs=[row, pl.BlockSpec((None, H, ts, HEAD_PAD), lambda g, s: (g, 0, s, 0))],
        out_shape=[jax.ShapeDtypeStruct((G, S, D), BF16),
                   jax.ShapeDtypeStruct((G, H, S, HEAD_PAD), BF16)],
        compiler_params=_params("arbitrary", "arbitrary"),
        name="prenorm_q_project",
    )(x, g_norm.reshape(1, D), sc, sh, w_q, g_q_a.reshape(1, ql), w_uq_pad, g_nope.reshape(1, LANES),
      g_rope_pad, *tabs)


def _kv_proj_kernel(h_ref, wkv_ref, gkva_ref, gkr_ref, tc_ref, ts_ref, *rest, n_rope, expand):
    if expand:
        wuk_ref, wuv_ref, gkn_ref, kshift_ref, c_ref, kr_ref, k_ref, v_ref = rest
    else:
        c_ref, kr_ref, krp_ref = rest
    lat = c_ref.shape[-1]
    z = jnp.dot(h_ref[...], wkv_ref[...], preferred_element_type=F32)
    c = _rms(z[:, :lat], gkva_ref[...])
    c_ref[...] = c
    rope = _rope_group(_rms_rope_group(z[:, lat:], gkr_ref[...]), tc_ref[...], ts_ref[...])
    kr_ref[...] = rope[:, :n_rope]
    rope_b = rope.astype(BF16)
    if not expand:
        krp_ref[...] = rope_b
        return
    cb = c.astype(BF16)
    gkn = gkn_ref[...]
    rope_b = (rope + kshift_ref[...]).astype(BF16)
    for pair in range(k_ref.shape[0] // 2):
        cols = slice(pair * 2 * LANES, (pair + 1) * 2 * LANES)
        kn2 = jnp.dot(cb, wuk_ref[:, cols], preferred_element_type=F32)
        v2 = jnp.dot(cb, wuv_ref[:, cols], preferred_element_type=F32)
        for sub in range(2):
            hh = 2 * pair + sub
            lanes = slice(sub * LANES, (sub + 1) * LANES)
            k_ref[hh, :, :LANES] = _rms(kn2[:, lanes], gkn).astype(k_ref.dtype)
            k_ref[hh, :, LANES:] = rope_b
            v_ref[hh] = v2[:, lanes].astype(v_ref.dtype)


def kv_project(h, w_kv, g_kv_a, g_k_rope_pad, tabs, ts, n_rope, expand_weights=None):
    G, S, D = h.shape
    lat = g_kv_a.shape[-1]
    const = lambda g, s: (0, 0)
    tab_spec = pl.BlockSpec((ts, LANES), lambda g, s: (s, 0))
    in_specs = [pl.BlockSpec((None, ts, D), lambda g, s: (g, s, 0)),
                pl.BlockSpec((D, lat + LANES), const),
                pl.BlockSpec((1, lat), const),
                pl.BlockSpec((1, LANES), const),
                tab_spec, tab_spec]
    args = [h, w_kv, g_kv_a.reshape(1, lat), g_k_rope_pad, *tabs]
    out_specs = [pl.BlockSpec((None, ts, lat), lambda g, s: (g, s, 0)),
                 pl.BlockSpec((None, ts, n_rope), lambda g, s: (g, s, 0))]
    out_shape = [jax.ShapeDtypeStruct((G, S, lat), F32),
                 jax.ShapeDtypeStruct((G, S, n_rope), F32)]
    expand = expand_weights is not None
    if expand:
        w_uk, w_uv, g_k_nope, k_shift = expand_weights
        H = w_uk.shape[1] // LANES
        in_specs += [pl.BlockSpec(w_uk.shape, const), pl.BlockSpec(w_uv.shape, const),
                     pl.BlockSpec((1, LANES), const), pl.BlockSpec((1, LANES), const)]
        args += [w_uk, w_uv, g_k_nope.reshape(1, LANES), k_shift]
        out_specs += [pl.BlockSpec((None, H, ts, HEAD_PAD), lambda g, s: (g, 0, s, 0)),
                      pl.BlockSpec((None, H, ts, LANES), lambda g, s: (g, 0, s, 0))]
        out_shape += [jax.ShapeDtypeStruct((G, H, S, HEAD_PAD), BF16),
                      jax.ShapeDtypeStruct((G, H, S, LANES), BF16)]
    else:
        out_specs.append(pl.BlockSpec((None, ts, LANES), lambda g, s: (g, s, 0)))
        out_shape.append(jax.ShapeDtypeStruct((G, S, LANES), BF16))
    return pl.pallas_call(
        functools.partial(_kv_proj_kernel, n_rope=n_rope, expand=expand),
        grid=(G, S // ts),
        in_specs=in_specs,
        out_specs=out_specs,
        out_shape=out_shape,
        compiler_params=_params("arbitrary", "arbitrary"),
        name="kv_project",
    )(*args)


def _gelu_gate_kernel(h_ref, wu_ref, v_ref, w_ref, b_ref, o_ref, wm_sc, u_sc, *, n_chunks):
    u_sc[...] = _gelu(jnp.dot(h_ref[...], wu_ref[...], preferred_element_type=F32)).astype(u_sc.dtype)
    row = lax.broadcasted_iota(jnp.int32, (SG_CHUNK, SG_CHUNK), 0)
    col = lax.broadcasted_iota(jnp.int32, (SG_CHUNK, SG_CHUNK), 1)
    tril = col <= row
    n_groups = w_ref.shape[0]
    for g in range(n_groups):
        wm_sc[g] = jnp.where(tril, w_ref[g], 0.0).astype(BF16)

    def chunk_body(n, carry):
        rows = pl.ds(pl.multiple_of(n * SG_CHUNK, SG_CHUNK), SG_CHUNK)
        for g in range(n_groups):
            lanes = slice(g * LANES, (g + 1) * LANES)
            mix = jnp.dot(wm_sc[g], v_ref[rows, lanes], preferred_element_type=F32) + b_ref[:, lanes]
            o_ref[rows, lanes] = (u_sc[rows, lanes].astype(F32) * mix).astype(o_ref.dtype)
        return carry

    lax.fori_loop(0, n_chunks, chunk_body, 0)


def gelu_spatial_gate(h, w_u, v, w_s, bias_rows, ts):
    G, S, D = h.shape
    W = w_u.shape[1]
    blk = pl.BlockSpec((None, ts, W), lambda g, s: (g, s, 0))
    return pl.pallas_call(
        functools.partial(_gelu_gate_kernel, n_chunks=ts // SG_CHUNK),
        grid=(G, S // ts),
        in_specs=[pl.BlockSpec((None, ts, D), lambda g, s: (g, s, 0)),
                  pl.BlockSpec(w_u.shape, lambda g, s: (0, 0)),
                  blk,
                  pl.BlockSpec(w_s.shape, lambda g, s: (0, 0, 0)),
                  pl.BlockSpec(bias_rows.shape, lambda g, s: (0, 0))],
        out_specs=blk,
        out_shape=jax.ShapeDtypeStruct((G, S, W), BF16),
        scratch_shapes=[pltpu.VMEM(w_s.shape, BF16), pltpu.VMEM((ts, W), BF16)],
        compiler_params=_params("arbitrary", "arbitrary"),
        name="gelu_spatial_gate",
    )(h, w_u, v, w_s, bias_rows)


def _spatial_gate_open_kernel(u_ref, v_ref, wl_ref, b_ref, o_ref):
    T = u_ref.shape[1]
    for i in range(T):
        acc = b_ref[i:i + 1, :] + wl_ref[i, 0:1, :] * v_ref[:, 0, :]
        for j in range(1, i + 1):
            acc = acc + wl_ref[i, j:j + 1, :] * v_ref[:, j, :]
        o_ref[:, i, :] = (u_ref[:, i, :].astype(F32) * acc).astype(o_ref.dtype)


def spatial_gate_open(u, v, w_lanes, bias_rows):
    B, T, W = u.shape
    full = lambda a: pl.BlockSpec(a.shape, lambda i: (0,) * a.ndim)
    return pl.pallas_call(
        _spatial_gate_open_kernel,
        grid=(1,),
        in_specs=[full(u), full(v), full(w_lanes), full(bias_rows)],
        out_specs=pl.BlockSpec((B, T, W), lambda i: (0, 0, 0)),
        out_shape=jax.ShapeDtypeStruct((B, T, W), BF16),
        compiler_params=_params("arbitrary"),
        name="spatial_gate_open",
    )(u, v, w_lanes, bias_rows)


def _attn_kernel(bounded_ref, q_ref, k_ref, v_ref, o_ref, m_sc, acc_sc, *, tq, tk, td):
    i = pl.program_id(2)
    dv = v_ref.shape[-1]
    unroll = tq // tk
    nt = (((1,), (1,)), ((), ()))
    ones = jnp.ones((tk, LANES), BF16)
    d0 = pl.multiple_of(i * tq, tq)
    chunk_mask = (lax.broadcasted_iota(jnp.int32, (td, td), 1) // CHUNK
                  <= lax.broadcasted_iota(jnp.int32, (td, td), 0) // CHUNK)

    def scores(rows0, k0, keys):
        return lax.dot_general(q_ref[rows0:, :], k_ref[pl.ds(k0, keys), :], nt, preferred_element_type=F32)

    def diag_scores(c):
        r0 = c * td
        s = scores(r0, d0 + r0, td)
        top = jnp.where(chunk_mask, s[:td], -jnp.inf)
        return top if r0 + td == tq else jnp.concatenate([top, s[td:]], axis=0)

    def values(k0, keys):
        return jnp.concatenate([v_ref[pl.ds(k0, keys), :], ones[:keys]], axis=1)

    def finish():
        acc = acc_sc[...]
        o_ref[...] = (acc[:, :dv] / acc[:, dv:]).astype(o_ref.dtype)

    @pl.when(bounded_ref[0] == 1)
    def _():
        for c in range(tq // td):
            r0 = c * td
            pv = jnp.dot(jnp.exp2(diag_scores(c)).astype(BF16), values(d0 + r0, td),
                         preferred_element_type=F32)
            if c == 0:
                acc_sc[...] = pv
            else:
                acc_sc[r0:, :] += pv

        def body(j, carry):
            k0 = pl.multiple_of(j * tq, tq)
            acc = acc_sc[...]
            for u in range(unroll):
                p = jnp.exp2(scores(0, k0 + u * tk, tk)).astype(BF16)
                acc = acc + jnp.dot(p, values(k0 + u * tk, tk), preferred_element_type=F32)
            acc_sc[...] = acc
            return carry

        lax.fori_loop(0, i, body, 0)
        finish()

    @pl.when(bounded_ref[0] != 1)
    def _():
        def step(s, k0, m_old, acc_old):
            rows, keys = s.shape
            row_max = jnp.max(s, axis=-1, keepdims=True)
            m_new = (jnp.broadcast_to(row_max, (rows, LANES)) if m_old is None
                     else jnp.maximum(m_old, row_max))
            p = jnp.concatenate([jnp.exp2(s[:, c * LANES:(c + 1) * LANES] - m_new)
                                 for c in range(keys // LANES)], axis=1).astype(BF16)
            pv = jnp.dot(p, values(k0, keys), preferred_element_type=F32)
            if m_old is None:
                return m_new, pv
            alpha = jnp.exp2(m_old - m_new)
            return m_new, jnp.concatenate([alpha] * ((dv + LANES) // LANES), axis=1) * acc_old + pv

        for c in range(tq // td):
            r0 = c * td
            if c == 0:
                m, acc = step(diag_scores(c), d0, None, None)
            else:
                m, acc = step(diag_scores(c), d0 + r0, m_sc[r0:, :], acc_sc[r0:, :])
            m_sc[r0:, :] = m
            acc_sc[r0:, :] = acc

        def body(j, carry):
            k0 = pl.multiple_of(j * tq, tq)
            ss = [scores(0, k0 + u * tk, tk) for u in range(unroll)]
            m, acc = m_sc[...], acc_sc[...]
            for u in range(unroll):
                m, acc = step(ss[u], k0 + u * tk, m, acc)
            m_sc[...] = m
            acc_sc[...] = acc
            return carry

        lax.fori_loop(0, i, body, 0)
        finish()


def attention_prompt(bounded, q, k, v, tq, tk, td):
    G, H, S, _ = q.shape
    dv = v.shape[-1]
    assert dv == LANES and tq % tk == 0 and tq % td == 0 and td % LANES == 0 and td <= tk
    return pl.pallas_call(
        functools.partial(_attn_kernel, tq=tq, tk=tk, td=td),
        grid=(G, H, S // tq),
        in_specs=[pl.BlockSpec(memory_space=pltpu.SMEM),
                  pl.BlockSpec((None, None, tq, HEAD_PAD), lambda g, h, i: (g, h, i, 0)),
                  pl.BlockSpec((None, None, S, HEAD_PAD), lambda g, h, i: (g, h, 0, 0)),
                  pl.BlockSpec((None, None, S, dv), lambda g, h, i: (g, h, 0, 0))],
        out_specs=pl.BlockSpec((None, tq, dv), lambda g, h, i: (g, i, h)),
        out_shape=jax.ShapeDtypeStruct((G, S, H * dv), BF16),
        scratch_shapes=[pltpu.VMEM((tq, LANES), F32), pltpu.VMEM((tq, dv + LANES), F32)],
        compiler_params=_params("arbitrary", "arbitrary", "arbitrary"),
        name="attention_prompt",
    )(bounded, q, k, v)


def _attn_cached_kernel(q_ref, cache_ref, ckr_ref, cnew_ref, krnew_ref, wuk_ref, wuv_ref, gkn_ref,
                        o_ref, call_sc, krall_sc, p_sc, *, past, n_new):
    H = q_ref.shape[0]
    L = past + n_new
    Lp = call_sc.shape[0]
    lat = call_sc.shape[1]
    call_sc[0:past, :] = cache_ref[...].astype(BF16)
    call_sc[past:L, :] = cnew_ref[...].astype(BF16)
    call_sc[L:Lp, :] = jnp.zeros((Lp - L, lat), BF16)
    n_rope = ckr_ref.shape[-1]
    krall_sc[0:past, n_rope:] = jnp.zeros((past, LANES - n_rope), BF16)
    krall_sc[0:past, :n_rope] = ckr_ref[...].astype(BF16)
    krall_sc[past:L, :] = krnew_ref[...]
    krall_sc[L:Lp, :] = jnp.zeros((Lp - L, LANES), BF16)
    call = call_sc[...]
    krall = krall_sc[...]
    gkn = gkn_ref[...]
    valid = lax.broadcasted_iota(jnp.int32, (n_new, Lp), 1) < L
    nt = (((1,), (1,)), ((), ()))
    group = 4
    for grp in range(H // group):
        cols = slice(grp * group * LANES, (grp + 1) * group * LANES)
        kn_g = jnp.dot(call, wuk_ref[:, cols], preferred_element_type=F32)
        for sub in range(group):
            hh = group * grp + sub
            kn = _rms(kn_g[:, sub * LANES:(sub + 1) * LANES], gkn).astype(BF16)
            kh = jnp.concatenate([kn, krall], axis=1)
            s = lax.dot_general(q_ref[hh], kh, nt, preferred_element_type=F32)
            s = jnp.where(valid, s, -jnp.inf)
            m = jnp.max(s, axis=-1, keepdims=True)
            p = jnp.exp2(s - m)
            p = p / jnp.sum(p, axis=-1, keepdims=True)
            p_sc[hh * n_new:(hh + 1) * n_new, :] = p.astype(BF16)
    o_lat = jnp.dot(p_sc[...], call, preferred_element_type=F32).astype(BF16)
    for hh in range(H):
        o_ref[:, hh * LANES:(hh + 1) * LANES] = jnp.dot(
            o_lat[hh * n_new:(hh + 1) * n_new, :], wuv_ref[:, hh * LANES:(hh + 1) * LANES],
            preferred_element_type=F32).astype(o_ref.dtype)


def attention_cached(q, cache_lat, cache_kr, layer, c_new, kr_new_pad, w_uk, w_uv, g_k_nope, n_new):
    _, H, BT, _ = q.shape
    _, B, past, lat = cache_lat.shape
    n_rope = cache_kr.shape[-1]
    L = past + n_new
    Lp = -(-L // LANES) * LANES
    const = lambda b: (0, 0)
    return pl.pallas_call(
        functools.partial(_attn_cached_kernel, past=past, n_new=n_new),
        grid=(B,),
        in_specs=[pl.BlockSpec((None, H, n_new, HEAD_PAD), lambda b: (0, 0, b, 0)),
                  pl.BlockSpec((None, None, past, lat), lambda b: (layer, b, 0, 0)),
                  pl.BlockSpec((None, None, past, n_rope), lambda b: (layer, b, 0, 0)),
                  pl.BlockSpec((None, n_new, lat), lambda b: (0, b, 0)),
                  pl.BlockSpec((None, n_new, LANES), lambda b: (0, b, 0)),
                  pl.BlockSpec(w_uk.shape, const), pl.BlockSpec(w_uv.shape, const),
                  pl.BlockSpec((1, LANES), const)],
        out_specs=pl.BlockSpec((None, n_new, H * LANES), lambda b: (0, b, 0)),
        out_shape=jax.ShapeDtypeStruct((1, BT, H * LANES), BF16),
        scratch_shapes=[pltpu.VMEM((Lp, lat), BF16), pltpu.VMEM((Lp, LANES), BF16),
                        pltpu.VMEM((H * n_new, Lp), BF16)],
        compiler_params=_params("arbitrary"),
        name="attention_cached",
    )(q, cache_lat, cache_kr, c_new, kr_new_pad, w_uk, w_uv, g_k_nope.reshape(1, LANES))


def _merge_kernel(a_ref, b_ref, ga_ref, gb_ref, wpa_ref, wpb_ref, o_ref):
    pa = jnp.dot(a_ref[...], wpa_ref[...], preferred_element_type=F32)
    pb = jnp.dot(b_ref[...], wpb_ref[...], preferred_element_type=F32)
    o_ref[...] = (ga_ref[...].astype(F32) * pa + gb_ref[...].astype(F32) * pb).astype(o_ref.dtype)


def merge_branches(o_sg, o_mla, gates, w_pa, w_pb, ts, tn):
    G, S, W = o_sg.shape
    D = w_pa.shape[1]
    nj = D // tn
    row = lambda g, s, j: (g, s, 0)
    return pl.pallas_call(
        _merge_kernel,
        grid=(G, S // ts, nj),
        in_specs=[pl.BlockSpec((None, ts, W), row),
                  pl.BlockSpec((None, ts, o_mla.shape[-1]), row),
                  pl.BlockSpec((None, ts, tn), lambda g, s, j: (g, s, j)),
                  pl.BlockSpec((None, ts, tn), lambda g, s, j: (g, s, j + nj)),
                  pl.BlockSpec((W, tn), lambda g, s, j: (0, j)),
                  pl.BlockSpec((o_mla.shape[-1], tn), lambda g, s, j: (0, j))],
        out_specs=pl.BlockSpec((None, ts, tn), lambda g, s, j: (g, s, j)),
        out_shape=jax.ShapeDtypeStruct((G, S, D), BF16),
        compiler_params=_params("arbitrary", "arbitrary", "arbitrary"),
        name="merge_branches",
    )(o_sg, o_mla, gates, gates, w_pa, w_pb)


def _out_proj_kernel(m_ref, wo_ref, x_ref, g1_ref, gn_ref, sc_ref, sh_ref, x1_ref, h2_ref):
    y = jnp.dot(m_ref[...], wo_ref[...], preferred_element_type=F32)
    x1 = x_ref[...] + g1_ref[...] * y
    x1_ref[...] = x1
    h2_ref[...] = (_rms(x1, gn_ref[...]) * (1.0 + sc_ref[...]) + sh_ref[...]).astype(h2_ref.dtype)


def out_project(m, w_o, x, g1, g_norm2, sc2, sh2, ts):
    G, S, D = x.shape
    row = pl.BlockSpec((None, ts, D), lambda g, s: (g, s, 0))
    return pl.pallas_call(
        _out_proj_kernel,
        grid=(G, S // ts),
        in_specs=[row, pl.BlockSpec(w_o.shape, lambda g, s: (0, 0)), row,
                  _mod_spec(g1, ts), pl.BlockSpec((1, D), lambda g, s: (0, 0)),
                  _mod_spec(sc2, ts), _mod_spec(sh2, ts)],
        out_specs=[row, row],
        out_shape=[jax.ShapeDtypeStruct((G, S, D), F32), jax.ShapeDtypeStruct((G, S, D), BF16)],
        compiler_params=_params("arbitrary", "arbitrary"),
        name="out_project",
    )(m, w_o, x, g1, g_norm2.reshape(1, D), sc2, sh2)


def _mlp_kernel(h_ref, wup_ref, wdn_ref, x1_ref, g2_ref, o_ref):
    j = pl.program_id(2)
    hid = jnp.dot(h_ref[...], wup_ref[...], preferred_element_type=F32)
    hid = jnp.square(jnp.maximum(hid, 0.0)).astype(BF16)
    part = jnp.dot(hid, wdn_ref[...], preferred_element_type=F32)

    @pl.when(j == 0)
    def _():
        o_ref[...] = part

    @pl.when(j > 0)
    def _():
        o_ref[...] += part

    @pl.when(j == pl.num_programs(2) - 1)
    def _():
        o_ref[...] = x1_ref[...] + g2_ref[...] * o_ref[...]


def mlp_residual(h2, w_up, w_down, x1, g2, ts, th):
    G, S, D = x1.shape
    hidden = w_up.shape[1]
    row = pl.BlockSpec((None, ts, D), lambda g, s, j: (g, s, 0))
    return pl.pallas_call(
        _mlp_kernel,
        grid=(G, S // ts, hidden // th),
        in_specs=[row,
                  pl.BlockSpec((D, th), lambda g, s, j: (0, j)),
                  pl.BlockSpec((th, D), lambda g, s, j: (j, 0)),
                  row, _mod_spec(g2, ts)],
        out_specs=row,
        out_shape=jax.ShapeDtypeStruct((G, S, D), F32),
        compiler_params=_params("arbitrary", "arbitrary", "arbitrary"),
        name="mlp_residual",
    )(h2, w_up, w_down, x1, g2)


def _rope_tables(pos, n_rope):
    inv = jnp.float32(ROPE_BASE) ** (-jnp.arange(0, n_rope, 2, dtype=F32) / n_rope)
    ang = pos.astype(F32)[:, None] * inv[None, :]
    cos, sin = lax.optimization_barrier((jnp.cos(ang), jnp.sin(ang)))
    pad = jnp.zeros((pos.shape[0], LANES - n_rope), F32)
    return (jnp.concatenate([cos, cos, pad], axis=1), jnp.concatenate([-sin, sin, pad], axis=1))


def _dup_lanes(g):
    return jnp.concatenate([g.astype(F32), g.astype(F32)]).reshape(1, 2 * g.shape[0])


def _layer(x, mods, pos, P, ts, *, cache=None):
    sh1, sc1, g1, sh2, sc2, g2 = mods
    n_rope = P["n_rope"]
    tabs = _rope_tables(pos, n_rope)
    h, q = prenorm_q_project(x, P["g_norm1"], sc1, sh1, P["w_q"], P["g_q_a"], P["w_uq"], P["g_q_nope"],
                             P["g_q_rope"], tabs, ts, P["q_scale"])
    gates = proj_act(h, P["w_g"], "sigmoid", ts, P["w_g"].shape[1] // 2)
    if cache is None:
        v = proj_gelu_norm(h, P["w_v"], P["g_sg"], ts, BF16)
        o_sg = gelu_spatial_gate(h, P["w_u"], v, P["w_s"], P["b_rows"], ts)
        c_kv, k_rope, k, vv = kv_project(h, P["w_kv"], P["g_kv_a"], P["g_k_rope"], tabs, ts, n_rope,
                                         expand_weights=(P["w_uk"], P["w_uv"], P["g_k_nope"], P["k_shift"]))
        o_mla = attention_prompt(P["bounded"], q, k, vv, tq=min(2048, x.shape[1]),
                                 tk=min(1024, x.shape[1]), td=min(512, x.shape[1]))
        extra = ()
    else:
        cache_lat, cache_kr, layer, B, T = cache
        u = proj_act(h, P["w_u"], "gelu", ts, P["w_u"].shape[1])
        v = proj_gelu_norm(h, P["w_v"], P["g_sg"], ts, F32)
        W = v.shape[-1]
        o_sg = spatial_gate_open(u.reshape(B, T, W), v.reshape(B, T, W), P["w_lanes"][:T, :T],
                                 P["b_rows"][:T]).reshape(1, B * T, W)
        c_kv, k_rope, kr_pad = kv_project(h, P["w_kv"], P["g_kv_a"], P["g_k_rope"], tabs, ts, n_rope)
        o_mla = attention_cached(q, cache_lat, cache_kr, layer, c_kv, kr_pad, P["w_uk"], P["w_uv"],
                                 P["g_k_nope"], T)
        extra = (v,)
    m = merge_branches(o_sg, o_mla, gates, P["w_pa"], P["w_pb"], ts, P["w_pa"].shape[1] // 2)
    ts_res = min(ts, 512)
    x1, h2 = out_project(m, P["w_o"], x, g1, P["g_norm2"], sc2, sh2, ts_res)
    y = mlp_residual(h2, P["w_up"], P["w_down"], x1, g2, ts_res, min(2048, P["w_up"].shape[1]))
    return (y, c_kv, k_rope) + extra


def kernel(x_prompt, x_sample, cache_kv_latent, cache_k_rope, c_prompt, c_sample, w_ada, b_ada, g_norm1, g_norm2, w_in, g_sg, w_s, b_s, g_q_a, w_uq, g_q_nope, g_q_rope, g_kv_a, g_k_rope, w_uk, g_k_nope, w_uv, w_pa, w_pb, w_o, w_up, w_down):
    depth = w_in.shape[0]
    Bp, S, D = x_prompt.shape
    Bs, T, _ = x_sample.shape
    past = cache_kv_latent.shape[2]
    q_lora = g_q_a.shape[-1]
    lat = g_kv_a.shape[-1]
    n_nope = g_q_nope.shape[-1]
    n_rope = g_q_rope.shape[-1]
    H = w_uk.shape[2]
    sg_w = g_sg.shape[-1]
    off_q = 2 * sg_w
    off_kv = off_q + q_lora
    off_gate = off_kv + lat + n_rope
    assert n_nope == LANES and w_uv.shape[-1] == LANES and 2 * n_rope == LANES
    assert sg_w // SG_GROUPS == LANES and H == N_HEADS

    nb = Bp + Bs
    nb_pad = -(-nb // 8) * 8
    c_all = jnp.concatenate([c_prompt, c_sample, jnp.zeros((nb_pad - nb, D), F32)], axis=0)

    y_p, y_s = x_prompt, x_sample.reshape(1, Bs * T, D)
    outs = [[] for _ in range(5)]
    for l in range(depth):
        def w_in_cols(a, b):
            return w_in[l][:, a:b].astype(BF16)
        w_uq_l = w_uq[l].astype(BF16).reshape(q_lora, H, n_nope + n_rope)
        w_uq_pad = jnp.concatenate([w_uq_l, w_uq_l[:, :, n_nope:]], axis=2)
        q_scale = float((n_nope + n_rope) ** -0.5 * LOG2E)
        def sq_norm_bound(g_nope, g_rope):
            return n_nope * jnp.max(jnp.square(g_nope)) + n_rope * jnp.max(jnp.square(g_rope))
        score_bound = q_scale * jnp.sqrt(sq_norm_bound(g_q_nope[l], g_q_rope[l])
                                         * sq_norm_bound(g_k_nope[l], g_k_rope[l]))
        bounded = (score_bound <= BOUND_MAX).astype(jnp.int32).reshape(1)
        k_shift = jnp.where(jnp.arange(LANES) == LANES // 2, -score_bound, 0.0).astype(F32).reshape(1, LANES)
        P = {
            "n_rope": n_rope,
            "q_scale": q_scale, "k_shift": k_shift, "bounded": bounded,
            "g_norm1": g_norm1[l], "g_norm2": g_norm2[l], "g_sg": g_sg[l],
            "w_u": w_in_cols(0, sg_w), "w_v": w_in_cols(sg_w, off_q), "w_q": w_in_cols(off_q, off_kv),
            "w_kv": jnp.concatenate([w_in_cols(off_kv, off_gate), w_in_cols(off_kv + lat, off_gate)], axis=1),
            "w_g": w_in_cols(off_gate, w_in.shape[2]),
            "g_q_a": g_q_a[l], "w_uq": w_uq_pad.reshape(q_lora, H * HEAD_PAD),
            "g_q_nope": g_q_nope[l], "g_q_rope": _dup_lanes(g_q_rope[l]),
            "g_kv_a": g_kv_a[l], "g_k_rope": _dup_lanes(g_k_rope[l]),
            "w_uk": w_uk[l].astype(BF16).reshape(lat, H * n_nope),
            "w_uv": w_uv[l].astype(BF16).reshape(lat, H * LANES),
            "g_k_nope": g_k_nope[l],
            "w_s": w_s[l],
            "b_rows": jnp.repeat(b_s[l].T, LANES, axis=1),
            "w_lanes": jnp.repeat(w_s[l][:, :T, :T].transpose(1, 2, 0), LANES, axis=2),
            "w_pa": w_pa[l].astype(BF16), "w_pb": w_pb[l].astype(BF16), "w_o": w_o[l].astype(BF16),
            "w_up": w_up[l].astype(BF16), "w_down": w_down[l].astype(BF16),
        }
        mod = ada_project(c_all, w_ada[l], b_ada[l])
        mods_p = [a.reshape(Bp, 1, D) for a in jnp.split(mod[:Bp], 6, axis=-1)]
        mods_s = [jnp.repeat(a, T, axis=0).reshape(1, Bs * T, D)
                  for a in jnp.split(mod[Bp:nb], 6, axis=-1)]

        ts_p = min(1024, S)
        y_p, lp, kp = _layer(y_p, mods_p, jnp.arange(S), P, ts_p)
        pos_s = jnp.tile(past + jnp.arange(T), Bs)
        y_s, ls, ks, vs = _layer(y_s, mods_s, pos_s, P, Bs * T,
                                 cache=(cache_kv_latent, cache_k_rope, l, Bs, T))
        for lst, a in zip(outs, (lp, kp, ls.reshape(Bs, T, lat), ks.reshape(Bs, T, n_rope),
                                 vs.reshape(Bs, T, sg_w))):
            lst.append(a)
    return (y_p, y_s.reshape(Bs, T, D)) + tuple(o[0][None] if depth == 1 else jnp.stack(o) for o in outs)
```

```python
import functools
import math

import jax
import jax.numpy as jnp
import numpy as np
from jax import lax
from jax.experimental import pallas as pl
from jax.experimental.pallas import tpu as pltpu

F32 = jnp.float32
BF16 = jnp.bfloat16

EPS = 1e-6
ROPE_BASE = 10000.0
N_HEADS = 16
CHUNK = 64
SG_CHUNK = 128
SG_GROUPS = 16
LANES = 128
HEAD_PAD = 256
LOG2E = math.log2(math.e)
BOUND_MAX = 50.0

VMEM_LIMIT = 58 * 1024 * 1024


def _params(*sem):
    return pltpu.CompilerParams(dimension_semantics=sem, vmem_limit_bytes=VMEM_LIMIT)


def _rms(x, g):
    ms = jnp.mean(x * x, axis=-1, keepdims=True)
    return x * lax.rsqrt(ms + EPS) * g


def _rms_rope_group(t, g_dup):
    ms = jnp.mean(t * t, axis=-1, keepdims=True)
    return t * lax.rsqrt(ms + EPS) * g_dup


def _rope_group(t, tc, ts):
    return t * tc + pltpu.roll(t, LANES - LANES // 4, 1) * ts


def _gelu(z):
    return 0.5 * z * (1.0 + lax.erf(z * np.float32(math.sqrt(0.5))))


def _mod_spec(mod, ts):
    d = mod.shape[-1]
    if mod.shape[1] == 1:
        return pl.BlockSpec((None, 1, d), lambda g, s, *_: (g, 0, 0))
    return pl.BlockSpec((None, ts, d), lambda g, s, *_: (g, s, 0))


def _ada_kernel(c_ref, w_ref, b_ref, o_ref):
    c = c_ref[...]
    s = (c * jax.nn.sigmoid(c)).astype(BF16)
    o_ref[...] = jnp.dot(s, w_ref[...].astype(BF16), preferred_element_type=F32) + b_ref[...]


def ada_project(c, w_ada, b_ada, tn=1024):
    r, d = c.shape
    n = w_ada.shape[1]
    return pl.pallas_call(
        _ada_kernel,
        grid=(n // tn,),
        in_specs=[pl.BlockSpec((r, d), lambda j: (0, 0)),
                  pl.BlockSpec((d, tn), lambda j: (0, j)),
                  pl.BlockSpec((1, tn), lambda j: (0, j))],
        out_specs=pl.BlockSpec((r, tn), lambda j: (0, j)),
        out_shape=jax.ShapeDtypeStruct((r, n), F32),
        compiler_params=_params("arbitrary"),
        name="ada_project",
    )(c, w_ada, b_ada.reshape(1, n))


def _proj_act_kernel(h_ref, w_ref, o_ref, *, act):
    z = jnp.dot(h_ref[...], w_ref[...], preferred_element_type=F32)
    if act == "gelu":
        a = _gelu(z)
    else:
        a = jax.nn.sigmoid(z)
    o_ref[...] = a.astype(o_ref.dtype)


def proj_act(h, w, act, ts, tn):
    G, S, D = h.shape
    n = w.shape[1]
    return pl.pallas_call(
        functools.partial(_proj_act_kernel, act=act),
        grid=(G, S // ts, n // tn),
        in_specs=[pl.BlockSpec((None, ts, D), lambda g, s, j: (g, s, 0)),
                  pl.BlockSpec((D, tn), lambda g, s, j: (0, j))],
        out_specs=pl.BlockSpec((None, ts, tn), lambda g, s, j: (g, s, j)),
        out_shape=jax.ShapeDtypeStruct((G, S, n), BF16),
        compiler_params=_params("arbitrary", "arbitrary", "arbitrary"),
        name="proj_" + act,
    )(h, w)


def _proj_gelu_norm_kernel(h_ref, w_ref, g_ref, o_ref):
    z = jnp.dot(h_ref[...], w_ref[...], preferred_element_type=F32)
    o_ref[...] = _rms(_gelu(z), g_ref[...]).astype(o_ref.dtype)


def proj_gelu_norm(h, w, g, ts, out_dtype):
    G, S, D = h.shape
    n = w.shape[1]
    return pl.pallas_call(
        _proj_gelu_norm_kernel,
        grid=(G, S // ts),
        in_specs=[pl.BlockSpec((None, ts, D), lambda g_, s: (g_, s, 0)),
                  pl.BlockSpec((D, n), lambda g_, s: (0, 0)),
                  pl.BlockSpec((1, n), lambda g_, s: (0, 0))],
        out_specs=pl.BlockSpec((None, ts, n), lambda g_, s: (g_, s, 0)),
        out_shape=jax.ShapeDtypeStruct((G, S, n), out_dtype),
        compiler_params=_params("arbitrary", "arbitrary"),
        name="proj_gelu_norm",
    )(h, w, g.reshape(1, n))


def _q_proj_kernel(x_ref, g1_ref, sc_ref, sh_ref, wq_ref, gqa_ref, wuq_ref, gn_ref, gr_ref, tc_ref, ts_ref,
                   h_ref, q_ref, *, q_scale):
    h = (_rms(x_ref[...], g1_ref[...]) * (1.0 + sc_ref[...]) + sh_ref[...]).astype(h_ref.dtype)
    h_ref[...] = h
    zq = jnp.dot(h, wq_ref[...], preferred_element_type=F32)
    zn = _rms(zq, gqa_ref[...]).astype(BF16)
    tc, ts = tc_ref[...], ts_ref[...]
    gains = jnp.concatenate([gn_ref[...], gr_ref[...]], axis=1) * q_scale
    same_tile = (lax.broadcasted_iota(jnp.int32, (HEAD_PAD, HEAD_PAD), 0) // LANES
                 == lax.broadcasted_iota(jnp.int32, (HEAD_PAD, HEAD_PAD), 1) // LANES)
    tile_mean = jnp.where(same_tile, 1.0 / LANES, 0.0).astype(BF16)
    shift_lane = jnp.where(lax.broadcasted_iota(jnp.int32, (1, LANES), 1) == LANES // 2, 1.0, 0.0)
    for pair in range(q_ref.shape[0] // 2):
        blk2 = jnp.dot(zn, wuq_ref[:, pair * 2 * HEAD_PAD:(pair + 1) * 2 * HEAD_PAD],
                       preferred_element_type=F32)
        for sub in range(2):
            hh = 2 * pair + sub
            blk = blk2[:, sub * HEAD_PAD:(sub + 1) * HEAD_PAD]
            ms = jnp.dot((blk * blk).astype(BF16), tile_mean, preferred_element_type=F32)
            y = blk * lax.rsqrt(ms + EPS) * gains
            q_ref[hh, :, :LANES] = y[:, :LANES].astype(q_ref.dtype)
            q_ref[hh, :, LANES:] = (_rope_group(y[:, LANES:], tc, ts) + shift_lane).astype(q_ref.dtype)


def prenorm_q_project(x, g_norm, sc, sh, w_q, g_q_a, w_uq_pad, g_nope, g_rope_pad, tabs, ts, q_scale):
    G, S, D = x.shape
    ql = w_q.shape[1]
    H = w_uq_pad.shape[1] // HEAD_PAD
    const = lambda g, s: (0, 0)
    tab_spec = pl.BlockSpec((ts, LANES), lambda g, s: (s, 0))
    row = pl.BlockSpec((None, ts, D), lambda g, s: (g, s, 0))
    return pl.pallas_call(
        functools.partial(_q_proj_kernel, q_scale=q_scale),
        grid=(G, S // ts),
        in_specs=[row, pl.BlockSpec((1, D), const), _mod_spec(sc, ts), _mod_spec(sh, ts),
                  pl.BlockSpec((D, ql), const),
                  pl.BlockSpec((1, ql), const),
                  pl.BlockSpec((ql, H * HEAD_PAD), const),
                  pl.BlockSpec((1, LANES), const),
                  pl.BlockSpec((1, LANES), const),
                  tab_spec, tab_spec],
        out_specs=[row, pl.BlockSpec((None, H, ts, HEAD_PAD), lambda g, s: (g, 0, s, 0))],
        out_shape=[jax.ShapeDtypeStruct((G, S, D), BF16),
                   jax.ShapeDtypeStruct((G, H, S, HEAD_PAD), BF16)],
        compiler_params=_params("arbitrary", "arbitrary"),
        name="prenorm_q_project",
    )(x, g_norm.reshape(1, D), sc, sh, w_q, g_q_a.reshape(1, ql), w_uq_pad, g_nope.reshape(1, LANES),
      g_rope_pad, *tabs)


def _kv_proj_kernel(h_ref, wkv_ref, gkva_ref, gkr_ref, tc_ref, ts_ref, *rest, n_rope, expand):
    if expand:
        wuk_ref, wuv_ref, gkn_ref, kshift_ref, c_ref, kr_ref, k_ref, v_ref = rest
    else:
        c_ref, kr_ref, krp_ref = rest
    lat = c_ref.shape[-1]
    z = jnp.dot(h_ref[...], wkv_ref[...], preferred_element_type=F32)
    c = _rms(z[:, :lat], gkva_ref[...])
    c_ref[...] = c
    rope = _rope_group(_rms_rope_group(z[:, lat:], gkr_ref[...]), tc_ref[...], ts_ref[...])
    kr_ref[...] = rope[:, :n_rope]
    rope_b = rope.astype(BF16)
    if not expand:
        krp_ref[...] = rope_b
        return
    cb = c.astype(BF16)
    gkn = gkn_ref[...]
    rope_b = (rope + kshift_ref[...]).astype(BF16)
    for pair in range(k_ref.shape[0] // 2):
        cols = slice(pair * 2 * LANES, (pair + 1) * 2 * LANES)
        kn2 = jnp.dot(cb, wuk_ref[:, cols], preferred_element_type=F32)
        v2 = jnp.dot(cb, wuv_ref[:, cols], preferred_element_type=F32)
        for sub in range(2):
            hh = 2 * pair + sub
            lanes = slice(sub * LANES, (sub + 1) * LANES)
            k_ref[hh, :, :LANES] = _rms(kn2[:, lanes], gkn).astype(k_ref.dtype)
            k_ref[hh, :, LANES:] = rope_b
            v_ref[hh] = v2[:, lanes].astype(v_ref.dtype)


def kv_project(h, w_kv, g_kv_a, g_k_rope_pad, tabs, ts, n_rope, expand_weights=None):
    G, S, D = h.shape
    lat = g_kv_a.shape[-1]
    const = lambda g, s: (0, 0)
    tab_spec = pl.BlockSpec((ts, LANES), lambda g, s: (s, 0))
    in_specs = [pl.BlockSpec((None, ts, D), lambda g, s: (g, s, 0)),
                pl.BlockSpec((D, lat + LANES), const),
                pl.BlockSpec((1, lat), const),
                pl.BlockSpec((1, LANES), const),
                tab_spec, tab_spec]
    args = [h, w_kv, g_kv_a.reshape(1, lat), g_k_rope_pad, *tabs]
    out_specs = [pl.BlockSpec((None, ts, lat), lambda g, s: (g, s, 0)),
                 pl.BlockSpec((None, ts, n_rope), lambda g, s: (g, s, 0))]
    out_shape = [jax.ShapeDtypeStruct((G, S, lat), F32),
                 jax.ShapeDtypeStruct((G, S, n_rope), F32)]
    expand = expand_weights is not None
    if expand:
        w_uk, w_uv, g_k_nope, k_shift = expand_weights
        H = w_uk.shape[1] // LANES
        in_specs += [pl.BlockSpec(w_uk.shape, const), pl.BlockSpec(w_uv.shape, const),
                     pl.BlockSpec((1, LANES), const), pl.BlockSpec((1, LANES), const)]
        args += [w_uk, w_uv, g_k_nope.reshape(1, LANES), k_shift]
        out_specs += [pl.BlockSpec((None, H, ts, HEAD_PAD), lambda g, s: (g, 0, s, 0)),
                      pl.BlockSpec((None, H, ts, LANES), lambda g, s: (g, 0, s, 0))]
        out_shape += [jax.ShapeDtypeStruct((G, H, S, HEAD_PAD), BF16),
                      jax.ShapeDtypeStruct((G, H, S, LANES), BF16)]
    else:
        out_specs.append(pl.BlockSpec((None, ts, LANES), lambda g, s: (g, s, 0)))
        out_shape.append(jax.ShapeDtypeStruct((G, S, LANES), BF16))
    return pl.pallas_call(
        functools.partial(_kv_proj_kernel, n_rope=n_rope, expand=expand),
        grid=(G, S // ts),
        in_specs=in_specs,
        out_specs=out_specs,
        out_shape=out_shape,
        compiler_params=_params("arbitrary", "arbitrary"),
        name="kv_project",
    )(*args)


def _gelu_gate_kernel(h_ref, wu_ref, v_ref, w_ref, b_ref, o_ref, wm_sc, u_sc, *, n_chunks):
    u_sc[...] = _gelu(jnp.dot(h_ref[...], wu_ref[...], preferred_element_type=F32)).astype(u_sc.dtype)
    row = lax.broadcasted_iota(jnp.int32, (SG_CHUNK, SG_CHUNK), 0)
    col = lax.broadcasted_iota(jnp.int32, (SG_CHUNK, SG_CHUNK), 1)
    tril = col <= row
    n_groups = w_ref.shape[0]
    for g in range(n_groups):
        wm_sc[g] = jnp.where(tril, w_ref[g], 0.0).astype(BF16)

    def chunk_body(n, carry):
        rows = pl.ds(pl.multiple_of(n * SG_CHUNK, SG_CHUNK), SG_CHUNK)
        for g in range(n_groups):
            lanes = slice(g * LANES, (g + 1) * LANES)
            mix = jnp.dot(wm_sc[g], v_ref[rows, lanes], preferred_element_type=F32) + b_ref[:, lanes]
            o_ref[rows, lanes] = (u_sc[rows, lanes].astype(F32) * mix).astype(o_ref.dtype)
        return carry

    lax.fori_loop(0, n_chunks, chunk_body, 0)


def gelu_spatial_gate(h, w_u, v, w_s, bias_rows, ts):
    G, S, D = h.shape
    W = w_u.shape[1]
    blk = pl.BlockSpec((None, ts, W), lambda g, s: (g, s, 0))
    return pl.pallas_call(
        functools.partial(_gelu_gate_kernel, n_chunks=ts // SG_CHUNK),
        grid=(G, S // ts),
        in_specs=[pl.BlockSpec((None, ts, D), lambda g, s: (g, s, 0)),
                  pl.BlockSpec(w_u.shape, lambda g, s: (0, 0)),
                  blk,
                  pl.BlockSpec(w_s.shape, lambda g, s: (0, 0, 0)),
                  pl.BlockSpec(bias_rows.shape, lambda g, s: (0, 0))],
        out_specs=blk,
        out_shape=jax.ShapeDtypeStruct((G, S, W), BF16),
        scratch_shapes=[pltpu.VMEM(w_s.shape, BF16), pltpu.VMEM((ts, W), BF16)],
        compiler_params=_params("arbitrary", "arbitrary"),
        name="gelu_spatial_gate",
    )(h, w_u, v, w_s, bias_rows)


def _spatial_gate_open_kernel(u_ref, v_ref, wl_ref, b_ref, o_ref):
    T = u_ref.shape[1]
    for i in range(T):
        acc = b_ref[i:i + 1, :] + wl_ref[i, 0:1, :] * v_ref[:, 0, :]
        for j in range(1, i + 1):
            acc = acc + wl_ref[i, j:j + 1, :] * v_ref[:, j, :]
        o_ref[:, i, :] = (u_ref[:, i, :].astype(F32) * acc).astype(o_ref.dtype)


def spatial_gate_open(u, v, w_lanes, bias_rows):
    B, T, W = u.shape
    full = lambda a: pl.BlockSpec(a.shape, lambda i: (0,) * a.ndim)
    return pl.pallas_call(
        _spatial_gate_open_kernel,
        grid=(1,),
        in_specs=[full(u), full(v), full(w_lanes), full(bias_rows)],
        out_specs=pl.BlockSpec((B, T, W), lambda i: (0, 0, 0)),
        out_shape=jax.ShapeDtypeStruct((B, T, W), BF16),
        compiler_params=_params("arbitrary"),
        name="spatial_gate_open",
    )(u, v, w_lanes, bias_rows)


def _attn_kernel(bounded_ref, q_ref, k_ref, v_ref, o_ref, m_sc, acc_sc, *, tq, tk, td):
    i = pl.program_id(2)
    dv = v_ref.shape[-1]
    unroll = tq // tk
    nt = (((1,), (1,)), ((), ()))
    ones = jnp.ones((tk, LANES), BF16)
    d0 = pl.multiple_of(i * tq, tq)
    chunk_mask = (lax.broadcasted_iota(jnp.int32, (td, td), 1) // CHUNK
                  <= lax.broadcasted_iota(jnp.int32, (td, td), 0) // CHUNK)

    def scores(rows0, k0, keys):
        return lax.dot_general(q_ref[rows0:, :], k_ref[pl.ds(k0, keys), :], nt, preferred_element_type=F32)

    def diag_scores(c):
        r0 = c * td
        s = scores(r0, d0 + r0, td)
        top = jnp.where(chunk_mask, s[:td], -jnp.inf)
        return top if r0 + td == tq else jnp.concatenate([top, s[td:]], axis=0)

    def values(k0, keys):
        return jnp.concatenate([v_ref[pl.ds(k0, keys), :], ones[:keys]], axis=1)

    def finish():
        acc = acc_sc[...]
        o_ref[...] = (acc[:, :dv] / acc[:, dv:]).astype(o_ref.dtype)

    @pl.when(bounded_ref[0] == 1)
    def _():
        for c in range(tq // td):
            r0 = c * td
            pv = jnp.dot(jnp.exp2(diag_scores(c)).astype(BF16), values(d0 + r0, td),
                         preferred_element_type=F32)
            if c == 0:
                acc_sc[...] = pv
            else:
                acc_sc[r0:, :] += pv

        def body(j, carry):
            k0 = pl.multiple_of(j * tq, tq)
            acc = acc_sc[...]
            for u in range(unroll):
                p = jnp.exp2(scores(0, k0 + u * tk, tk)).astype(BF16)
                acc = acc + jnp.dot(p, values(k0 + u * tk, tk), preferred_element_type=F32)
            acc_sc[...] = acc
            return carry

        lax.fori_loop(0, i, body, 0)
        finish()

    @pl.when(bounded_ref[0] != 1)
    def _():
        def step(s, k0, m_old, acc_old):
            rows, keys = s.shape
            row_max = jnp.max(s, axis=-1, keepdims=True)
            m_new = (jnp.broadcast_to(row_max, (rows, LANES)) if m_old is None
                     else jnp.maximum(m_old, row_max))
            p = jnp.concatenate([jnp.exp2(s[:, c * LANES:(c + 1) * LANES] - m_new)
                                 for c in range(keys // LANES)], axis=1).astype(BF16)
            pv = jnp.dot(p, values(k0, keys), preferred_element_type=F32)
            if m_old is None:
                return m_new, pv
            alpha = jnp.exp2(m_old - m_new)
            return m_new, jnp.concatenate([alpha] * ((dv + LANES) // LANES), axis=1) * acc_old + pv

        for c in range(tq // td):
            r0 = c * td
            if c == 0:
                m, acc = step(diag_scores(c), d0, None, None)
            else:
                m, acc = step(diag_scores(c), d0 + r0, m_sc[r0:, :], acc_sc[r0:, :])
            m_sc[r0:, :] = m
            acc_sc[r0:, :] = acc

        def body(j, carry):
            k0 = pl.multiple_of(j * tq, tq)
            ss = [scores(0, k0 + u * tk, tk) for u in range(unroll)]
            m, acc = m_sc[...], acc_sc[...]
            for u in range(unroll):
                m, acc = step(ss[u], k0 + u * tk, m, acc)
            m_sc[...] = m
            acc_sc[...] = acc
            return carry

        lax.fori_loop(0, i, body, 0)
        finish()


def attention_prompt(bounded, q, k, v, tq, tk, td):
    G, H, S, _ = q.shape
    dv = v.shape[-1]
    assert dv == LANES and tq % tk == 0 and tq % td == 0 and td % LANES == 0 and td <= tk
    return pl.pallas_call(
        functools.partial(_attn_kernel, tq=tq, tk=tk, td=td),
        grid=(G, H, S // tq),
        in_specs=[pl.BlockSpec(memory_space=pltpu.SMEM),
                  pl.BlockSpec((None, None, tq, HEAD_PAD), lambda g, h, i: (g, h, i, 0)),
                  pl.BlockSpec((None, None, S, HEAD_PAD), lambda g, h, i: (g, h, 0, 0)),
                  pl.BlockSpec((None, None, S, dv), lambda g, h, i: (g, h, 0, 0))],
        out_specs=pl.BlockSpec((None, tq, dv), lambda g, h, i: (g, i, h)),
        out_shape=jax.ShapeDtypeStruct((G, S, H * dv), BF16),
        scratch_shapes=[pltpu.VMEM((tq, LANES), F32), pltpu.VMEM((tq, dv + LANES), F32)],
        compiler_params=_params("arbitrary", "arbitrary", "arbitrary"),
        name="attention_prompt",
    )(bounded, q, k, v)


def _attn_cached_kernel(q_ref, cache_ref, ckr_ref, cnew_ref, krnew_ref, wukt_ref, wuv_ref, gkn_ref,
                        o_ref, call_sc, krall_sc, p_sc, *, past, n_new):
    H = q_ref.shape[0]
    L = past + n_new
    Lp = call_sc.shape[0]
    lat = call_sc.shape[1]
    call_sc[0:past, :] = cache_ref[...].astype(BF16)
    call_sc[past:L, :] = cnew_ref[...].astype(BF16)
    call_sc[L:Lp, :] = jnp.zeros((Lp - L, lat), BF16)
    n_rope = ckr_ref.shape[-1]
    krall_sc[0:past, n_rope:] = jnp.zeros((past, LANES - n_rope), BF16)
    krall_sc[0:past, :n_rope] = ckr_ref[...].astype(BF16)
    krall_sc[past:L, :] = krnew_ref[...]
    krall_sc[L:Lp, :] = jnp.zeros((Lp - L, LANES), BF16)
    call = call_sc[...]
    gkn = gkn_ref[...]
    valid = lax.broadcasted_iota(jnp.int32, (n_new, Lp), 1) < L
    nt = (((1,), (1,)), ((), ()))
    q_rope = jnp.concatenate([q_ref[hh][:, LANES:] for hh in range(H)], axis=0)
    s_rope = lax.dot_general(q_rope, krall_sc[...], nt, preferred_element_type=F32)
    group = 4
    for grp in range(H // group):
        kn_t = lax.dot_general(wukt_ref[grp * group * LANES:(grp + 1) * group * LANES, :], call, nt,
                               preferred_element_type=F32)
        for sub in range(group):
            hh = group * grp + sub
            blk = kn_t[sub * LANES:(sub + 1) * LANES, :]
            inv_rms = lax.rsqrt(jnp.mean(blk * blk, axis=0, keepdims=True) + EPS)
            q_nope = (q_ref[hh][:, :LANES].astype(F32) * gkn).astype(BF16)
            s = jnp.dot(q_nope, blk.astype(BF16), preferred_element_type=F32) * inv_rms
            s = s + s_rope[hh * n_new:(hh + 1) * n_new, :]
            s = jnp.where(valid, s, -jnp.inf)
            m = jnp.max(s, axis=-1, keepdims=True)
            p = jnp.exp2(s - m)
            p = p / jnp.sum(p, axis=-1, keepdims=True)
            p_sc[hh * n_new:(hh + 1) * n_new, :] = p.astype(BF16)
    o_lat = jnp.dot(p_sc[...], call, preferred_element_type=F32).astype(BF16)
    for hh in range(H):
        o_ref[:, hh * LANES:(hh + 1) * LANES] = jnp.dot(
            o_lat[hh * n_new:(hh + 1) * n_new, :], wuv_ref[:, hh * LANES:(hh + 1) * LANES],
            preferred_element_type=F32).astype(o_ref.dtype)


def attention_cached(q, cache_lat, cache_kr, layer, c_new, kr_new_pad, w_uk_t, w_uv, g_k_nope, n_new):
    _, H, BT, _ = q.shape
    _, B, past, lat = cache_lat.shape
    n_rope = cache_kr.shape[-1]
    L = past + n_new
    Lp = -(-L // LANES) * LANES
    const = lambda b: (0, 0)
    return pl.pallas_call(
        functools.partial(_attn_cached_kernel, past=past, n_new=n_new),
        grid=(B,),
        in_specs=[pl.BlockSpec((None, H, n_new, HEAD_PAD), lambda b: (0, 0, b, 0)),
                  pl.BlockSpec((None, None, past, lat), lambda b: (layer, b, 0, 0)),
                  pl.BlockSpec((None, None, past, n_rope), lambda b: (layer, b, 0, 0)),
                  pl.BlockSpec((None, n_new, lat), lambda b: (0, b, 0)),
                  pl.BlockSpec((None, n_new, LANES), lambda b: (0, b, 0)),
                  pl.BlockSpec(w_uk_t.shape, const), pl.BlockSpec(w_uv.shape, const),
                  pl.BlockSpec((1, LANES), const)],
        out_specs=pl.BlockSpec((None, n_new, H * LANES), lambda b: (0, b, 0)),
        out_shape=jax.ShapeDtypeStruct((1, BT, H * LANES), BF16),
        scratch_shapes=[pltpu.VMEM((Lp, lat), BF16), pltpu.VMEM((Lp, LANES), BF16),
                        pltpu.VMEM((H * n_new, Lp), BF16)],
        compiler_params=_params("arbitrary"),
        name="attention_cached",
    )(q, cache_lat, cache_kr, c_new, kr_new_pad, w_uk_t, w_uv, g_k_nope.reshape(1, LANES))


def _merge_kernel(a_ref, b_ref, ga_ref, gb_ref, wpa_ref, wpb_ref, o_ref):
    pa = jnp.dot(a_ref[...], wpa_ref[...], preferred_element_type=F32)
    pb = jnp.dot(b_ref[...], wpb_ref[...], preferred_element_type=F32)
    o_ref[...] = (ga_ref[...].astype(F32) * pa + gb_ref[...].astype(F32) * pb).astype(o_ref.dtype)


def merge_branches(o_sg, o_mla, gates, w_pa, w_pb, ts, tn):
    G, S, W = o_sg.shape
    D = w_pa.shape[1]
    nj = D // tn
    row = lambda g, s, j: (g, s, 0)
    return pl.pallas_call(
        _merge_kernel,
        grid=(G, S // ts, nj),
        in_specs=[pl.BlockSpec((None, ts, W), row),
                  pl.BlockSpec((None, ts, o_mla.shape[-1]), row),
                  pl.BlockSpec((None, ts, tn), lambda g, s, j: (g, s, j)),
                  pl.BlockSpec((None, ts, tn), lambda g, s, j: (g, s, j + nj)),
                  pl.BlockSpec((W, tn), lambda g, s, j: (0, j)),
                  pl.BlockSpec((o_mla.shape[-1], tn), lambda g, s, j: (0, j))],
        out_specs=pl.BlockSpec((None, ts, tn), lambda g, s, j: (g, s, j)),
        out_shape=jax.ShapeDtypeStruct((G, S, D), BF16),
        compiler_params=_params("arbitrary", "arbitrary", "arbitrary"),
        name="merge_branches",
    )(o_sg, o_mla, gates, gates, w_pa, w_pb)


def _out_proj_kernel(m_ref, wo_ref, x_ref, g1_ref, gn_ref, sc_ref, sh_ref, x1_ref, h2_ref):
    y = jnp.dot(m_ref[...], wo_ref[...], preferred_element_type=F32)
    x1 = x_ref[...] + g1_ref[...] * y
    x1_ref[...] = x1
    h2_ref[...] = (_rms(x1, gn_ref[...]) * (1.0 + sc_ref[...]) + sh_ref[...]).astype(h2_ref.dtype)


def out_project(m, w_o, x, g1, g_norm2, sc2, sh2, ts):
    G, S, D = x.shape
    row = pl.BlockSpec((None, ts, D), lambda g, s: (g, s, 0))
    return pl.pallas_call(
        _out_proj_kernel,
        grid=(G, S // ts),
        in_specs=[row, pl.BlockSpec(w_o.shape, lambda g, s: (0, 0)), row,
                  _mod_spec(g1, ts), pl.BlockSpec((1, D), lambda g, s: (0, 0)),
                  _mod_spec(sc2, ts), _mod_spec(sh2, ts)],
        out_specs=[row, row],
        out_shape=[jax.ShapeDtypeStruct((G, S, D), F32), jax.ShapeDtypeStruct((G, S, D), BF16)],
        compiler_params=_params("arbitrary", "arbitrary"),
        name="out_project",
    )(m, w_o, x, g1, g_norm2.reshape(1, D), sc2, sh2)


def _mlp_kernel(h_ref, wup_ref, wdn_ref, x1_ref, g2_ref, o_ref):
    j = pl.program_id(2)
    hid = jnp.dot(h_ref[...], wup_ref[...], preferred_element_type=F32)
    hid = jnp.square(jnp.maximum(hid, 0.0)).astype(BF16)
    part = jnp.dot(hid, wdn_ref[...], preferred_element_type=F32)

    @pl.when(j == 0)
    def _():
        o_ref[...] = part

    @pl.when(j > 0)
    def _():
        o_ref[...] += part

    @pl.when(j == pl.num_programs(2) - 1)
    def _():
        o_ref[...] = x1_ref[...] + g2_ref[...] * o_ref[...]


def mlp_residual(h2, w_up, w_down, x1, g2, ts, th):
    G, S, D = x1.shape
    hidden = w_up.shape[1]
    row = pl.BlockSpec((None, ts, D), lambda g, s, j: (g, s, 0))
    return pl.pallas_call(
        _mlp_kernel,
        grid=(G, S // ts, hidden // th),
        in_specs=[row,
                  pl.BlockSpec((D, th), lambda g, s, j: (0, j)),
                  pl.BlockSpec((th, D), lambda g, s, j: (j, 0)),
                  row, _mod_spec(g2, ts)],
        out_specs=row,
        out_shape=jax.ShapeDtypeStruct((G, S, D), F32),
        compiler_params=_params("arbitrary", "arbitrary", "arbitrary"),
        name="mlp_residual",
    )(h2, w_up, w_down, x1, g2)


def _rope_tables(pos, n_rope):
    inv = jnp.float32(ROPE_BASE) ** (-jnp.arange(0, n_rope, 2, dtype=F32) / n_rope)
    ang = pos.astype(F32)[:, None] * inv[None, :]
    cos, sin = lax.optimization_barrier((jnp.cos(ang), jnp.sin(ang)))
    pad = jnp.zeros((pos.shape[0], LANES - n_rope), F32)
    return (jnp.concatenate([cos, cos, pad], axis=1), jnp.concatenate([-sin, sin, pad], axis=1))


def _dup_lanes(g):
    return jnp.concatenate([g.astype(F32), g.astype(F32)]).reshape(1, 2 * g.shape[0])


def _layer(x, mods, pos, P, ts, *, cache=None):
    sh1, sc1, g1, sh2, sc2, g2 = mods
    n_rope = P["n_rope"]
    tabs = _rope_tables(pos, n_rope)
    h, q = prenorm_q_project(x, P["g_norm1"], sc1, sh1, P["w_q"], P["g_q_a"], P["w_uq"], P["g_q_nope"],
                             P["g_q_rope"], tabs, ts, P["q_scale"])
    gates = proj_act(h, P["w_g"], "sigmoid", ts, P["w_g"].shape[1] // 2)
    if cache is None:
        v = proj_gelu_norm(h, P["w_v"], P["g_sg"], ts, BF16)
        o_sg = gelu_spatial_gate(h, P["w_u"], v, P["w_s"], P["b_rows"], ts)
        c_kv, k_rope, k, vv = kv_project(h, P["w_kv"], P["g_kv_a"], P["g_k_rope"], tabs, ts, n_rope,
                                         expand_weights=(P["w_uk"], P["w_uv"], P["g_k_nope"], P["k_shift"]))
        o_mla = attention_prompt(P["bounded"], q, k, vv, tq=min(2048, x.shape[1]),
                                 tk=min(1024, x.shape[1]), td=min(512, x.shape[1]))
        extra = ()
    else:
        cache_lat, cache_kr, layer, B, T = cache
        u = proj_act(h, P["w_u"], "gelu", ts, P["w_u"].shape[1])
        v = proj_gelu_norm(h, P["w_v"], P["g_sg"], ts, F32)
        W = v.shape[-1]
        o_sg = spatial_gate_open(u.reshape(B, T, W), v.reshape(B, T, W), P["w_lanes"][:T, :T],
                                 P["b_rows"][:T]).reshape(1, B * T, W)
        c_kv, k_rope, kr_pad = kv_project(h, P["w_kv"], P["g_kv_a"], P["g_k_rope"], tabs, ts, n_rope)
        o_mla = attention_cached(q, cache_lat, cache_kr, layer, c_kv, kr_pad, P["w_uk_t"], P["w_uv"],
                                 P["g_k_nope"], T)
        extra = (v,)
    m = merge_branches(o_sg, o_mla, gates, P["w_pa"], P["w_pb"], ts, P["w_pa"].shape[1] // 2)
    ts_res = min(ts, 512)
    x1, h2 = out_project(m, P["w_o"], x, g1, P["g_norm2"], sc2, sh2, ts_res)
    y = mlp_residual(h2, P["w_up"], P["w_down"], x1, g2, ts_res, min(2048, P["w_up"].shape[1]))
    return (y, c_kv, k_rope) + extra


def kernel(x_prompt, x_sample, cache_kv_latent, cache_k_rope, c_prompt, c_sample, w_ada, b_ada, g_norm1, g_norm2, w_in, g_sg, w_s, b_s, g_q_a, w_uq, g_q_nope, g_q_rope, g_kv_a, g_k_rope, w_uk, g_k_nope, w_uv, w_pa, w_pb, w_o, w_up, w_down):
    depth = w_in.shape[0]
    Bp, S, D = x_prompt.shape
    Bs, T, _ = x_sample.shape
    past = cache_kv_latent.shape[2]
    q_lora = g_q_a.shape[-1]
    lat = g_kv_a.shape[-1]
    n_nope = g_q_nope.shape[-1]
    n_rope = g_q_rope.shape[-1]
    H = w_uk.shape[2]
    sg_w = g_sg.shape[-1]
    off_q = 2 * sg_w
    off_kv = off_q + q_lora
    off_gate = off_kv + lat + n_rope
    assert n_nope == LANES and w_uv.shape[-1] == LANES and 2 * n_rope == LANES
    assert sg_w // SG_GROUPS == LANES and H == N_HEADS

    nb = Bp + Bs
    nb_pad = -(-nb // 8) * 8
    c_all = jnp.concatenate([c_prompt, c_sample, jnp.zeros((nb_pad - nb, D), F32)], axis=0)

    y_p, y_s = x_prompt, x_sample.reshape(1, Bs * T, D)
    outs = [[] for _ in range(5)]
    for l in range(depth):
        def w_in_cols(a, b):
            return w_in[l][:, a:b].astype(BF16)
        w_uq_l = w_uq[l].astype(BF16).reshape(q_lora, H, n_nope + n_rope)
        w_uq_pad = jnp.concatenate([w_uq_l, w_uq_l[:, :, n_nope:]], axis=2)
        q_scale = float((n_nope + n_rope) ** -0.5 * LOG2E)
        def sq_norm_bound(g_nope, g_rope):
            return n_nope * jnp.max(jnp.square(g_nope)) + n_rope * jnp.max(jnp.square(g_rope))
        score_bound = q_scale * jnp.sqrt(sq_norm_bound(g_q_nope[l], g_q_rope[l])
                                         * sq_norm_bound(g_k_nope[l], g_k_rope[l]))
        bounded = (score_bound <= BOUND_MAX).astype(jnp.int32).reshape(1)
        k_shift = jnp.where(jnp.arange(LANES) == LANES // 2, -score_bound, 0.0).astype(F32).reshape(1, LANES)
        P = {
            "n_rope": n_rope,
            "q_scale": q_scale, "k_shift": k_shift, "bounded": bounded,
            "g_norm1": g_norm1[l], "g_norm2": g_norm2[l], "g_sg": g_sg[l],
            "w_u": w_in_cols(0, sg_w), "w_v": w_in_cols(sg_w, off_q), "w_q": w_in_cols(off_q, off_kv),
            "w_kv": jnp.concatenate([w_in_cols(off_kv, off_gate), w_in_cols(off_kv + lat, off_gate)], axis=1),
            "w_g": w_in_cols(off_gate, w_in.shape[2]),
            "g_q_a": g_q_a[l], "w_uq": w_uq_pad.reshape(q_lora, H * HEAD_PAD),
            "g_q_nope": g_q_nope[l], "g_q_rope": _dup_lanes(g_q_rope[l]),
            "g_kv_a": g_kv_a[l], "g_k_rope": _dup_lanes(g_k_rope[l]),
            "w_uk": w_uk[l].astype(BF16).reshape(lat, H * n_nope),
            "w_uk_t": w_uk[l].astype(BF16).reshape(lat, H * n_nope).T,
            "w_uv": w_uv[l].astype(BF16).reshape(lat, H * LANES),
            "g_k_nope": g_k_nope[l],
            "w_s": w_s[l],
            "b_rows": jnp.repeat(b_s[l].T, LANES, axis=1),
            "w_lanes": jnp.repeat(w_s[l][:, :T, :T].transpose(1, 2, 0), LANES, axis=2),
            "w_pa": w_pa[l].astype(BF16), "w_pb": w_pb[l].astype(BF16), "w_o": w_o[l].astype(BF16),
            "w_up": w_up[l].astype(BF16), "w_down": w_down[l].astype(BF16),
        }
        mod = ada_project(c_all, w_ada[l], b_ada[l])
        mods_p = [a.reshape(Bp, 1, D) for a in jnp.split(mod[:Bp], 6, axis=-1)]
        mods_s = [jnp.repeat(a, T, axis=0).reshape(1, Bs * T, D)
                  for a in jnp.split(mod[Bp:nb], 6, axis=-1)]

        ts_p = min(1024, S)
        y_p, lp, kp = _layer(y_p, mods_p, jnp.arange(S), P, ts_p)
        pos_s = jnp.tile(past + jnp.arange(T), Bs)
        y_s, ls, ks, vs = _layer(y_s, mods_s, pos_s, P, Bs * T,
                                 cache=(cache_kv_latent, cache_k_rope, l, Bs, T))
        for lst, a in zip(outs, (lp, kp, ls.reshape(Bs, T, lat), ks.reshape(Bs, T, n_rope),
                                 vs.reshape(Bs, T, sg_w))):
            lst.append(a)
    return (y_p, y_s.reshape(Bs, T, D)) + tuple(o[0][None] if depth == 1 else jnp.stack(o) for o in outs)
```

```python
import functools
import math

import jax
import jax.numpy as jnp
import numpy as np
from jax import lax
from jax.experimental import pallas as pl
from jax.experimental.pallas import tpu as pltpu

F32 = jnp.float32
BF16 = jnp.bfloat16

EPS = 1e-6
ROPE_BASE = 10000.0
N_HEADS = 16
CHUNK = 64
SG_CHUNK = 128
SG_GROUPS = 16
LANES = 128
HEAD_PAD = 256
LOG2E = math.log2(math.e)
BOUND_MAX = 50.0

VMEM_LIMIT = 58 * 1024 * 1024


def _params(*sem):
    return pltpu.CompilerParams(dimension_semantics=sem, vmem_limit_bytes=VMEM_LIMIT)


def _rms(x, g):
    ms = jnp.mean(x * x, axis=-1, keepdims=True)
    return x * lax.rsqrt(ms + EPS) * g


def _rms_rope_group(t, g_dup):
    ms = jnp.mean(t * t, axis=-1, keepdims=True)
    return t * lax.rsqrt(ms + EPS) * g_dup


def _rope_group(t, tc, ts):
    return t * tc + pltpu.roll(t, LANES - LANES // 4, 1) * ts


def _gelu(z):
    return 0.5 * z * (1.0 + lax.erf(z * np.float32(math.sqrt(0.5))))


def _mod_spec(mod, ts):
    d = mod.shape[-1]
    if mod.shape[1] == 1:
        return pl.BlockSpec((None, 1, d), lambda g, s, *_: (g, 0, 0))
    return pl.BlockSpec((None, ts, d), lambda g, s, *_: (g, s, 0))


def _ada_kernel(c_ref, w_ref, b_ref, o_ref):
    c = c_ref[...]
    s = (c * jax.nn.sigmoid(c)).astype(BF16)
    o_ref[...] = jnp.dot(s, w_ref[...].astype(BF16), preferred_element_type=F32) + b_ref[...]


def ada_project(c, w_ada, b_ada, tn=1024):
    r, d = c.shape
    n = w_ada.shape[1]
    return pl.pallas_call(
        _ada_kernel,
        grid=(n // tn,),
        in_specs=[pl.BlockSpec((r, d), lambda j: (0, 0)),
                  pl.BlockSpec((d, tn), lambda j: (0, j)),
                  pl.BlockSpec((1, tn), lambda j: (0, j))],
        out_specs=pl.BlockSpec((r, tn), lambda j: (0, j)),
        out_shape=jax.ShapeDtypeStruct((r, n), F32),
        compiler_params=_params("arbitrary"),
        name="ada_project",
    )(c, w_ada, b_ada.reshape(1, n))


def _proj_act_kernel(h_ref, w_ref, o_ref, *, act):
    z = jnp.dot(h_ref[...], w_ref[...], preferred_element_type=F32)
    if act == "gelu":
        a = _gelu(z)
    else:
        a = jax.nn.sigmoid(z)
    o_ref[...] = a.astype(o_ref.dtype)


def proj_act(h, w, act, ts, tn):
    G, S, D = h.shape
    n = w.shape[1]
    return pl.pallas_call(
        functools.partial(_proj_act_kernel, act=act),
        grid=(G, S // ts, n // tn),
        in_specs=[pl.BlockSpec((None, ts, D), lambda g, s, j: (g, s, 0)),
                  pl.BlockSpec((D, tn), lambda g, s, j: (0, j))],
        out_specs=pl.BlockSpec((None, ts, tn), lambda g, s, j: (g, s, j)),
        out_shape=jax.ShapeDtypeStruct((G, S, n), BF16),
        compiler_params=_params("arbitrary", "arbitrary", "arbitrary"),
        name="proj_" + act,
    )(h, w)


def _proj_gelu_norm_kernel(h_ref, w_ref, g_ref, o_ref):
    z = jnp.dot(h_ref[...], w_ref[...], preferred_element_type=F32)
    o_ref[...] = _rms(_gelu(z), g_ref[...]).astype(o_ref.dtype)


def proj_gelu_norm(h, w, g, ts, out_dtype):
    G, S, D = h.shape
    n = w.shape[1]
    return pl.pallas_call(
        _proj_gelu_norm_kernel,
        grid=(G, S // ts),
        in_specs=[pl.BlockSpec((None, ts, D), lambda g_, s: (g_, s, 0)),
                  pl.BlockSpec((D, n), lambda g_, s: (0, 0)),
                  pl.BlockSpec((1, n), lambda g_, s: (0, 0))],
        out_specs=pl.BlockSpec((None, ts, n), lambda g_, s: (g_, s, 0)),
        out_shape=jax.ShapeDtypeStruct((G, S, n), out_dtype),
        compiler_params=_params("arbitrary", "arbitrary"),
        name="proj_gelu_norm",
    )(h, w, g.reshape(1, n))


def _q_proj_kernel(x_ref, g1_ref, sc_ref, sh_ref, wq_ref, gqa_ref, wuq_ref, gn_ref, gr_ref, tc_ref, ts_ref,
                   h_ref, q_ref, *, q_scale):
    h = (_rms(x_ref[...], g1_ref[...]) * (1.0 + sc_ref[...]) + sh_ref[...]).astype(h_ref.dtype)
    h_ref[...] = h
    zq = jnp.dot(h, wq_ref[...], preferred_element_type=F32)
    zn = _rms(zq, gqa_ref[...]).astype(BF16)
    tc, ts = tc_ref[...], ts_ref[...]
    gains = jnp.concatenate([gn_ref[...], gr_ref[...]], axis=1) * q_scale
    same_tile = (lax.broadcasted_iota(jnp.int32, (HEAD_PAD, HEAD_PAD), 0) // LANES
                 == lax.broadcasted_iota(jnp.int32, (HEAD_PAD, HEAD_PAD), 1) // LANES)
    tile_mean = jnp.where(same_tile, 1.0 / LANES, 0.0).astype(BF16)
    shift_lane = jnp.where(lax.broadcasted_iota(jnp.int32, (1, LANES), 1) == LANES // 2, 1.0, 0.0)
    for pair in range(q_ref.shape[0] // 2):
        blk2 = jnp.dot(zn, wuq_ref[:, pair * 2 * HEAD_PAD:(pair + 1) * 2 * HEAD_PAD],
                       preferred_element_type=F32)
        for sub in range(2):
            hh = 2 * pair + sub
            blk = blk2[:, sub * HEAD_PAD:(sub + 1) * HEAD_PAD]
            ms = jnp.dot((blk * blk).astype(BF16), tile_mean, preferred_element_type=F32)
            y = blk * lax.rsqrt(ms + EPS) * gains
            q_ref[hh, :, :LANES] = y[:, :LANES].astype(q_ref.dtype)
            q_ref[hh, :, LANES:] = (_rope_group(y[:, LANES:], tc, ts) + shift_lane).astype(q_ref.dtype)


def prenorm_q_project(x, g_norm, sc, sh, w_q, g_q_a, w_uq_pad, g_nope, g_rope_pad, tabs, ts, q_scale):
    G, S, D = x.shape
    ql = w_q.shape[1]
    H = w_uq_pad.shape[1] // HEAD_PAD
    const = lambda g, s: (0, 0)
    tab_spec = pl.BlockSpec((ts, LANES), lambda g, s: (s, 0))
    row = pl.BlockSpec((None, ts, D), lambda g, s: (g, s, 0))
    return pl.pallas_call(
        functools.partial(_q_proj_kernel, q_scale=q_scale),
        grid=(G, S // ts),
        in_specs=[row, pl.BlockSpec((1, D), const), _mod_spec(sc, ts), _mod_spec(sh, ts),
                  pl.BlockSpec((D, ql), const),
                  pl.BlockSpec((1, ql), const),
                  pl.BlockSpec((ql, H * HEAD_PAD), const),
                  pl.BlockSpec((1, LANES), const),
                  pl.BlockSpec((1, LANES), const),
                  tab_spec, tab_spec],
        out_specs=[row, pl.BlockSpec((None, H, ts, HEAD_PAD), lambda g, s: (g, 0, s, 0))],
        out_shape=[jax.ShapeDtypeStruct((G, S, D), BF16),
                   jax.ShapeDtypeStruct((G, H, S, HEAD_PAD), BF16)],
        compiler_params=_params("arbitrary", "arbitrary"),
        name="prenorm_q_project",
    )(x, g_norm.reshape(1, D), sc, sh, w_q, g_q_a.reshape(1, ql), w_uq_pad, g_nope.reshape(1, LANES),
      g_rope_pad, *tabs)


def _kv_proj_kernel(h_ref, wkv_ref, gkva_ref, gkr_ref, tc_ref, ts_ref, *rest, n_rope, expand):
    if expand:
        wuk_ref, wuv_ref, gkn_ref, kshift_ref, c_ref, kr_ref, k_ref, v_ref = rest
    else:
        c_ref, kr_ref, krp_ref = rest
    lat = c_ref.shape[-1]
    z = jnp.dot(h_ref[...], wkv_ref[...], preferred_element_type=F32)
    c = _rms(z[:, :lat], gkva_ref[...])
    c_ref[...] = c
    rope = _rope_group(_rms_rope_group(z[:, lat:], gkr_ref[...]), tc_ref[...], ts_ref[...])
    kr_ref[...] = rope[:, :n_rope]
    rope_b = rope.astype(BF16)
    if not expand:
        krp_ref[...] = rope_b
        return
    cb = c.astype(BF16)
    gkn = gkn_ref[...]
    rope_b = (rope + kshift_ref[...]).astype(BF16)
    for pair in range(k_ref.shape[0] // 2):
        cols = slice(pair * 2 * LANES, (pair + 1) * 2 * LANES)
        kn2 = jnp.dot(cb, wuk_ref[:, cols], preferred_element_type=F32)
        v2 = jnp.dot(cb, wuv_ref[:, cols], preferred_element_type=F32)
        for sub in range(2):
            hh = 2 * pair + sub
            lanes = slice(sub * LANES, (sub + 1) * LANES)
            k_ref[hh, :, :LANES] = _rms(kn2[:, lanes], gkn).astype(k_ref.dtype)
            k_ref[hh, :, LANES:] = rope_b
            v_ref[hh] = v2[:, lanes].astype(v_ref.dtype)


def kv_project(h, w_kv, g_kv_a, g_k_rope_pad, tabs, ts, n_rope, expand_weights=None):
    G, S, D = h.shape
    lat = g_kv_a.shape[-1]
    const = lambda g, s: (0, 0)
    tab_spec = pl.BlockSpec((ts, LANES), lambda g, s: (s, 0))
    in_specs = [pl.BlockSpec((None, ts, D), lambda g, s: (g, s, 0)),
                pl.BlockSpec((D, lat + LANES), const),
                pl.BlockSpec((1, lat), const),
                pl.BlockSpec((1, LANES), const),
                tab_spec, tab_spec]
    args = [h, w_kv, g_kv_a.reshape(1, lat), g_k_rope_pad, *tabs]
    out_specs = [pl.BlockSpec((None, ts, lat), lambda g, s: (g, s, 0)),
                 pl.BlockSpec((None, ts, n_rope), lambda g, s: (g, s, 0))]
    out_shape = [jax.ShapeDtypeStruct((G, S, lat), F32),
                 jax.ShapeDtypeStruct((G, S, n_rope), F32)]
    expand = expand_weights is not None
    if expand:
        w_uk, w_uv, g_k_nope, k_shift = expand_weights
        H = w_uk.shape[1] // LANES
        in_specs += [pl.BlockSpec(w_uk.shape, const), pl.BlockSpec(w_uv.shape, const),
                     pl.BlockSpec((1, LANES), const), pl.BlockSpec((1, LANES), const)]
        args += [w_uk, w_uv, g_k_nope.reshape(1, LANES), k_shift]
        out_specs += [pl.BlockSpec((None, H, ts, HEAD_PAD), lambda g, s: (g, 0, s, 0)),
                      pl.BlockSpec((None, H, ts, LANES), lambda g, s: (g, 0, s, 0))]
        out_shape += [jax.ShapeDtypeStruct((G, H, S, HEAD_PAD), BF16),
                      jax.ShapeDtypeStruct((G, H, S, LANES), BF16)]
    else:
        out_specs.append(pl.BlockSpec((None, ts, LANES), lambda g, s: (g, s, 0)))
        out_shape.append(jax.ShapeDtypeStruct((G, S, LANES), BF16))
    return pl.pallas_call(
        functools.partial(_kv_proj_kernel, n_rope=n_rope, expand=expand),
        grid=(G, S // ts),
        in_specs=in_specs,
        out_specs=out_specs,
        out_shape=out_shape,
        compiler_params=_params("arbitrary", "arbitrary"),
        name="kv_project",
    )(*args)


def _gelu_gate_kernel(h_ref, wu_ref, v_ref, w_ref, b_ref, o_ref, wm_sc, u_sc, *, n_chunks):
    u_sc[...] = _gelu(jnp.dot(h_ref[...], wu_ref[...], preferred_element_type=F32)).astype(u_sc.dtype)
    row = lax.broadcasted_iota(jnp.int32, (SG_CHUNK, SG_CHUNK), 0)
    col = lax.broadcasted_iota(jnp.int32, (SG_CHUNK, SG_CHUNK), 1)
    tril = col <= row
    n_groups = w_ref.shape[0]
    for g in range(n_groups):
        wm_sc[g] = jnp.where(tril, w_ref[g], 0.0).astype(BF16)

    def chunk_body(n, carry):
        rows = pl.ds(pl.multiple_of(n * SG_CHUNK, SG_CHUNK), SG_CHUNK)
        for g in range(n_groups):
            lanes = slice(g * LANES, (g + 1) * LANES)
            mix = jnp.dot(wm_sc[g], v_ref[rows, lanes], preferred_element_type=F32) + b_ref[:, lanes]
            o_ref[rows, lanes] = (u_sc[rows, lanes].astype(F32) * mix).astype(o_ref.dtype)
        return carry

    lax.fori_loop(0, n_chunks, chunk_body, 0)


def gelu_spatial_gate(h, w_u, v, w_s, bias_rows, ts):
    G, S, D = h.shape
    W = w_u.shape[1]
    blk = pl.BlockSpec((None, ts, W), lambda g, s: (g, s, 0))
    return pl.pallas_call(
        functools.partial(_gelu_gate_kernel, n_chunks=ts // SG_CHUNK),
        grid=(G, S // ts),
        in_specs=[pl.BlockSpec((None, ts, D), lambda g, s: (g, s, 0)),
                  pl.BlockSpec(w_u.shape, lambda g, s: (0, 0)),
                  blk,
                  pl.BlockSpec(w_s.shape, lambda g, s: (0, 0, 0)),
                  pl.BlockSpec(bias_rows.shape, lambda g, s: (0, 0))],
        out_specs=blk,
        out_shape=jax.ShapeDtypeStruct((G, S, W), BF16),
        scratch_shapes=[pltpu.VMEM(w_s.shape, BF16), pltpu.VMEM((ts, W), BF16)],
        compiler_params=_params("arbitrary", "arbitrary"),
        name="gelu_spatial_gate",
    )(h, w_u, v, w_s, bias_rows)


def _spatial_gate_open_kernel(u_ref, v_ref, wl_ref, b_ref, o_ref):
    T = u_ref.shape[1]
    for i in range(T):
        acc = b_ref[i:i + 1, :] + wl_ref[i, 0:1, :] * v_ref[:, 0, :]
        for j in range(1, i + 1):
            acc = acc + wl_ref[i, j:j + 1, :] * v_ref[:, j, :]
        o_ref[:, i, :] = (u_ref[:, i, :].astype(F32) * acc).astype(o_ref.dtype)


def spatial_gate_open(u, v, w_lanes, bias_rows):
    B, T, W = u.shape
    full = lambda a: pl.BlockSpec(a.shape, lambda i: (0,) * a.ndim)
    return pl.pallas_call(
        _spatial_gate_open_kernel,
        grid=(1,),
        in_specs=[full(u), full(v), full(w_lanes), full(bias_rows)],
        out_specs=pl.BlockSpec((B, T, W), lambda i: (0, 0, 0)),
        out_shape=jax.ShapeDtypeStruct((B, T, W), BF16),
        compiler_params=_params("arbitrary"),
        name="spatial_gate_open",
    )(u, v, w_lanes, bias_rows)


def _attn_kernel(bounded_ref, q_ref, k_ref, v_ref, o_ref, m_sc, acc_sc, *, tq, tk, td):
    i = pl.program_id(2)
    dv = v_ref.shape[-1]
    unroll = tq // tk
    nt = (((1,), (1,)), ((), ()))
    ones = jnp.ones((tk, LANES), BF16)
    d0 = pl.multiple_of(i * tq, tq)
    chunk_mask = (lax.broadcasted_iota(jnp.int32, (td, td), 1) // CHUNK
                  <= lax.broadcasted_iota(jnp.int32, (td, td), 0) // CHUNK)

    def scores(rows0, k0, keys):
        return lax.dot_general(q_ref[rows0:, :], k_ref[pl.ds(k0, keys), :], nt, preferred_element_type=F32)

    def diag_scores(c):
        r0 = c * td
        s = scores(r0, d0 + r0, td)
        top = jnp.where(chunk_mask, s[:td], -jnp.inf)
        return top if r0 + td == tq else jnp.concatenate([top, s[td:]], axis=0)

    def values(k0, keys):
        return jnp.concatenate([v_ref[pl.ds(k0, keys), :], ones[:keys]], axis=1)

    def finish():
        acc = acc_sc[...]
        o_ref[...] = (acc[:, :dv] / acc[:, dv:]).astype(o_ref.dtype)

    @pl.when(bounded_ref[0] == 1)
    def _():
        for c in range(tq // td):
            r0 = c * td
            pv = jnp.dot(jnp.exp2(diag_scores(c)).astype(BF16), values(d0 + r0, td),
                         preferred_element_type=F32)
            if c == 0:
                acc_sc[...] = pv
            else:
                acc_sc[r0:, :] += pv

        def sweep(k0, n_tiles):
            acc = acc_sc[...]
            for u in range(n_tiles):
                p = jnp.exp2(scores(0, k0 + u * tk, tk)).astype(BF16)
                acc = acc + jnp.dot(p, values(k0 + u * tk, tk), preferred_element_type=F32)
            acc_sc[...] = acc

        def body(j, carry):
            sweep(pl.multiple_of(j * (2 * tq), 2 * tq), 2 * unroll)
            return carry

        lax.fori_loop(0, i // 2, body, 0)

        @pl.when(i % 2 == 1)
        def _():
            sweep(pl.multiple_of((i - 1) * tq, tq), unroll)

        finish()

    @pl.when(bounded_ref[0] != 1)
    def _():
        def step(s, k0, m_old, acc_old):
            rows, keys = s.shape
            row_max = jnp.max(s, axis=-1, keepdims=True)
            m_new = (jnp.broadcast_to(row_max, (rows, LANES)) if m_old is None
                     else jnp.maximum(m_old, row_max))
            p = jnp.concatenate([jnp.exp2(s[:, c * LANES:(c + 1) * LANES] - m_new)
                                 for c in range(keys // LANES)], axis=1).astype(BF16)
            pv = jnp.dot(p, values(k0, keys), preferred_element_type=F32)
            if m_old is None:
                return m_new, pv
            alpha = jnp.exp2(m_old - m_new)
            return m_new, jnp.concatenate([alpha] * ((dv + LANES) // LANES), axis=1) * acc_old + pv

        for c in range(tq // td):
            r0 = c * td
            if c == 0:
                m, acc = step(diag_scores(c), d0, None, None)
            else:
                m, acc = step(diag_scores(c), d0 + r0, m_sc[r0:, :], acc_sc[r0:, :])
            m_sc[r0:, :] = m
            acc_sc[r0:, :] = acc

        def body(j, carry):
            k0 = pl.multiple_of(j * tq, tq)
            ss = [scores(0, k0 + u * tk, tk) for u in range(unroll)]
            m, acc = m_sc[...], acc_sc[...]
            for u in range(unroll):
                m, acc = step(ss[u], k0 + u * tk, m, acc)
            m_sc[...] = m
            acc_sc[...] = acc
            return carry

        lax.fori_loop(0, i, body, 0)
        finish()


def attention_prompt(bounded, q, k, v, tq, tk, td):
    G, H, S, _ = q.shape
    dv = v.shape[-1]
    assert dv == LANES and tq % tk == 0 and tq % td == 0 and td % LANES == 0 and td <= tk
    return pl.pallas_call(
        functools.partial(_attn_kernel, tq=tq, tk=tk, td=td),
        grid=(G, H, S // tq),
        in_specs=[pl.BlockSpec(memory_space=pltpu.SMEM),
                  pl.BlockSpec((None, None, tq, HEAD_PAD), lambda g, h, i: (g, h, i, 0)),
                  pl.BlockSpec((None, None, S, HEAD_PAD), lambda g, h, i: (g, h, 0, 0)),
                  pl.BlockSpec((None, None, S, dv), lambda g, h, i: (g, h, 0, 0))],
        out_specs=pl.BlockSpec((None, tq, dv), lambda g, h, i: (g, i, h)),
        out_shape=jax.ShapeDtypeStruct((G, S, H * dv), BF16),
        scratch_shapes=[pltpu.VMEM((tq, LANES), F32), pltpu.VMEM((tq, dv + LANES), F32)],
        compiler_params=_params("arbitrary", "arbitrary", "arbitrary"),
        name="attention_prompt",
    )(bounded, q, k, v)


def _attn_cached_kernel(q_ref, cache_ref, ckr_ref, cnew_ref, krnew_ref, wukt_ref, wuv_ref, gkn_ref,
                        o_ref, call_sc, krall_sc, p_sc, *, past, n_new):
    H = q_ref.shape[0]
    L = past + n_new
    Lp = call_sc.shape[0]
    lat = call_sc.shape[1]
    call_sc[0:past, :] = cache_ref[...].astype(BF16)
    call_sc[past:L, :] = cnew_ref[...].astype(BF16)
    call_sc[L:Lp, :] = jnp.zeros((Lp - L, lat), BF16)
    n_rope = ckr_ref.shape[-1]
    krall_sc[0:past, n_rope:] = jnp.zeros((past, LANES - n_rope), BF16)
    krall_sc[0:past, :n_rope] = ckr_ref[...].astype(BF16)
    krall_sc[past:L, :] = krnew_ref[...]
    krall_sc[L:Lp, :] = jnp.zeros((Lp - L, LANES), BF16)
    call = call_sc[...]
    gkn = gkn_ref[...]
    valid = lax.broadcasted_iota(jnp.int32, (n_new, Lp), 1) < L
    nt = (((1,), (1,)), ((), ()))
    q_rope = jnp.concatenate([q_ref[hh][:, LANES:] for hh in range(H)], axis=0)
    s_rope = lax.dot_general(q_rope, krall_sc[...], nt, preferred_element_type=F32)
    group = 4
    for grp in range(H // group):
        kn_t = lax.dot_general(wukt_ref[grp * group * LANES:(grp + 1) * group * LANES, :], call, nt,
                               preferred_element_type=F32)
        for sub in range(group):
            hh = group * grp + sub
            blk = kn_t[sub * LANES:(sub + 1) * LANES, :]
            inv_rms = lax.rsqrt(jnp.mean(blk * blk, axis=0, keepdims=True) + EPS)
            q_nope = (q_ref[hh][:, :LANES].astype(F32) * gkn).astype(BF16)
            s = jnp.dot(q_nope, blk.astype(BF16), preferred_element_type=F32) * inv_rms
            s = s + s_rope[hh * n_new:(hh + 1) * n_new, :]
            s = jnp.where(valid, s, -jnp.inf)
            m = jnp.max(s, axis=-1, keepdims=True)
            p = jnp.exp2(s - m)
            p = p / jnp.sum(p, axis=-1, keepdims=True)
            p_sc[hh * n_new:(hh + 1) * n_new, :] = p.astype(BF16)
    o_lat = jnp.dot(p_sc[...], call, preferred_element_type=F32).astype(BF16)
    for hh in range(H):
        o_ref[:, hh * LANES:(hh + 1) * LANES] = jnp.dot(
            o_lat[hh * n_new:(hh + 1) * n_new, :], wuv_ref[:, hh * LANES:(hh + 1) * LANES],
            preferred_element_type=F32).astype(o_ref.dtype)


def attention_cached(q, cache_lat, cache_kr, layer, c_new, kr_new_pad, w_uk_t, w_uv, g_k_nope, n_new):
    _, H, BT, _ = q.shape
    _, B, past, lat = cache_lat.shape
    n_rope = cache_kr.shape[-1]
    L = past + n_new
    Lp = -(-L // LANES) * LANES
    const = lambda b: (0, 0)
    return pl.pallas_call(
        functools.partial(_attn_cached_kernel, past=past, n_new=n_new),
        grid=(B,),
        in_specs=[pl.BlockSpec((None, H, n_new, HEAD_PAD), lambda b: (0, 0, b, 0)),
                  pl.BlockSpec((None, None, past, lat), lambda b: (layer, b, 0, 0)),
                  pl.BlockSpec((None, None, past, n_rope), lambda b: (layer, b, 0, 0)),
                  pl.BlockSpec((None, n_new, lat), lambda b: (0, b, 0)),
                  pl.BlockSpec((None, n_new, LANES), lambda b: (0, b, 0)),
                  pl.BlockSpec(w_uk_t.shape, const), pl.BlockSpec(w_uv.shape, const),
                  pl.BlockSpec((1, LANES), const)],
        out_specs=pl.BlockSpec((None, n_new, H * LANES), lambda b: (0, b, 0)),
        out_shape=jax.ShapeDtypeStruct((1, BT, H * LANES), BF16),
        scratch_shapes=[pltpu.VMEM((Lp, lat), BF16), pltpu.VMEM((Lp, LANES), BF16),
                        pltpu.VMEM((H * n_new, Lp), BF16)],
        compiler_params=_params("arbitrary"),
        name="attention_cached",
    )(q, cache_lat, cache_kr, c_new, kr_new_pad, w_uk_t, w_uv, g_k_nope.reshape(1, LANES))


def _merge_kernel(a_ref, b_ref, ga_ref, gb_ref, wpa_ref, wpb_ref, o_ref):
    pa = jnp.dot(a_ref[...], wpa_ref[...], preferred_element_type=F32)
    pb = jnp.dot(b_ref[...], wpb_ref[...], preferred_element_type=F32)
    o_ref[...] = (ga_ref[...].astype(F32) * pa + gb_ref[...].astype(F32) * pb).astype(o_ref.dtype)


def merge_branches(o_sg, o_mla, gates, w_pa, w_pb, ts, tn):
    G, S, W = o_sg.shape
    D = w_pa.shape[1]
    nj = D // tn
    row = lambda g, s, j: (g, s, 0)
    return pl.pallas_call(
        _merge_kernel,
        grid=(G, S // ts, nj),
        in_specs=[pl.BlockSpec((None, ts, W), row),
                  pl.BlockSpec((None, ts, o_mla.shape[-1]), row),
                  pl.BlockSpec((None, ts, tn), lambda g, s, j: (g, s, j)),
                  pl.BlockSpec((None, ts, tn), lambda g, s, j: (g, s, j + nj)),
                  pl.BlockSpec((W, tn), lambda g, s, j: (0, j)),
                  pl.BlockSpec((o_mla.shape[-1], tn), lambda g, s, j: (0, j))],
        out_specs=pl.BlockSpec((None, ts, tn), lambda g, s, j: (g, s, j)),
        out_shape=jax.ShapeDtypeStruct((G, S, D), BF16),
        compiler_params=_params("arbitrary", "arbitrary", "arbitrary"),
        name="merge_branches",
    )(o_sg, o_mla, gates, gates, w_pa, w_pb)


def _out_proj_kernel(m_ref, wo_ref, x_ref, g1_ref, gn_ref, sc_ref, sh_ref, x1_ref, h2_ref):
    y = jnp.dot(m_ref[...], wo_ref[...], preferred_element_type=F32)
    x1 = x_ref[...] + g1_ref[...] * y
    x1_ref[...] = x1
    h2_ref[...] = (_rms(x1, gn_ref[...]) * (1.0 + sc_ref[...]) + sh_ref[...]).astype(h2_ref.dtype)


def out_project(m, w_o, x, g1, g_norm2, sc2, sh2, ts):
    G, S, D = x.shape
    row = pl.BlockSpec((None, ts, D), lambda g, s: (g, s, 0))
    return pl.pallas_call(
        _out_proj_kernel,
        grid=(G, S // ts),
        in_specs=[row, pl.BlockSpec(w_o.shape, lambda g, s: (0, 0)), row,
                  _mod_spec(g1, ts), pl.BlockSpec((1, D), lambda g, s: (0, 0)),
                  _mod_spec(sc2, ts), _mod_spec(sh2, ts)],
        out_specs=[row, row],
        out_shape=[jax.ShapeDtypeStruct((G, S, D), F32), jax.ShapeDtypeStruct((G, S, D), BF16)],
        compiler_params=_params("arbitrary", "arbitrary"),
        name="out_project",
    )(m, w_o, x, g1, g_norm2.reshape(1, D), sc2, sh2)


def _mlp_kernel(h_ref, wup_ref, wdn_ref, x1_ref, g2_ref, o_ref):
    j = pl.program_id(2)
    hid = jnp.dot(h_ref[...], wup_ref[...], preferred_element_type=F32)
    hid = jnp.square(jnp.maximum(hid, 0.0)).astype(BF16)
    part = jnp.dot(hid, wdn_ref[...], preferred_element_type=F32)

    @pl.when(j == 0)
    def _():
        o_ref[...] = part

    @pl.when(j > 0)
    def _():
        o_ref[...] += part

    @pl.when(j == pl.num_programs(2) - 1)
    def _():
        o_ref[...] = x1_ref[...] + g2_ref[...] * o_ref[...]


def mlp_residual(h2, w_up, w_down, x1, g2, ts, th):
    G, S, D = x1.shape
    hidden = w_up.shape[1]
    row = pl.BlockSpec((None, ts, D), lambda g, s, j: (g, s, 0))
    return pl.pallas_call(
        _mlp_kernel,
        grid=(G, S // ts, hidden // th),
        in_specs=[row,
                  pl.BlockSpec((D, th), lambda g, s, j: (0, j)),
                  pl.BlockSpec((th, D), lambda g, s, j: (j, 0)),
                  row, _mod_spec(g2, ts)],
        out_specs=row,
        out_shape=jax.ShapeDtypeStruct((G, S, D), F32),
        compiler_params=_params("arbitrary", "arbitrary", "arbitrary"),
        name="mlp_residual",
    )(h2, w_up, w_down, x1, g2)


def _rope_tables(pos, n_rope):
    inv = jnp.float32(ROPE_BASE) ** (-jnp.arange(0, n_rope, 2, dtype=F32) / n_rope)
    ang = pos.astype(F32)[:, None] * inv[None, :]
    cos, sin = lax.optimization_barrier((jnp.cos(ang), jnp.sin(ang)))
    pad = jnp.zeros((pos.shape[0], LANES - n_rope), F32)
    return (jnp.concatenate([cos, cos, pad], axis=1), jnp.concatenate([-sin, sin, pad], axis=1))


def _dup_lanes(g):
    return jnp.concatenate([g.astype(F32), g.astype(F32)]).reshape(1, 2 * g.shape[0])


def _layer(x, mods, pos, P, ts, *, cache=None):
    sh1, sc1, g1, sh2, sc2, g2 = mods
    n_rope = P["n_rope"]
    tabs = _rope_tables(pos, n_rope)
    h, q = prenorm_q_project(x, P["g_norm1"], sc1, sh1, P["w_q"], P["g_q_a"], P["w_uq"], P["g_q_nope"],
                             P["g_q_rope"], tabs, ts, P["q_scale"])
    gates = proj_act(h, P["w_g"], "sigmoid", ts, P["w_g"].shape[1] // 2)
    if cache is None:
        v = proj_gelu_norm(h, P["w_v"], P["g_sg"], ts, BF16)
        o_sg = gelu_spatial_gate(h, P["w_u"], v, P["w_s"], P["b_rows"], ts)
        c_kv, k_rope, k, vv = kv_project(h, P["w_kv"], P["g_kv_a"], P["g_k_rope"], tabs, ts, n_rope,
                                         expand_weights=(P["w_uk"], P["w_uv"], P["g_k_nope"], P["k_shift"]))
        o_mla = attention_prompt(P["bounded"], q, k, vv, tq=min(2048, x.shape[1]),
                                 tk=min(1024, x.shape[1]), td=min(512, x.shape[1]))
        extra = ()
    else:
        cache_lat, cache_kr, layer, B, T = cache
        u = proj_act(h, P["w_u"], "gelu", ts, P["w_u"].shape[1])
        v = proj_gelu_norm(h, P["w_v"], P["g_sg"], ts, F32)
        W = v.shape[-1]
        o_sg = spatial_gate_open(u.reshape(B, T, W), v.reshape(B, T, W), P["w_lanes"][:T, :T],
                                 P["b_rows"][:T]).reshape(1, B * T, W)
        c_kv, k_rope, kr_pad = kv_project(h, P["w_kv"], P["g_kv_a"], P["g_k_rope"], tabs, ts, n_rope)
        o_mla = attention_cached(q, cache_lat, cache_kr, layer, c_kv, kr_pad, P["w_uk_t"], P["w_uv"],
                                 P["g_k_nope"], T)
        extra = (v,)
    m = merge_branches(o_sg, o_mla, gates, P["w_pa"], P["w_pb"], ts, P["w_pa"].shape[1] // 2)
    ts_res = min(ts, 512)
    x1, h2 = out_project(m, P["w_o"], x, g1, P["g_norm2"], sc2, sh2, ts_res)
    y = mlp_residual(h2, P["w_up"], P["w_down"], x1, g2, ts_res, min(2048, P["w_up"].shape[1]))
    return (y, c_kv, k_rope) + extra


def kernel(x_prompt, x_sample, cache_kv_latent, cache_k_rope, c_prompt, c_sample, w_ada, b_ada, g_norm1, g_norm2, w_in, g_sg, w_s, b_s, g_q_a, w_uq, g_q_nope, g_q_rope, g_kv_a, g_k_rope, w_uk, g_k_nope, w_uv, w_pa, w_pb, w_o, w_up, w_down):
    depth = w_in.shape[0]
    Bp, S, D = x_prompt.shape
    Bs, T, _ = x_sample.shape
    past = cache_kv_latent.shape[2]
    q_lora = g_q_a.shape[-1]
    lat = g_kv_a.shape[-1]
    n_nope = g_q_nope.shape[-1]
    n_rope = g_q_rope.shape[-1]
    H = w_uk.shape[2]
    sg_w = g_sg.shape[-1]
    off_q = 2 * sg_w
    off_kv = off_q + q_lora
    off_gate = off_kv + lat + n_rope
    assert n_nope == LANES and w_uv.shape[-1] == LANES and 2 * n_rope == LANES
    assert sg_w // SG_GROUPS == LANES and H == N_HEADS

    nb = Bp + Bs
    nb_pad = -(-nb // 8) * 8
    c_all = jnp.concatenate([c_prompt, c_sample, jnp.zeros((nb_pad - nb, D), F32)], axis=0)

    y_p, y_s = x_prompt, x_sample.reshape(1, Bs * T, D)
    outs = [[] for _ in range(5)]
    for l in range(depth):
        def w_in_cols(a, b):
            return w_in[l][:, a:b].astype(BF16)
        w_uq_l = w_uq[l].astype(BF16).reshape(q_lora, H, n_nope + n_rope)
        w_uq_pad = jnp.concatenate([w_uq_l, w_uq_l[:, :, n_nope:]], axis=2)
        q_scale = float((n_nope + n_rope) ** -0.5 * LOG2E)
        def sq_norm_bound(g_nope, g_rope):
            return n_nope * jnp.max(jnp.square(g_nope)) + n_rope * jnp.max(jnp.square(g_rope))
        score_bound = q_scale * jnp.sqrt(sq_norm_bound(g_q_nope[l], g_q_rope[l])
                                         * sq_norm_bound(g_k_nope[l], g_k_rope[l]))
        bounded = (score_bound <= BOUND_MAX).astype(jnp.int32).reshape(1)
        k_shift = jnp.where(jnp.arange(LANES) == LANES // 2, -score_bound, 0.0).astype(F32).reshape(1, LANES)
        P = {
            "n_rope": n_rope,
            "q_scale": q_scale, "k_shift": k_shift, "bounded": bounded,
            "g_norm1": g_norm1[l], "g_norm2": g_norm2[l], "g_sg": g_sg[l],
            "w_u": w_in_cols(0, sg_w), "w_v": w_in_cols(sg_w, off_q), "w_q": w_in_cols(off_q, off_kv),
            "w_kv": jnp.concatenate([w_in_cols(off_kv, off_gate), w_in_cols(off_kv + lat, off_gate)], axis=1),
            "w_g": w_in_cols(off_gate, w_in.shape[2]),
            "g_q_a": g_q_a[l], "w_uq": w_uq_pad.reshape(q_lora, H * HEAD_PAD),
            "g_q_nope": g_q_nope[l], "g_q_rope": _dup_lanes(g_q_rope[l]),
            "g_kv_a": g_kv_a[l], "g_k_rope": _dup_lanes(g_k_rope[l]),
            "w_uk": w_uk[l].astype(BF16).reshape(lat, H * n_nope),
            "w_uk_t": w_uk[l].astype(BF16).reshape(lat, H * n_nope).T,
            "w_uv": w_uv[l].astype(BF16).reshape(lat, H * LANES),
            "g_k_nope": g_k_nope[l],
            "w_s": w_s[l],
            "b_rows": jnp.repeat(b_s[l].T, LANES, axis=1),
            "w_lanes": jnp.repeat(w_s[l][:, :T, :T].transpose(1, 2, 0), LANES, axis=2),
            "w_pa": w_pa[l].astype(BF16), "w_pb": w_pb[l].astype(BF16), "w_o": w_o[l].astype(BF16),
            "w_up": w_up[l].astype(BF16), "w_down": w_down[l].astype(BF16),
        }
        mod = ada_project(c_all, w_ada[l], b_ada[l])
        mods_p = [a.reshape(Bp, 1, D) for a in jnp.split(mod[:Bp], 6, axis=-1)]
        mods_s = [jnp.repeat(a, T, axis=0).reshape(1, Bs * T, D)
                  for a in jnp.split(mod[Bp:nb], 6, axis=-1)]

        ts_p = min(1024, S)
        y_p, lp, kp = _layer(y_p, mods_p, jnp.arange(S), P, ts_p)
        pos_s = jnp.tile(past + jnp.arange(T), Bs)
        y_s, ls, ks, vs = _layer(y_s, mods_s, pos_s, P, Bs * T,
                                 cache=(cache_kv_latent, cache_k_rope, l, Bs, T))
        for lst, a in zip(outs, (lp, kp, ls.reshape(Bs, T, lat), ks.reshape(Bs, T, n_rope),
                                 vs.reshape(Bs, T, sg_w))):
            lst.append(a)
    return (y_p, y_s.reshape(Bs, T, D)) + tuple(o[0][None] if depth == 1 else jnp.stack(o) for o in outs)
```

```python
import functools
import math

import jax
import jax.numpy as jnp
import numpy as np
from jax import lax
from jax.experimental import pallas as pl
from jax.experimental.pallas import tpu as pltpu

F32 = jnp.float32
BF16 = jnp.bfloat16

EPS = 1e-6
ROPE_BASE = 10000.0
N_HEADS = 16
CHUNK = 64
SG_CHUNK = 128
SG_GROUPS = 16
LANES = 128
HEAD_PAD = 256
LOG2E = math.log2(math.e)
BOUND_MAX = 50.0

VMEM_LIMIT = 58 * 1024 * 1024
VMEM_LIMIT_MLP = 61 * 1024 * 1024


def _params(*sem):
    return pltpu.CompilerParams(dimension_semantics=sem, vmem_limit_bytes=VMEM_LIMIT)


def _rms(x, g):
    ms = jnp.mean(x * x, axis=-1, keepdims=True)
    return x * lax.rsqrt(ms + EPS) * g


def _rope_group(t, tc, ts):
    return t * tc + pltpu.roll(t, LANES - LANES // 4, 1) * ts


def _gelu(z):
    return 0.5 * z * (1.0 + lax.erf(z * np.float32(math.sqrt(0.5))))


def _mod_spec(mod, ts):
    d = mod.shape[-1]
    if mod.shape[1] == 1:
        return pl.BlockSpec((None, 1, d), lambda g, s, *_: (g, 0, 0))
    return pl.BlockSpec((None, ts, d), lambda g, s, *_: (g, s, 0))


def _ada_kernel(c_ref, w_ref, b_ref, o_ref):
    c = c_ref[...]
    s = (c * jax.nn.sigmoid(c)).astype(BF16)
    o_ref[...] = jnp.dot(s, w_ref[...].astype(BF16), preferred_element_type=F32) + b_ref[...]


def ada_project(c, w_ada, b_ada, tn=1024):
    r, d = c.shape
    n = w_ada.shape[1]
    return pl.pallas_call(
        _ada_kernel,
        grid=(n // tn,),
        in_specs=[pl.BlockSpec((r, d), lambda j: (0, 0)),
                  pl.BlockSpec((d, tn), lambda j: (0, j)),
                  pl.BlockSpec((1, tn), lambda j: (0, j))],
        out_specs=pl.BlockSpec((r, tn), lambda j: (0, j)),
        out_shape=jax.ShapeDtypeStruct((r, n), F32),
        compiler_params=_params("arbitrary"),
        name="ada_project",
    )(c, w_ada, b_ada.reshape(1, n))


def _proj_act_kernel(h_ref, w_ref, o_ref, *, act):
    z = jnp.dot(h_ref[...], w_ref[...], preferred_element_type=F32)
    if act == "gelu":
        a = _gelu(z)
    else:
        a = jax.nn.sigmoid(z)
    o_ref[...] = a.astype(o_ref.dtype)


def proj_act(h, w, act, ts, tn):
    G, S, D = h.shape
    n = w.shape[1]
    return pl.pallas_call(
        functools.partial(_proj_act_kernel, act=act),
        grid=(G, S // ts, n // tn),
        in_specs=[pl.BlockSpec((None, ts, D), lambda g, s, j: (g, s, 0)),
                  pl.BlockSpec((D, tn), lambda g, s, j: (0, j))],
        out_specs=pl.BlockSpec((None, ts, tn), lambda g, s, j: (g, s, j)),
        out_shape=jax.ShapeDtypeStruct((G, S, n), BF16),
        compiler_params=_params("arbitrary", "arbitrary", "arbitrary"),
        name="proj_" + act,
    )(h, w)


def _proj_gelu_norm_kernel(h_ref, w_ref, g_ref, o_ref):
    z = jnp.dot(h_ref[...], w_ref[...], preferred_element_type=F32)
    o_ref[...] = _rms(_gelu(z), g_ref[...]).astype(o_ref.dtype)


def proj_gelu_norm(h, w, g, ts, out_dtype):
    G, S, D = h.shape
    n = w.shape[1]
    return pl.pallas_call(
        _proj_gelu_norm_kernel,
        grid=(G, S // ts),
        in_specs=[pl.BlockSpec((None, ts, D), lambda g_, s: (g_, s, 0)),
                  pl.BlockSpec((D, n), lambda g_, s: (0, 0)),
                  pl.BlockSpec((1, n), lambda g_, s: (0, 0))],
        out_specs=pl.BlockSpec((None, ts, n), lambda g_, s: (g_, s, 0)),
        out_shape=jax.ShapeDtypeStruct((G, S, n), out_dtype),
        compiler_params=_params("arbitrary", "arbitrary"),
        name="proj_gelu_norm",
    )(h, w, g.reshape(1, n))


def _q_proj_kernel(x_ref, g1_ref, sc_ref, sh_ref, wq_ref, gqa_ref, wuq_ref, gn_ref, gr_ref, tc_ref, ts_ref,
                   h_ref, q_ref, *, q_scale):
    h = (_rms(x_ref[...], g1_ref[...]) * (1.0 + sc_ref[...]) + sh_ref[...]).astype(h_ref.dtype)
    h_ref[...] = h
    zq = jnp.dot(h, wq_ref[...], preferred_element_type=F32)
    zn = _rms(zq, gqa_ref[...]).astype(BF16)
    tc, ts = tc_ref[...], ts_ref[...]
    gains = jnp.concatenate([gn_ref[...], gr_ref[...]], axis=1) * q_scale
    same_tile = (lax.broadcasted_iota(jnp.int32, (HEAD_PAD, HEAD_PAD), 0) // LANES
                 == lax.broadcasted_iota(jnp.int32, (HEAD_PAD, HEAD_PAD), 1) // LANES)
    tile_mean = jnp.where(same_tile, 1.0 / LANES, 0.0).astype(BF16)
    shift_lane = jnp.where(lax.broadcasted_iota(jnp.int32, (1, LANES), 1) == LANES // 2, 1.0, 0.0)
    for pair in range(q_ref.shape[0] // 2):
        blk2 = jnp.dot(zn, wuq_ref[:, pair * 2 * HEAD_PAD:(pair + 1) * 2 * HEAD_PAD],
                       preferred_element_type=F32)
        for sub in range(2):
            hh = 2 * pair + sub
            blk = blk2[:, sub * HEAD_PAD:(sub + 1) * HEAD_PAD]
            ms = jnp.dot((blk * blk).astype(BF16), tile_mean, preferred_element_type=F32)
            y = blk * lax.rsqrt(ms + EPS) * gains
            q_ref[hh, :, :LANES] = y[:, :LANES].astype(q_ref.dtype)
            q_ref[hh, :, LANES:] = (_rope_group(y[:, LANES:], tc, ts) + shift_lane).astype(q_ref.dtype)


def prenorm_q_project(x, g_norm, sc, sh, w_q, g_q_a, w_uq_pad, g_nope, g_rope_pad, tabs, ts, q_scale):
    G, S, D = x.shape
    ql = w_q.shape[1]
    H = w_uq_pad.shape[1] // HEAD_PAD
    const = lambda g, s: (0, 0)
    tab_spec = pl.BlockSpec((ts, LANES), lambda g, s: (s, 0))
    row = pl.BlockSpec((None, ts, D), lambda g, s: (g, s, 0))
    return pl.pallas_call(
        functools.partial(_q_proj_kernel, q_scale=q_scale),
        grid=(G, S // ts),
        in_specs=[row, pl.BlockSpec((1, D), const), _mod_spec(sc, ts), _mod_spec(sh, ts),
                  pl.BlockSpec((D, ql), const),
                  pl.BlockSpec((1, ql), const),
                  pl.BlockSpec((ql, H * HEAD_PAD), const),
                  pl.BlockSpec((1, LANES), const),
                  pl.BlockSpec((1, LANES), const),
                  tab_spec, tab_spec],
        out_specs=[row, pl.BlockSpec((None, H, ts, HEAD_PAD), lambda g, s: (g, 0, s, 0))],
        out_shape=[jax.ShapeDtypeStruct((G, S, D), BF16),
                   jax.ShapeDtypeStruct((G, H, S, HEAD_PAD), BF16)],
        compiler_params=_params("arbitrary", "arbitrary"),
        name="prenorm_q_project",
    )(x, g_norm.reshape(1, D), sc, sh, w_q, g_q_a.reshape(1, ql), w_uq_pad, g_nope.reshape(1, LANES),
      g_rope_pad, *tabs)


def _kv_proj_kernel(h_ref, wkv_ref, gkva_ref, gkr_ref, tc_ref, ts_ref, *rest, n_rope, expand):
    if expand:
        wuk_ref, wuv_ref, gkn_ref, kshift_ref, c_ref, kr_ref, k_ref, v_ref = rest
    else:
        c_ref, kr_ref, krp_ref = rest
    lat = c_ref.shape[-1]
    z = jnp.dot(h_ref[...], wkv_ref[...], preferred_element_type=F32)
    c = _rms(z[:, :lat], gkva_ref[...])
    c_ref[...] = c
    rope = _rope_group(_rms(z[:, lat:], gkr_ref[...]), tc_ref[...], ts_ref[...])
    kr_ref[...] = rope[:, :n_rope]
    rope_b = rope.astype(BF16)
    if not expand:
        krp_ref[...] = rope_b
        return
    cb = c.astype(BF16)
    gkn = gkn_ref[...]
    rope_b = (rope + kshift_ref[...]).astype(BF16)
    for pair in range(k_ref.shape[0] // 2):
        cols = slice(pair * 2 * LANES, (pair + 1) * 2 * LANES)
        kn2 = jnp.dot(cb, wuk_ref[:, cols], preferred_element_type=F32)
        v2 = jnp.dot(cb, wuv_ref[:, cols], preferred_element_type=F32)
        for sub in range(2):
            hh = 2 * pair + sub
            lanes = slice(sub * LANES, (sub + 1) * LANES)
            k_ref[hh, :, :LANES] = _rms(kn2[:, lanes], gkn).astype(k_ref.dtype)
            k_ref[hh, :, LANES:] = rope_b
            v_ref[hh] = v2[:, lanes].astype(v_ref.dtype)


def kv_project(h, w_kv, g_kv_a, g_k_rope_pad, tabs, ts, n_rope, expand_weights=None):
    G, S, D = h.shape
    lat = g_kv_a.shape[-1]
    const = lambda g, s: (0, 0)
    tab_spec = pl.BlockSpec((ts, LANES), lambda g, s: (s, 0))
    in_specs = [pl.BlockSpec((None, ts, D), lambda g, s: (g, s, 0)),
                pl.BlockSpec((D, lat + LANES), const),
                pl.BlockSpec((1, lat), const),
                pl.BlockSpec((1, LANES), const),
                tab_spec, tab_spec]
    args = [h, w_kv, g_kv_a.reshape(1, lat), g_k_rope_pad, *tabs]
    out_specs = [pl.BlockSpec((None, ts, lat), lambda g, s: (g, s, 0)),
                 pl.BlockSpec((None, ts, n_rope), lambda g, s: (g, s, 0))]
    out_shape = [jax.ShapeDtypeStruct((G, S, lat), F32),
                 jax.ShapeDtypeStruct((G, S, n_rope), F32)]
    expand = expand_weights is not None
    if expand:
        w_uk, w_uv, g_k_nope, k_shift = expand_weights
        H = w_uk.shape[1] // LANES
        in_specs += [pl.BlockSpec(w_uk.shape, const), pl.BlockSpec(w_uv.shape, const),
                     pl.BlockSpec((1, LANES), const), pl.BlockSpec((1, LANES), const)]
        args += [w_uk, w_uv, g_k_nope.reshape(1, LANES), k_shift]
        out_specs += [pl.BlockSpec((None, H, ts, HEAD_PAD), lambda g, s: (g, 0, s, 0)),
                      pl.BlockSpec((None, H, ts, LANES), lambda g, s: (g, 0, s, 0))]
        out_shape += [jax.ShapeDtypeStruct((G, H, S, HEAD_PAD), BF16),
                      jax.ShapeDtypeStruct((G, H, S, LANES), BF16)]
    else:
        out_specs.append(pl.BlockSpec((None, ts, LANES), lambda g, s: (g, s, 0)))
        out_shape.append(jax.ShapeDtypeStruct((G, S, LANES), BF16))
    return pl.pallas_call(
        functools.partial(_kv_proj_kernel, n_rope=n_rope, expand=expand),
        grid=(G, S // ts),
        in_specs=in_specs,
        out_specs=out_specs,
        out_shape=out_shape,
        compiler_params=_params("arbitrary", "arbitrary"),
        name="kv_project",
    )(*args)


def _gelu_gate_kernel(h_ref, wu_ref, v_ref, w_ref, b_ref, o_ref, wm_sc, u_sc, *, n_chunks):
    u_sc[...] = _gelu(jnp.dot(h_ref[...], wu_ref[...], preferred_element_type=F32)).astype(u_sc.dtype)
    row = lax.broadcasted_iota(jnp.int32, (SG_CHUNK, SG_CHUNK), 0)
    col = lax.broadcasted_iota(jnp.int32, (SG_CHUNK, SG_CHUNK), 1)
    tril = col <= row
    n_groups = w_ref.shape[0]
    for g in range(n_groups):
        wm_sc[g] = jnp.where(tril, w_ref[g], 0.0).astype(BF16)

    def chunk_body(n, carry):
        rows = pl.ds(pl.multiple_of(n * SG_CHUNK, SG_CHUNK), SG_CHUNK)
        for g in range(n_groups):
            lanes = slice(g * LANES, (g + 1) * LANES)
            mix = jnp.dot(wm_sc[g], v_ref[rows, lanes], preferred_element_type=F32) + b_ref[:, lanes]
            o_ref[rows, lanes] = (u_sc[rows, lanes].astype(F32) * mix).astype(o_ref.dtype)
        return carry

    lax.fori_loop(0, n_chunks, chunk_body, 0)


def gelu_spatial_gate(h, w_u, v, w_s, bias_rows, ts):
    G, S, D = h.shape
    W = w_u.shape[1]
    blk = pl.BlockSpec((None, ts, W), lambda g, s: (g, s, 0))
    return pl.pallas_call(
        functools.partial(_gelu_gate_kernel, n_chunks=ts // SG_CHUNK),
        grid=(G, S // ts),
        in_specs=[pl.BlockSpec((None, ts, D), lambda g, s: (g, s, 0)),
                  pl.BlockSpec(w_u.shape, lambda g, s: (0, 0)),
                  blk,
                  pl.BlockSpec(w_s.shape, lambda g, s: (0, 0, 0)),
                  pl.BlockSpec(bias_rows.shape, lambda g, s: (0, 0))],
        out_specs=blk,
        out_shape=jax.ShapeDtypeStruct((G, S, W), BF16),
        scratch_shapes=[pltpu.VMEM(w_s.shape, BF16), pltpu.VMEM((ts, W), BF16)],
        compiler_params=_params("arbitrary", "arbitrary"),
        name="gelu_spatial_gate",
    )(h, w_u, v, w_s, bias_rows)


def _spatial_gate_open_kernel(u_ref, v_ref, wl_ref, b_ref, o_ref):
    T = u_ref.shape[1]
    for i in range(T):
        acc = b_ref[i:i + 1, :] + wl_ref[i, 0:1, :] * v_ref[:, 0, :]
        for j in range(1, i + 1):
            acc = acc + wl_ref[i, j:j + 1, :] * v_ref[:, j, :]
        o_ref[:, i, :] = (u_ref[:, i, :].astype(F32) * acc).astype(o_ref.dtype)


def spatial_gate_open(u, v, w_lanes, bias_rows):
    B, T, W = u.shape
    full = lambda a: pl.BlockSpec(a.shape, lambda i: (0,) * a.ndim)
    return pl.pallas_call(
        _spatial_gate_open_kernel,
        grid=(1,),
        in_specs=[full(u), full(v), full(w_lanes), full(bias_rows)],
        out_specs=pl.BlockSpec((B, T, W), lambda i: (0, 0, 0)),
        out_shape=jax.ShapeDtypeStruct((B, T, W), BF16),
        compiler_params=_params("arbitrary"),
        name="spatial_gate_open",
    )(u, v, w_lanes, bias_rows)


def _attn_kernel(bounded_ref, q_ref, k_ref, v_ref, o_ref, m_sc, acc_sc, *, tq, tk, td):
    i = pl.program_id(2)
    dv = v_ref.shape[-1]
    unroll = tq // tk
    nt = (((1,), (1,)), ((), ()))
    ones = jnp.ones((tk, LANES), BF16)
    d0 = pl.multiple_of(i * tq, tq)
    chunk_mask = (lax.broadcasted_iota(jnp.int32, (td, td), 1) // CHUNK
                  <= lax.broadcasted_iota(jnp.int32, (td, td), 0) // CHUNK)

    def scores(rows0, k0, keys):
        return lax.dot_general(q_ref[rows0:, :], k_ref[pl.ds(k0, keys), :], nt, preferred_element_type=F32)

    def diag_scores(c):
        r0 = c * td
        s = scores(r0, d0 + r0, td)
        top = jnp.where(chunk_mask, s[:td], -jnp.inf)
        return top if r0 + td == tq else jnp.concatenate([top, s[td:]], axis=0)

    def values(k0, keys):
        return jnp.concatenate([v_ref[pl.ds(k0, keys), :], ones[:keys]], axis=1)

    def finish():
        acc = acc_sc[...]
        o_ref[...] = (acc[:, :dv] / acc[:, dv:]).astype(o_ref.dtype)

    @pl.when(bounded_ref[0] == 1)
    def _():
        for c in range(tq // td):
            r0 = c * td
            pv = jnp.dot(jnp.exp2(diag_scores(c)).astype(BF16), values(d0 + r0, td),
                         preferred_element_type=F32)
            if c == 0:
                acc_sc[...] = pv
            else:
                acc_sc[r0:, :] += pv

        def sweep(k0, n_tiles):
            acc = acc_sc[...]
            for u in range(n_tiles):
                p = jnp.exp2(scores(0, k0 + u * tk, tk)).astype(BF16)
                acc = acc + jnp.dot(p, values(k0 + u * tk, tk), preferred_element_type=F32)
            acc_sc[...] = acc

        def body(j, carry):
            sweep(pl.multiple_of(j * (2 * tq), 2 * tq), 2 * unroll)
            return carry

        lax.fori_loop(0, i // 2, body, 0)

        @pl.when(i % 2 == 1)
        def _():
            sweep(pl.multiple_of((i - 1) * tq, tq), unroll)

        finish()

    @pl.when(bounded_ref[0] != 1)
    def _():
        def step(s, k0, m_old, acc_old):
            rows, keys = s.shape
            row_max = jnp.max(s, axis=-1, keepdims=True)
            m_new = (jnp.broadcast_to(row_max, (rows, LANES)) if m_old is None
                     else jnp.maximum(m_old, row_max))
            p = jnp.concatenate([jnp.exp2(s[:, c * LANES:(c + 1) * LANES] - m_new)
                                 for c in range(keys // LANES)], axis=1).astype(BF16)
            pv = jnp.dot(p, values(k0, keys), preferred_element_type=F32)
            if m_old is None:
                return m_new, pv
            alpha = jnp.exp2(m_old - m_new)
            return m_new, jnp.concatenate([alpha] * ((dv + LANES) // LANES), axis=1) * acc_old + pv

        for c in range(tq // td):
            r0 = c * td
            if c == 0:
                m, acc = step(diag_scores(c), d0, None, None)
            else:
                m, acc = step(diag_scores(c), d0 + r0, m_sc[r0:, :], acc_sc[r0:, :])
            m_sc[r0:, :] = m
            acc_sc[r0:, :] = acc

        def body(j, carry):
            k0 = pl.multiple_of(j * tq, tq)
            ss = [scores(0, k0 + u * tk, tk) for u in range(unroll)]
            m, acc = m_sc[...], acc_sc[...]
            for u in range(unroll):
                m, acc = step(ss[u], k0 + u * tk, m, acc)
            m_sc[...] = m
            acc_sc[...] = acc
            return carry

        lax.fori_loop(0, i, body, 0)
        finish()


def attention_prompt(bounded, q, k, v, tq, tk, td):
    G, H, S, _ = q.shape
    dv = v.shape[-1]
    assert dv == LANES and tq % tk == 0 and tq % td == 0 and td % LANES == 0 and td <= tk
    return pl.pallas_call(
        functools.partial(_attn_kernel, tq=tq, tk=tk, td=td),
        grid=(G, H, S // tq),
        in_specs=[pl.BlockSpec(memory_space=pltpu.SMEM),
                  pl.BlockSpec((None, None, tq, HEAD_PAD), lambda g, h, i: (g, h, i, 0)),
                  pl.BlockSpec((None, None, S, HEAD_PAD), lambda g, h, i: (g, h, 0, 0)),
                  pl.BlockSpec((None, None, S, dv), lambda g, h, i: (g, h, 0, 0))],
        out_specs=pl.BlockSpec((None, tq, dv), lambda g, h, i: (g, i, h)),
        out_shape=jax.ShapeDtypeStruct((G, S, H * dv), BF16),
        scratch_shapes=[pltpu.VMEM((tq, LANES), F32), pltpu.VMEM((tq, dv + LANES), F32)],
        compiler_params=_params("arbitrary", "arbitrary", "arbitrary"),
        name="attention_prompt",
    )(bounded, q, k, v)


def _attn_cached_kernel(q_ref, cache_ref, ckr_ref, cnew_ref, krnew_ref, wukt_ref, wuv_ref, gkn_ref,
                        o_ref, call_sc, krall_sc, p_sc, *, past, n_new):
    H = q_ref.shape[0]
    L = past + n_new
    Lp = call_sc.shape[0]
    lat = call_sc.shape[1]
    call_sc[0:past, :] = cache_ref[...].astype(BF16)
    call_sc[past:L, :] = cnew_ref[...].astype(BF16)
    call_sc[L:Lp, :] = jnp.zeros((Lp - L, lat), BF16)
    n_rope = ckr_ref.shape[-1]
    krall_sc[0:past, n_rope:] = jnp.zeros((past, LANES - n_rope), BF16)
    krall_sc[0:past, :n_rope] = ckr_ref[...].astype(BF16)
    krall_sc[past:L, :] = krnew_ref[...]
    krall_sc[L:Lp, :] = jnp.zeros((Lp - L, LANES), BF16)
    call = call_sc[...]
    gkn = gkn_ref[...]
    valid = lax.broadcasted_iota(jnp.int32, (n_new, Lp), 1) < L
    nt = (((1,), (1,)), ((), ()))
    q_rope = jnp.concatenate([q_ref[hh][:, LANES:] for hh in range(H)], axis=0)
    s_rope = lax.dot_general(q_rope, krall_sc[...], nt, preferred_element_type=F32)
    group = 4
    for grp in range(H // group):
        kn_t = lax.dot_general(wukt_ref[grp * group * LANES:(grp + 1) * group * LANES, :], call, nt,
                               preferred_element_type=F32)
        for sub in range(group):
            hh = group * grp + sub
            blk = kn_t[sub * LANES:(sub + 1) * LANES, :]
            inv_rms = lax.rsqrt(jnp.mean(blk * blk, axis=0, keepdims=True) + EPS)
            q_nope = (q_ref[hh][:, :LANES].astype(F32) * gkn).astype(BF16)
            s = jnp.dot(q_nope, blk.astype(BF16), preferred_element_type=F32) * inv_rms
            s = s + s_rope[hh * n_new:(hh + 1) * n_new, :]
            s = jnp.where(valid, s, -jnp.inf)
            m = jnp.max(s, axis=-1, keepdims=True)
            p = jnp.exp2(s - m)
            p = p / jnp.sum(p, axis=-1, keepdims=True)
            p_sc[hh * n_new:(hh + 1) * n_new, :] = p.astype(BF16)
    o_lat = jnp.dot(p_sc[...], call, preferred_element_type=F32).astype(BF16)
    for hh in range(H):
        o_ref[:, hh * LANES:(hh + 1) * LANES] = jnp.dot(
            o_lat[hh * n_new:(hh + 1) * n_new, :], wuv_ref[:, hh * LANES:(hh + 1) * LANES],
            preferred_element_type=F32).astype(o_ref.dtype)


def attention_cached(q, cache_lat, cache_kr, layer, c_new, kr_new_pad, w_uk_t, w_uv, g_k_nope, n_new):
    _, H, BT, _ = q.shape
    _, B, past, lat = cache_lat.shape
    n_rope = cache_kr.shape[-1]
    L = past + n_new
    Lp = -(-L // LANES) * LANES
    const = lambda b: (0, 0)
    return pl.pallas_call(
        functools.partial(_attn_cached_kernel, past=past, n_new=n_new),
        grid=(B,),
        in_specs=[pl.BlockSpec((None, H, n_new, HEAD_PAD), lambda b: (0, 0, b, 0)),
                  pl.BlockSpec((None, None, past, lat), lambda b: (layer, b, 0, 0)),
                  pl.BlockSpec((None, None, past, n_rope), lambda b: (layer, b, 0, 0)),
                  pl.BlockSpec((None, n_new, lat), lambda b: (0, b, 0)),
                  pl.BlockSpec((None, n_new, LANES), lambda b: (0, b, 0)),
                  pl.BlockSpec(w_uk_t.shape, const), pl.BlockSpec(w_uv.shape, const),
                  pl.BlockSpec((1, LANES), const)],
        out_specs=pl.BlockSpec((None, n_new, H * LANES), lambda b: (0, b, 0)),
        out_shape=jax.ShapeDtypeStruct((1, BT, H * LANES), BF16),
        scratch_shapes=[pltpu.VMEM((Lp, lat), BF16), pltpu.VMEM((Lp, LANES), BF16),
                        pltpu.VMEM((H * n_new, Lp), BF16)],
        compiler_params=_params("arbitrary"),
        name="attention_cached",
    )(q, cache_lat, cache_kr, c_new, kr_new_pad, w_uk_t, w_uv, g_k_nope.reshape(1, LANES))


def _merge_kernel(a_ref, b_ref, ga_ref, gb_ref, wpa_ref, wpb_ref, o_ref):
    pa = jnp.dot(a_ref[...], wpa_ref[...], preferred_element_type=F32)
    pb = jnp.dot(b_ref[...], wpb_ref[...], preferred_element_type=F32)
    o_ref[...] = (ga_ref[...].astype(F32) * pa + gb_ref[...].astype(F32) * pb).astype(o_ref.dtype)


def merge_branches(o_sg, o_mla, gates, w_pa, w_pb, ts, tn):
    G, S, W = o_sg.shape
    D = w_pa.shape[1]
    nj = D // tn
    row = lambda g, s, j: (g, s, 0)
    return pl.pallas_call(
        _merge_kernel,
        grid=(G, S // ts, nj),
        in_specs=[pl.BlockSpec((None, ts, W), row),
                  pl.BlockSpec((None, ts, o_mla.shape[-1]), row),
                  pl.BlockSpec((None, ts, tn), lambda g, s, j: (g, s, j)),
                  pl.BlockSpec((None, ts, tn), lambda g, s, j: (g, s, j + nj)),
                  pl.BlockSpec((W, tn), lambda g, s, j: (0, j)),
                  pl.BlockSpec((o_mla.shape[-1], tn), lambda g, s, j: (0, j))],
        out_specs=pl.BlockSpec((None, ts, tn), lambda g, s, j: (g, s, j)),
        out_shape=jax.ShapeDtypeStruct((G, S, D), BF16),
        compiler_params=_params("arbitrary", "arbitrary", "arbitrary"),
        name="merge_branches",
    )(o_sg, o_mla, gates, gates, w_pa, w_pb)


def _out_proj_kernel(m_ref, wo_ref, x_ref, g1_ref, gn_ref, sc_ref, sh_ref, x1_ref, h2_ref):
    y = jnp.dot(m_ref[...], wo_ref[...], preferred_element_type=F32)
    x1 = x_ref[...] + g1_ref[...] * y
    x1_ref[...] = x1
    h2_ref[...] = (_rms(x1, gn_ref[...]) * (1.0 + sc_ref[...]) + sh_ref[...]).astype(h2_ref.dtype)


def out_project(m, w_o, x, g1, g_norm2, sc2, sh2, ts):
    G, S, D = x.shape
    row = pl.BlockSpec((None, ts, D), lambda g, s: (g, s, 0))
    return pl.pallas_call(
        _out_proj_kernel,
        grid=(G, S // ts),
        in_specs=[row, pl.BlockSpec(w_o.shape, lambda g, s: (0, 0)), row,
                  _mod_spec(g1, ts), pl.BlockSpec((1, D), lambda g, s: (0, 0)),
                  _mod_spec(sc2, ts), _mod_spec(sh2, ts)],
        out_specs=[row, row],
        out_shape=[jax.ShapeDtypeStruct((G, S, D), F32), jax.ShapeDtypeStruct((G, S, D), BF16)],
        compiler_params=_params("arbitrary", "arbitrary"),
        name="out_project",
    )(m, w_o, x, g1, g_norm2.reshape(1, D), sc2, sh2)


def _mlp_kernel(h_ref, wup_ref, wdn_ref, x1_ref, g2_ref, o_ref, part_sc, *, n_steps):
    j = pl.program_id(2)

    def chunk():
        hid = jnp.dot(h_ref[...], wup_ref[...], preferred_element_type=F32)
        hid = jnp.square(jnp.maximum(hid, 0.0)).astype(BF16)
        return jnp.dot(hid, wdn_ref[...], preferred_element_type=F32)

    def residual(total):
        o_ref[...] = x1_ref[...] + g2_ref[...] * total

    if n_steps == 1:
        residual(chunk())
        return
    last = n_steps - 1

    @pl.when(j == 0)
    def _():
        part_sc[...] = chunk()

    if n_steps > 2:
        @pl.when(j == 1)
        def _():
            o_ref[...] = part_sc[...]
            part_sc[...] = chunk()

        @pl.when(jnp.logical_and(j > 1, j < last))
        def _():
            o_ref[...] += part_sc[...]
            part_sc[...] = chunk()

    @pl.when(j == last)
    def _():
        prev = part_sc[...] if n_steps == 2 else o_ref[...] + part_sc[...]
        residual(prev + chunk())


def mlp_residual(h2, w_up, w_down, x1, g2, ts, th):
    G, S, D = x1.shape
    hidden = w_up.shape[1]
    row = pl.BlockSpec((None, ts, D), lambda g, s, j: (g, s, 0))
    return pl.pallas_call(
        functools.partial(_mlp_kernel, n_steps=hidden // th),
        grid=(G, S // ts, hidden // th),
        in_specs=[row,
                  pl.BlockSpec((D, th), lambda g, s, j: (0, j)),
                  pl.BlockSpec((th, D), lambda g, s, j: (j, 0)),
                  row, _mod_spec(g2, ts)],
        out_specs=row,
        out_shape=jax.ShapeDtypeStruct((G, S, D), F32),
        scratch_shapes=[pltpu.VMEM((ts, D), F32)],
        compiler_params=pltpu.CompilerParams(dimension_semantics=("arbitrary",) * 3,
                                             vmem_limit_bytes=VMEM_LIMIT_MLP),
        name="mlp_residual",
    )(h2, w_up, w_down, x1, g2)


def _rope_tables(pos, n_rope):
    inv = jnp.float32(ROPE_BASE) ** (-jnp.arange(0, n_rope, 2, dtype=F32) / n_rope)
    ang = pos.astype(F32)[:, None] * inv[None, :]
    cos, sin = lax.optimization_barrier((jnp.cos(ang), jnp.sin(ang)))
    pad = jnp.zeros((pos.shape[0], LANES - n_rope), F32)
    return (jnp.concatenate([cos, cos, pad], axis=1), jnp.concatenate([-sin, sin, pad], axis=1))


def _dup_lanes(g):
    return jnp.concatenate([g.astype(F32), g.astype(F32)]).reshape(1, 2 * g.shape[0])


def _layer(x, mods, pos, P, ts, *, cache=None):
    sh1, sc1, g1, sh2, sc2, g2 = mods
    n_rope = P["n_rope"]
    tabs = _rope_tables(pos, n_rope)
    h, q = prenorm_q_project(x, P["g_norm1"], sc1, sh1, P["w_q"], P["g_q_a"], P["w_uq"], P["g_q_nope"],
                             P["g_q_rope"], tabs, ts, P["q_scale"])
    gates = proj_act(h, P["w_g"], "sigmoid", ts, P["w_g"].shape[1] // 2)
    if cache is None:
        v = proj_gelu_norm(h, P["w_v"], P["g_sg"], ts, BF16)
        o_sg = gelu_spatial_gate(h, P["w_u"], v, P["w_s"], P["b_rows"], ts)
        c_kv, k_rope, k, vv = kv_project(h, P["w_kv"], P["g_kv_a"], P["g_k_rope"], tabs, ts, n_rope,
                                         expand_weights=(P["w_uk"], P["w_uv"], P["g_k_nope"], P["k_shift"]))
        o_mla = attention_prompt(P["bounded"], q, k, vv, tq=min(2048, x.shape[1]),
                                 tk=min(1024, x.shape[1]), td=min(512, x.shape[1]))
        extra = ()
    else:
        cache_lat, cache_kr, layer, B, T = cache
        u = proj_act(h, P["w_u"], "gelu", ts, P["w_u"].shape[1])
        v = proj_gelu_norm(h, P["w_v"], P["g_sg"], ts, F32)
        W = v.shape[-1]
        o_sg = spatial_gate_open(u.reshape(B, T, W), v.reshape(B, T, W), P["w_lanes"][:T, :T],
                                 P["b_rows"][:T]).reshape(1, B * T, W)
        c_kv, k_rope, kr_pad = kv_project(h, P["w_kv"], P["g_kv_a"], P["g_k_rope"], tabs, ts, n_rope)
        o_mla = attention_cached(q, cache_lat, cache_kr, layer, c_kv, kr_pad, P["w_uk_t"], P["w_uv"],
                                 P["g_k_nope"], T)
        extra = (v,)
    m = merge_branches(o_sg, o_mla, gates, P["w_pa"], P["w_pb"], ts, P["w_pa"].shape[1] // 2)
    ts_res = min(ts, 512)
    x1, h2 = out_project(m, P["w_o"], x, g1, P["g_norm2"], sc2, sh2, ts_res)
    y = mlp_residual(h2, P["w_up"], P["w_down"], x1, g2, ts_res, min(2048, P["w_up"].shape[1]))
    return (y, c_kv, k_rope) + extra


def kernel(x_prompt, x_sample, cache_kv_latent, cache_k_rope, c_prompt, c_sample, w_ada, b_ada, g_norm1, g_norm2, w_in, g_sg, w_s, b_s, g_q_a, w_uq, g_q_nope, g_q_rope, g_kv_a, g_k_rope, w_uk, g_k_nope, w_uv, w_pa, w_pb, w_o, w_up, w_down):
    depth = w_in.shape[0]
    Bp, S, D = x_prompt.shape
    Bs, T, _ = x_sample.shape
    past = cache_kv_latent.shape[2]
    q_lora = g_q_a.shape[-1]
    lat = g_kv_a.shape[-1]
    n_nope = g_q_nope.shape[-1]
    n_rope = g_q_rope.shape[-1]
    H = w_uk.shape[2]
    sg_w = g_sg.shape[-1]
    off_q = 2 * sg_w
    off_kv = off_q + q_lora
    off_gate = off_kv + lat + n_rope
    assert n_nope == LANES and w_uv.shape[-1] == LANES and 2 * n_rope == LANES
    assert sg_w // SG_GROUPS == LANES and H == N_HEADS

    nb = Bp + Bs
    nb_pad = -(-nb // 8) * 8
    c_all = jnp.concatenate([c_prompt, c_sample, jnp.zeros((nb_pad - nb, D), F32)], axis=0)

    y_p, y_s = x_prompt, x_sample.reshape(1, Bs * T, D)
    outs = [[] for _ in range(5)]
    for l in range(depth):
        def w_in_cols(a, b):
            return w_in[l][:, a:b].astype(BF16)
        w_uq_l = w_uq[l].astype(BF16).reshape(q_lora, H, n_nope + n_rope)
        w_uq_pad = jnp.concatenate([w_uq_l, w_uq_l[:, :, n_nope:]], axis=2)
        q_scale = float((n_nope + n_rope) ** -0.5 * LOG2E)
        def sq_norm_bound(g_nope, g_rope):
            return n_nope * jnp.max(jnp.square(g_nope)) + n_rope * jnp.max(jnp.square(g_rope))
        score_bound = q_scale * jnp.sqrt(sq_norm_bound(g_q_nope[l], g_q_rope[l])
                                         * sq_norm_bound(g_k_nope[l], g_k_rope[l]))
        bounded = (score_bound <= BOUND_MAX).astype(jnp.int32).reshape(1)
        k_shift = jnp.where(jnp.arange(LANES) == LANES // 2, -score_bound, 0.0).astype(F32).reshape(1, LANES)
        P = {
            "n_rope": n_rope,
            "q_scale": q_scale, "k_shift": k_shift, "bounded": bounded,
            "g_norm1": g_norm1[l], "g_norm2": g_norm2[l], "g_sg": g_sg[l],
            "w_u": w_in_cols(0, sg_w), "w_v": w_in_cols(sg_w, off_q), "w_q": w_in_cols(off_q, off_kv),
            "w_kv": jnp.concatenate([w_in_cols(off_kv, off_gate), w_in_cols(off_kv + lat, off_gate)], axis=1),
            "w_g": w_in_cols(off_gate, w_in.shape[2]),
            "g_q_a": g_q_a[l], "w_uq": w_uq_pad.reshape(q_lora, H * HEAD_PAD),
            "g_q_nope": g_q_nope[l], "g_q_rope": _dup_lanes(g_q_rope[l]),
            "g_kv_a": g_kv_a[l], "g_k_rope": _dup_lanes(g_k_rope[l]),
            "w_uk": w_uk[l].astype(BF16).reshape(lat, H * n_nope),
            "w_uk_t": w_uk[l].astype(BF16).reshape(lat, H * n_nope).T,
            "w_uv": w_uv[l].astype(BF16).reshape(lat, H * LANES),
            "g_k_nope": g_k_nope[l],
            "w_s": w_s[l],
            "b_rows": jnp.repeat(b_s[l].T, LANES, axis=1),
            "w_lanes": jnp.repeat(w_s[l][:, :T, :T].transpose(1, 2, 0), LANES, axis=2),
            "w_pa": w_pa[l].astype(BF16), "w_pb": w_pb[l].astype(BF16), "w_o": w_o[l].astype(BF16),
            "w_up": w_up[l].astype(BF16), "w_down": w_down[l].astype(BF16),
        }
        mod = ada_project(c_all, w_ada[l], b_ada[l])
        mods_p = [a.reshape(Bp, 1, D) for a in jnp.split(mod[:Bp], 6, axis=-1)]
        mods_s = [jnp.repeat(a, T, axis=0).reshape(1, Bs * T, D)
                  for a in jnp.split(mod[Bp:nb], 6, axis=-1)]

        ts_p = min(1024, S)
        y_p, lp, kp = _layer(y_p, mods_p, jnp.arange(S), P, ts_p)
        pos_s = jnp.tile(past + jnp.arange(T), Bs)
        y_s, ls, ks, vs = _layer(y_s, mods_s, pos_s, P, Bs * T,
                                 cache=(cache_kv_latent, cache_k_rope, l, Bs, T))
        for lst, a in zip(outs, (lp, kp, ls.reshape(Bs, T, lat), ks.reshape(Bs, T, n_rope),
                                 vs.reshape(Bs, T, sg_w))):
            lst.append(a)
    return (y_p, y_s.reshape(Bs, T, D)) + tuple(o[0][None] if depth == 1 else jnp.stack(o) for o in outs)
```

```python
import functools
import math

import jax
import jax.numpy as jnp
import numpy as np
from jax import lax
from jax.experimental import pallas as pl
from jax.experimental.pallas import tpu as pltpu

F32 = jnp.float32
BF16 = jnp.bfloat16

EPS = 1e-6
ROPE_BASE = 10000.0
N_HEADS = 16
CHUNK = 64
SG_CHUNK = 128
SG_GROUPS = 16
LANES = 128
HEAD_PAD = 256
LOG2E = math.log2(math.e)
BOUND_MAX = 50.0

VMEM_LIMIT = 58 * 1024 * 1024
VMEM_LIMIT_MLP = 61 * 1024 * 1024


def _params(*sem):
    return pltpu.CompilerParams(dimension_semantics=sem, vmem_limit_bytes=VMEM_LIMIT)


def _rms(x, g):
    ms = jnp.mean(x * x, axis=-1, keepdims=True)
    return x * lax.rsqrt(ms + EPS) * g


def _rope_group(t, tc, ts):
    return t * tc + pltpu.roll(t, LANES - LANES // 4, 1) * ts


def _gelu(z):
    return 0.5 * z * (1.0 + lax.erf(z * np.float32(math.sqrt(0.5))))


def _mod_spec(mod, ts):
    d = mod.shape[-1]
    if mod.shape[1] == 1:
        return pl.BlockSpec((None, 1, d), lambda g, s, *_: (g, 0, 0))
    return pl.BlockSpec((None, ts, d), lambda g, s, *_: (g, s, 0))


def _ada_kernel(c_ref, w_ref, b_ref, o_ref):
    c = c_ref[...]
    s = (c * jax.nn.sigmoid(c)).astype(BF16)
    o_ref[...] = jnp.dot(s, w_ref[...].astype(BF16), preferred_element_type=F32) + b_ref[...]


def ada_project(c, w_ada, b_ada, tn=1024):
    r, d = c.shape
    n = w_ada.shape[1]
    return pl.pallas_call(
        _ada_kernel,
        grid=(n // tn,),
        in_specs=[pl.BlockSpec((r, d), lambda j: (0, 0)),
                  pl.BlockSpec((d, tn), lambda j: (0, j)),
                  pl.BlockSpec((1, tn), lambda j: (0, j))],
        out_specs=pl.BlockSpec((r, tn), lambda j: (0, j)),
        out_shape=jax.ShapeDtypeStruct((r, n), F32),
        compiler_params=_params("arbitrary"),
        name="ada_project",
    )(c, w_ada, b_ada.reshape(1, n))


def _proj_act_kernel(h_ref, w_ref, o_ref, *, act):
    z = jnp.dot(h_ref[...], w_ref[...], preferred_element_type=F32)
    if act == "gelu":
        a = _gelu(z)
    else:
        a = 0.5 * (1.0 + jnp.tanh(0.5 * z))
    o_ref[...] = a.astype(o_ref.dtype)


def proj_act(h, w, act, ts, tn):
    G, S, D = h.shape
    n = w.shape[1]
    return pl.pallas_call(
        functools.partial(_proj_act_kernel, act=act),
        grid=(G, S // ts, n // tn),
        in_specs=[pl.BlockSpec((None, ts, D), lambda g, s, j: (g, s, 0)),
                  pl.BlockSpec((D, tn), lambda g, s, j: (0, j))],
        out_specs=pl.BlockSpec((None, ts, tn), lambda g, s, j: (g, s, j)),
        out_shape=jax.ShapeDtypeStruct((G, S, n), BF16),
        compiler_params=_params("arbitrary", "arbitrary", "arbitrary"),
        name="proj_" + act,
    )(h, w)


def _proj_gelu_norm_kernel(h_ref, w_ref, g_ref, o_ref):
    z = jnp.dot(h_ref[...], w_ref[...], preferred_element_type=F32)
    o_ref[...] = _rms(_gelu(z), g_ref[...]).astype(o_ref.dtype)


def proj_gelu_norm(h, w, g, ts, out_dtype):
    G, S, D = h.shape
    n = w.shape[1]
    return pl.pallas_call(
        _proj_gelu_norm_kernel,
        grid=(G, S // ts),
        in_specs=[pl.BlockSpec((None, ts, D), lambda g_, s: (g_, s, 0)),
                  pl.BlockSpec((D, n), lambda g_, s: (0, 0)),
                  pl.BlockSpec((1, n), lambda g_, s: (0, 0))],
        out_specs=pl.BlockSpec((None, ts, n), lambda g_, s: (g_, s, 0)),
        out_shape=jax.ShapeDtypeStruct((G, S, n), out_dtype),
        compiler_params=_params("arbitrary", "arbitrary"),
        name="proj_gelu_norm",
    )(h, w, g.reshape(1, n))


def _q_proj_kernel(x_ref, g1_ref, sc_ref, sh_ref, wq_ref, gqa_ref, wuq_ref, gn_ref, gr_ref, tc_ref, ts_ref,
                   h_ref, q_ref, *, q_scale):
    h = (_rms(x_ref[...], g1_ref[...]) * (1.0 + sc_ref[...]) + sh_ref[...]).astype(h_ref.dtype)
    h_ref[...] = h
    zq = jnp.dot(h, wq_ref[...], preferred_element_type=F32)
    zn = _rms(zq, gqa_ref[...]).astype(BF16)
    tc, ts = tc_ref[...], ts_ref[...]
    gains = jnp.concatenate([gn_ref[...], gr_ref[...]], axis=1) * q_scale
    same_tile = (lax.broadcasted_iota(jnp.int32, (HEAD_PAD, HEAD_PAD), 0) // LANES
                 == lax.broadcasted_iota(jnp.int32, (HEAD_PAD, HEAD_PAD), 1) // LANES)
    tile_mean = jnp.where(same_tile, 1.0 / LANES, 0.0).astype(BF16)
    shift_lane = jnp.where(lax.broadcasted_iota(jnp.int32, (1, LANES), 1) == LANES // 2, 1.0, 0.0)
    for pair in range(q_ref.shape[0] // 2):
        blk2 = jnp.dot(zn, wuq_ref[:, pair * 2 * HEAD_PAD:(pair + 1) * 2 * HEAD_PAD],
                       preferred_element_type=F32)
        for sub in range(2):
            hh = 2 * pair + sub
            blk = blk2[:, sub * HEAD_PAD:(sub + 1) * HEAD_PAD]
            ms = jnp.dot((blk * blk).astype(BF16), tile_mean, preferred_element_type=F32)
            y = blk * lax.rsqrt(ms + EPS) * gains
            q_ref[hh, :, :LANES] = y[:, :LANES].astype(q_ref.dtype)
            q_ref[hh, :, LANES:] = (_rope_group(y[:, LANES:], tc, ts) + shift_lane).astype(q_ref.dtype)


def prenorm_q_project(x, g_norm, sc, sh, w_q, g_q_a, w_uq_pad, g_nope, g_rope_pad, tabs, ts, q_scale):
    G, S, D = x.shape
    ql = w_q.shape[1]
    H = w_uq_pad.shape[1] // HEAD_PAD
    const = lambda g, s: (0, 0)
    tab_spec = pl.BlockSpec((ts, LANES), lambda g, s: (s, 0))
    row = pl.BlockSpec((None, ts, D), lambda g, s: (g, s, 0))
    return pl.pallas_call(
        functools.partial(_q_proj_kernel, q_scale=q_scale),
        grid=(G, S // ts),
        in_specs=[row, pl.BlockSpec((1, D), const), _mod_spec(sc, ts), _mod_spec(sh, ts),
                  pl.BlockSpec((D, ql), const),
                  pl.BlockSpec((1, ql), const),
                  pl.BlockSpec((ql, H * HEAD_PAD), const),
                  pl.BlockSpec((1, LANES), const),
                  pl.BlockSpec((1, LANES), const),
                  tab_spec, tab_spec],
        out_specs=[row, pl.BlockSpec((None, H, ts, HEAD_PAD), lambda g, s: (g, 0, s, 0))],
        out_shape=[jax.ShapeDtypeStruct((G, S, D), BF16),
                   jax.ShapeDtypeStruct((G, H, S, HEAD_PAD), BF16)],
        compiler_params=_params("arbitrary", "arbitrary"),
        name="prenorm_q_project",
    )(x, g_norm.reshape(1, D), sc, sh, w_q, g_q_a.reshape(1, ql), w_uq_pad, g_nope.reshape(1, LANES),
      g_rope_pad, *tabs)


def _kv_proj_kernel(h_ref, wkv_ref, gkva_ref, gkr_ref, tc_ref, ts_ref, *rest, n_rope, expand):
    if expand:
        wuk_ref, wuv_ref, gkn_ref, kshift_ref, c_ref, kr_ref, k_ref, v_ref = rest
    else:
        c_ref, kr_ref, krp_ref = rest
    lat = c_ref.shape[-1]
    z = jnp.dot(h_ref[...], wkv_ref[...], preferred_element_type=F32)
    c = _rms(z[:, :lat], gkva_ref[...])
    c_ref[...] = c
    rope = _rope_group(_rms(z[:, lat:], gkr_ref[...]), tc_ref[...], ts_ref[...])
    kr_ref[...] = rope[:, :n_rope]
    if not expand:
        krp_ref[...] = rope.astype(BF16)
        return
    cb = c.astype(BF16)
    gkn = gkn_ref[...]
    rope_b = (rope + kshift_ref[...]).astype(BF16)
    for pair in range(k_ref.shape[0] // 2):
        cols = slice(pair * 2 * LANES, (pair + 1) * 2 * LANES)
        kn2 = jnp.dot(cb, wuk_ref[:, cols], preferred_element_type=F32)
        v2 = jnp.dot(cb, wuv_ref[:, cols], preferred_element_type=F32)
        for sub in range(2):
            hh = 2 * pair + sub
            lanes = slice(sub * LANES, (sub + 1) * LANES)
            k_ref[hh, :, :LANES] = _rms(kn2[:, lanes], gkn).astype(k_ref.dtype)
            k_ref[hh, :, LANES:] = rope_b
            v_ref[hh] = v2[:, lanes].astype(v_ref.dtype)


def kv_project(h, w_kv, g_kv_a, g_k_rope_pad, tabs, ts, n_rope, expand_weights=None):
    G, S, D = h.shape
    lat = g_kv_a.shape[-1]
    const = lambda g, s: (0, 0)
    tab_spec = pl.BlockSpec((ts, LANES), lambda g, s: (s, 0))
    in_specs = [pl.BlockSpec((None, ts, D), lambda g, s: (g, s, 0)),
                pl.BlockSpec((D, lat + LANES), const),
                pl.BlockSpec((1, lat), const),
                pl.BlockSpec((1, LANES), const),
                tab_spec, tab_spec]
    args = [h, w_kv, g_kv_a.reshape(1, lat), g_k_rope_pad, *tabs]
    out_specs = [pl.BlockSpec((None, ts, lat), lambda g, s: (g, s, 0)),
                 pl.BlockSpec((None, ts, n_rope), lambda g, s: (g, s, 0))]
    out_shape = [jax.ShapeDtypeStruct((G, S, lat), F32),
                 jax.ShapeDtypeStruct((G, S, n_rope), F32)]
    expand = expand_weights is not None
    if expand:
        w_uk, w_uv, g_k_nope, k_shift = expand_weights
        H = w_uk.shape[1] // LANES
        in_specs += [pl.BlockSpec(w_uk.shape, const), pl.BlockSpec(w_uv.shape, const),
                     pl.BlockSpec((1, LANES), const), pl.BlockSpec((1, LANES), const)]
        args += [w_uk, w_uv, g_k_nope.reshape(1, LANES), k_shift]
        out_specs += [pl.BlockSpec((None, H, ts, HEAD_PAD), lambda g, s: (g, 0, s, 0)),
                      pl.BlockSpec((None, H, ts, LANES), lambda g, s: (g, 0, s, 0))]
        out_shape += [jax.ShapeDtypeStruct((G, H, S, HEAD_PAD), BF16),
                      jax.ShapeDtypeStruct((G, H, S, LANES), BF16)]
    else:
        out_specs.append(pl.BlockSpec((None, ts, LANES), lambda g, s: (g, s, 0)))
        out_shape.append(jax.ShapeDtypeStruct((G, S, LANES), BF16))
    return pl.pallas_call(
        functools.partial(_kv_proj_kernel, n_rope=n_rope, expand=expand),
        grid=(G, S // ts),
        in_specs=in_specs,
        out_specs=out_specs,
        out_shape=out_shape,
        compiler_params=_params("arbitrary", "arbitrary"),
        name="kv_project",
    )(*args)


def _gelu_gate_kernel(h_ref, wu_ref, v_ref, w_ref, b_ref, o_ref, wm_sc, u_sc, *, n_chunks):
    u_sc[...] = _gelu(jnp.dot(h_ref[...], wu_ref[...], preferred_element_type=F32)).astype(u_sc.dtype)
    row = lax.broadcasted_iota(jnp.int32, (SG_CHUNK, SG_CHUNK), 0)
    col = lax.broadcasted_iota(jnp.int32, (SG_CHUNK, SG_CHUNK), 1)
    tril = col <= row
    n_groups = w_ref.shape[0]
    for g in range(n_groups):
        wm_sc[g] = jnp.where(tril, w_ref[g], 0.0).astype(BF16)

    def chunk_body(n, carry):
        rows = pl.ds(pl.multiple_of(n * SG_CHUNK, SG_CHUNK), SG_CHUNK)
        for g in range(n_groups):
            lanes = slice(g * LANES, (g + 1) * LANES)
            mix = jnp.dot(wm_sc[g], v_ref[rows, lanes], preferred_element_type=F32) + b_ref[:, lanes]
            o_ref[rows, lanes] = (u_sc[rows, lanes].astype(F32) * mix).astype(o_ref.dtype)
        return carry

    lax.fori_loop(0, n_chunks, chunk_body, 0)


def gelu_spatial_gate(h, w_u, v, w_s, bias_rows, ts):
    G, S, D = h.shape
    W = w_u.shape[1]
    blk = pl.BlockSpec((None, ts, W), lambda g, s: (g, s, 0))
    return pl.pallas_call(
        functools.partial(_gelu_gate_kernel, n_chunks=ts // SG_CHUNK),
        grid=(G, S // ts),
        in_specs=[pl.BlockSpec((None, ts, D), lambda g, s: (g, s, 0)),
                  pl.BlockSpec(w_u.shape, lambda g, s: (0, 0)),
                  blk,
                  pl.BlockSpec(w_s.shape, lambda g, s: (0, 0, 0)),
                  pl.BlockSpec(bias_rows.shape, lambda g, s: (0, 0))],
        out_specs=blk,
        out_shape=jax.ShapeDtypeStruct((G, S, W), BF16),
        scratch_shapes=[pltpu.VMEM(w_s.shape, BF16), pltpu.VMEM((ts, W), BF16)],
        compiler_params=_params("arbitrary", "arbitrary"),
        name="gelu_spatial_gate",
    )(h, w_u, v, w_s, bias_rows)


def _spatial_gate_open_kernel(u_ref, v_ref, wl_ref, b_ref, o_ref):
    T = u_ref.shape[1]
    for i in range(T):
        acc = b_ref[i:i + 1, :] + wl_ref[i, 0:1, :] * v_ref[:, 0, :]
        for j in range(1, i + 1):
            acc = acc + wl_ref[i, j:j + 1, :] * v_ref[:, j, :]
        o_ref[:, i, :] = (u_ref[:, i, :].astype(F32) * acc).astype(o_ref.dtype)


def spatial_gate_open(u, v, w_lanes, bias_rows):
    B, T, W = u.shape
    full = lambda a: pl.BlockSpec(a.shape, lambda i: (0,) * a.ndim)
    return pl.pallas_call(
        _spatial_gate_open_kernel,
        grid=(1,),
        in_specs=[full(u), full(v), full(w_lanes), full(bias_rows)],
        out_specs=pl.BlockSpec((B, T, W), lambda i: (0, 0, 0)),
        out_shape=jax.ShapeDtypeStruct((B, T, W), BF16),
        compiler_params=_params("arbitrary"),
        name="spatial_gate_open",
    )(u, v, w_lanes, bias_rows)


def _attn_kernel(bounded_ref, q_ref, k_ref, v_ref, o_ref, m_sc, acc_sc, *, tq, tk, td):
    i = pl.program_id(2)
    dv = v_ref.shape[-1]
    unroll = tq // tk
    nt = (((1,), (1,)), ((), ()))
    ones = jnp.ones((tk, LANES), BF16)
    d0 = pl.multiple_of(i * tq, tq)
    chunk_mask = (lax.broadcasted_iota(jnp.int32, (td, td), 1) // CHUNK
                  <= lax.broadcasted_iota(jnp.int32, (td, td), 0) // CHUNK)

    def scores(rows0, k0, keys):
        return lax.dot_general(q_ref[rows0:, :], k_ref[pl.ds(k0, keys), :], nt, preferred_element_type=F32)

    def diag_scores(c):
        r0 = c * td
        s = scores(r0, d0 + r0, td)
        top = jnp.where(chunk_mask, s[:td], -jnp.inf)
        return top if r0 + td == tq else jnp.concatenate([top, s[td:]], axis=0)

    def values(k0, keys):
        return jnp.concatenate([v_ref[pl.ds(k0, keys), :], ones[:keys]], axis=1)

    def finish():
        acc = acc_sc[...]
        o_ref[...] = (acc[:, :dv] / acc[:, dv:]).astype(o_ref.dtype)

    @pl.when(bounded_ref[0] == 1)
    def _():
        for c in range(tq // td):
            r0 = c * td
            pv = jnp.dot(jnp.exp2(diag_scores(c)).astype(BF16), values(d0 + r0, td),
                         preferred_element_type=F32)
            if c == 0:
                acc_sc[...] = pv
            else:
                acc_sc[r0:, :] += pv

        def sweep(k0, n_tiles):
            acc = acc_sc[...]
            for u in range(n_tiles):
                p = jnp.exp2(scores(0, k0 + u * tk, tk)).astype(BF16)
                acc = acc + jnp.dot(p, values(k0 + u * tk, tk), preferred_element_type=F32)
            acc_sc[...] = acc

        def body(j, carry):
            sweep(pl.multiple_of(j * (4 * tq), 4 * tq), 4 * unroll)
            return carry

        lax.fori_loop(0, i // 4, body, 0)

        @pl.when(i % 4 >= 2)
        def _():
            sweep(pl.multiple_of((i // 4) * (4 * tq), 4 * tq), 2 * unroll)

        @pl.when(i % 2 == 1)
        def _():
            sweep(pl.multiple_of((i - 1) * tq, tq), unroll)

        finish()

    @pl.when(bounded_ref[0] != 1)
    def _():
        def step(s, k0, m_old, acc_old):
            rows, keys = s.shape
            row_max = jnp.max(s, axis=-1, keepdims=True)
            m_new = (jnp.broadcast_to(row_max, (rows, LANES)) if m_old is None
                     else jnp.maximum(m_old, row_max))
            p = jnp.concatenate([jnp.exp2(s[:, c * LANES:(c + 1) * LANES] - m_new)
                                 for c in range(keys // LANES)], axis=1).astype(BF16)
            pv = jnp.dot(p, values(k0, keys), preferred_element_type=F32)
            if m_old is None:
                return m_new, pv
            alpha = jnp.exp2(m_old - m_new)
            return m_new, jnp.concatenate([alpha] * ((dv + LANES) // LANES), axis=1) * acc_old + pv

        for c in range(tq // td):
            r0 = c * td
            if c == 0:
                m, acc = step(diag_scores(c), d0, None, None)
            else:
                m, acc = step(diag_scores(c), d0 + r0, m_sc[r0:, :], acc_sc[r0:, :])
            m_sc[r0:, :] = m
            acc_sc[r0:, :] = acc

        def body(j, carry):
            k0 = pl.multiple_of(j * tq, tq)
            ss = [scores(0, k0 + u * tk, tk) for u in range(unroll)]
            m, acc = m_sc[...], acc_sc[...]
            for u in range(unroll):
                m, acc = step(ss[u], k0 + u * tk, m, acc)
            m_sc[...] = m
            acc_sc[...] = acc
            return carry

        lax.fori_loop(0, i, body, 0)
        finish()


def attention_prompt(bounded, q, k, v, tq, tk, td):
    G, H, S, _ = q.shape
    dv = v.shape[-1]
    assert dv == LANES and tq % tk == 0 and tq % td == 0 and td % LANES == 0 and td <= tk
    return pl.pallas_call(
        functools.partial(_attn_kernel, tq=tq, tk=tk, td=td),
        grid=(G, H, S // tq),
        in_specs=[pl.BlockSpec(memory_space=pltpu.SMEM),
                  pl.BlockSpec((None, None, tq, HEAD_PAD), lambda g, h, i: (g, h, i, 0)),
                  pl.BlockSpec((None, None, S, HEAD_PAD), lambda g, h, i: (g, h, 0, 0)),
                  pl.BlockSpec((None, None, S, dv), lambda g, h, i: (g, h, 0, 0))],
        out_specs=pl.BlockSpec((None, tq, dv), lambda g, h, i: (g, i, h)),
        out_shape=jax.ShapeDtypeStruct((G, S, H * dv), BF16),
        scratch_shapes=[pltpu.VMEM((tq, LANES), F32), pltpu.VMEM((tq, dv + LANES), F32)],
        compiler_params=_params("arbitrary", "arbitrary", "arbitrary"),
        name="attention_prompt",
    )(bounded, q, k, v)


def _attn_cached_kernel(q_ref, cache_ref, ckr_ref, cnew_ref, krnew_ref, wukt_ref, wuv_ref, gkn_ref,
                        o_ref, call_sc, krall_sc, p_sc, *, past, n_new):
    H = q_ref.shape[0]
    L = past + n_new
    Lp = call_sc.shape[0]
    lat = call_sc.shape[1]
    call_sc[0:past, :] = cache_ref[...].astype(BF16)
    call_sc[past:L, :] = cnew_ref[...].astype(BF16)
    call_sc[L:Lp, :] = jnp.zeros((Lp - L, lat), BF16)
    n_rope = ckr_ref.shape[-1]
    krall_sc[0:past, n_rope:] = jnp.zeros((past, LANES - n_rope), BF16)
    krall_sc[0:past, :n_rope] = ckr_ref[...].astype(BF16)
    krall_sc[past:L, :] = krnew_ref[...]
    krall_sc[L:Lp, :] = jnp.zeros((Lp - L, LANES), BF16)
    call = call_sc[...]
    gkn = gkn_ref[...]
    valid = lax.broadcasted_iota(jnp.int32, (n_new, Lp), 1) < L
    nt = (((1,), (1,)), ((), ()))
    q_rope = jnp.concatenate([q_ref[hh][:, LANES:] for hh in range(H)], axis=0)
    s_rope = lax.dot_general(q_rope, krall_sc[...], nt, preferred_element_type=F32)
    group = 4
    for grp in range(H // group):
        kn_t = lax.dot_general(wukt_ref[grp * group * LANES:(grp + 1) * group * LANES, :], call, nt,
                               preferred_element_type=F32)
        for sub in range(group):
            hh = group * grp + sub
            blk = kn_t[sub * LANES:(sub + 1) * LANES, :]
            inv_rms = lax.rsqrt(jnp.mean(blk * blk, axis=0, keepdims=True) + EPS)
            q_nope = (q_ref[hh][:, :LANES].astype(F32) * gkn).astype(BF16)
            s = jnp.dot(q_nope, blk.astype(BF16), preferred_element_type=F32) * inv_rms
            s = s + s_rope[hh * n_new:(hh + 1) * n_new, :]
            s = jnp.where(valid, s, -jnp.inf)
            m = jnp.max(s, axis=-1, keepdims=True)
            p = jnp.exp2(s - m)
            p = p / jnp.sum(p, axis=-1, keepdims=True)
            p_sc[hh * n_new:(hh + 1) * n_new, :] = p.astype(BF16)
    o_lat = jnp.dot(p_sc[...], call, preferred_element_type=F32).astype(BF16)
    for hh in range(H):
        o_ref[:, hh * LANES:(hh + 1) * LANES] = jnp.dot(
            o_lat[hh * n_new:(hh + 1) * n_new, :], wuv_ref[:, hh * LANES:(hh + 1) * LANES],
            preferred_element_type=F32).astype(o_ref.dtype)


def attention_cached(q, cache_lat, cache_kr, layer, c_new, kr_new_pad, w_uk_t, w_uv, g_k_nope, n_new):
    _, H, BT, _ = q.shape
    _, B, past, lat = cache_lat.shape
    n_rope = cache_kr.shape[-1]
    L = past + n_new
    Lp = -(-L // LANES) * LANES
    const = lambda b: (0, 0)
    return pl.pallas_call(
        functools.partial(_attn_cached_kernel, past=past, n_new=n_new),
        grid=(B,),
        in_specs=[pl.BlockSpec((None, H, n_new, HEAD_PAD), lambda b: (0, 0, b, 0)),
                  pl.BlockSpec((None, None, past, lat), lambda b: (layer, b, 0, 0)),
                  pl.BlockSpec((None, None, past, n_rope), lambda b: (layer, b, 0, 0)),
                  pl.BlockSpec((None, n_new, lat), lambda b: (0, b, 0)),
                  pl.BlockSpec((None, n_new, LANES), lambda b: (0, b, 0)),
                  pl.BlockSpec(w_uk_t.shape, const), pl.BlockSpec(w_uv.shape, const),
                  pl.BlockSpec((1, LANES), const)],
        out_specs=pl.BlockSpec((None, n_new, H * LANES), lambda b: (0, b, 0)),
        out_shape=jax.ShapeDtypeStruct((1, BT, H * LANES), BF16),
        scratch_shapes=[pltpu.VMEM((Lp, lat), BF16), pltpu.VMEM((Lp, LANES), BF16),
                        pltpu.VMEM((H * n_new, Lp), BF16)],
        compiler_params=_params("arbitrary"),
        name="attention_cached",
    )(q, cache_lat, cache_kr, c_new, kr_new_pad, w_uk_t, w_uv, g_k_nope.reshape(1, LANES))


def _merge_kernel(a_ref, b_ref, ga_ref, gb_ref, wpa_ref, wpb_ref, o_ref):
    pa = jnp.dot(a_ref[...], wpa_ref[...], preferred_element_type=F32)
    pb = jnp.dot(b_ref[...], wpb_ref[...], preferred_element_type=F32)
    o_ref[...] = (ga_ref[...].astype(F32) * pa + gb_ref[...].astype(F32) * pb).astype(o_ref.dtype)


def merge_branches(o_sg, o_mla, gates, w_pa, w_pb, ts, tn):
    G, S, W = o_sg.shape
    D = w_pa.shape[1]
    nj = D // tn
    row = lambda g, s, j: (g, s, 0)
    return pl.pallas_call(
        _merge_kernel,
        grid=(G, S // ts, nj),
        in_specs=[pl.BlockSpec((None, ts, W), row),
                  pl.BlockSpec((None, ts, o_mla.shape[-1]), row),
                  pl.BlockSpec((None, ts, tn), lambda g, s, j: (g, s, j)),
                  pl.BlockSpec((None, ts, tn), lambda g, s, j: (g, s, j + nj)),
                  pl.BlockSpec((W, tn), lambda g, s, j: (0, j)),
                  pl.BlockSpec((o_mla.shape[-1], tn), lambda g, s, j: (0, j))],
        out_specs=pl.BlockSpec((None, ts, tn), lambda g, s, j: (g, s, j)),
        out_shape=jax.ShapeDtypeStruct((G, S, D), BF16),
        compiler_params=_params("arbitrary", "arbitrary", "arbitrary"),
        name="merge_branches",
    )(o_sg, o_mla, gates, gates, w_pa, w_pb)


def _out_proj_kernel(m_ref, wo_ref, x_ref, g1_ref, gn_ref, sc_ref, sh_ref, x1_ref, h2_ref):
    y = jnp.dot(m_ref[...], wo_ref[...], preferred_element_type=F32)
    x1 = x_ref[...] + g1_ref[...] * y
    x1_ref[...] = x1
    h2_ref[...] = (_rms(x1, gn_ref[...]) * (1.0 + sc_ref[...]) + sh_ref[...]).astype(h2_ref.dtype)


def out_project(m, w_o, x, g1, g_norm2, sc2, sh2, ts):
    G, S, D = x.shape
    row = pl.BlockSpec((None, ts, D), lambda g, s: (g, s, 0))
    return pl.pallas_call(
        _out_proj_kernel,
        grid=(G, S // ts),
        in_specs=[row, pl.BlockSpec(w_o.shape, lambda g, s: (0, 0)), row,
                  _mod_spec(g1, ts), pl.BlockSpec((1, D), lambda g, s: (0, 0)),
                  _mod_spec(sc2, ts), _mod_spec(sh2, ts)],
        out_specs=[row, row],
        out_shape=[jax.ShapeDtypeStruct((G, S, D), F32), jax.ShapeDtypeStruct((G, S, D), BF16)],
        compiler_params=_params("arbitrary", "arbitrary"),
        name="out_project",
    )(m, w_o, x, g1, g_norm2.reshape(1, D), sc2, sh2)


def _mlp_kernel(h_ref, wup_ref, wdn_ref, x1_ref, g2_ref, o_ref, part_sc, *, n_steps):
    j = pl.program_id(2)

    def chunk():
        hid = jnp.dot(h_ref[...], wup_ref[...], preferred_element_type=F32)
        hid = jnp.square(jnp.maximum(hid, 0.0)).astype(BF16)
        return jnp.dot(hid, wdn_ref[...], preferred_element_type=F32)

    def residual(total):
        o_ref[...] = x1_ref[...] + g2_ref[...] * total

    if n_steps == 1:
        residual(chunk())
        return
    last = n_steps - 1

    @pl.when(j == 0)
    def _():
        part_sc[...] = chunk()

    if n_steps > 2:
        @pl.when(j == 1)
        def _():
            o_ref[...] = part_sc[...]
            part_sc[...] = chunk()

        @pl.when(jnp.logical_and(j > 1, j < last))
        def _():
            o_ref[...] += part_sc[...]
            part_sc[...] = chunk()

    @pl.when(j == last)
    def _():
        prev = part_sc[...] if n_steps == 2 else o_ref[...] + part_sc[...]
        residual(prev + chunk())


def mlp_residual(h2, w_up, w_down, x1, g2, ts, th):
    G, S, D = x1.shape
    hidden = w_up.shape[1]
    row = pl.BlockSpec((None, ts, D), lambda g, s, j: (g, s, 0))
    return pl.pallas_call(
        functools.partial(_mlp_kernel, n_steps=hidden // th),
        grid=(G, S // ts, hidden // th),
        in_specs=[row,
                  pl.BlockSpec((D, th), lambda g, s, j: (0, j)),
                  pl.BlockSpec((th, D), lambda g, s, j: (j, 0)),
                  row, _mod_spec(g2, ts)],
        out_specs=row,
        out_shape=jax.ShapeDtypeStruct((G, S, D), F32),
        scratch_shapes=[pltpu.VMEM((ts, D), F32)],
        compiler_params=pltpu.CompilerParams(dimension_semantics=("arbitrary",) * 3,
                                             vmem_limit_bytes=VMEM_LIMIT_MLP),
        name="mlp_residual",
    )(h2, w_up, w_down, x1, g2)


def _rope_tables(pos, n_rope):
    inv = jnp.float32(ROPE_BASE) ** (-jnp.arange(0, n_rope, 2, dtype=F32) / n_rope)
    ang = pos.astype(F32)[:, None] * inv[None, :]
    cos, sin = lax.optimization_barrier((jnp.cos(ang), jnp.sin(ang)))
    pad = jnp.zeros((pos.shape[0], LANES - n_rope), F32)
    return (jnp.concatenate([cos, cos, pad], axis=1), jnp.concatenate([-sin, sin, pad], axis=1))


def _dup_lanes(g):
    return jnp.concatenate([g.astype(F32), g.astype(F32)]).reshape(1, 2 * g.shape[0])


def _layer(x, mods, pos, P, ts, *, cache=None):
    sh1, sc1, g1, sh2, sc2, g2 = mods
    n_rope = P["n_rope"]
    tabs = _rope_tables(pos, n_rope)
    h, q = prenorm_q_project(x, P["g_norm1"], sc1, sh1, P["w_q"], P["g_q_a"], P["w_uq"], P["g_q_nope"],
                             P["g_q_rope"], tabs, ts, P["q_scale"])
    gates = proj_act(h, P["w_g"], "sigmoid", ts, P["w_g"].shape[1] // 2)
    if cache is None:
        v = proj_gelu_norm(h, P["w_v"], P["g_sg"], ts, BF16)
        o_sg = gelu_spatial_gate(h, P["w_u"], v, P["w_s"], P["b_rows"], ts)
        c_kv, k_rope, k, vv = kv_project(h, P["w_kv"], P["g_kv_a"], P["g_k_rope"], tabs, ts, n_rope,
                                         expand_weights=(P["w_uk"], P["w_uv"], P["g_k_nope"], P["k_shift"]))
        o_mla = attention_prompt(P["bounded"], q, k, vv, tq=min(2048, x.shape[1]),
                                 tk=min(1024, x.shape[1]), td=min(512, x.shape[1]))
        extra = ()
    else:
        cache_lat, cache_kr, layer, B, T = cache
        u = proj_act(h, P["w_u"], "gelu", ts, P["w_u"].shape[1])
        v = proj_gelu_norm(h, P["w_v"], P["g_sg"], ts, F32)
        W = v.shape[-1]
        o_sg = spatial_gate_open(u.reshape(B, T, W), v.reshape(B, T, W), P["w_lanes"][:T, :T],
                                 P["b_rows"][:T]).reshape(1, B * T, W)
        c_kv, k_rope, kr_pad = kv_project(h, P["w_kv"], P["g_kv_a"], P["g_k_rope"], tabs, ts, n_rope)
        o_mla = attention_cached(q, cache_lat, cache_kr, layer, c_kv, kr_pad, P["w_uk_t"], P["w_uv"],
                                 P["g_k_nope"], T)
        extra = (v,)
    m = merge_branches(o_sg, o_mla, gates, P["w_pa"], P["w_pb"], ts, P["w_pa"].shape[1] // 2)
    ts_res = min(ts, 512)
    x1, h2 = out_project(m, P["w_o"], x, g1, P["g_norm2"], sc2, sh2, ts_res)
    y = mlp_residual(h2, P["w_up"], P["w_down"], x1, g2, ts_res, min(2048, P["w_up"].shape[1]))
    return (y, c_kv, k_rope) + extra


def kernel(x_prompt, x_sample, cache_kv_latent, cache_k_rope, c_prompt, c_sample, w_ada, b_ada, g_norm1, g_norm2, w_in, g_sg, w_s, b_s, g_q_a, w_uq, g_q_nope, g_q_rope, g_kv_a, g_k_rope, w_uk, g_k_nope, w_uv, w_pa, w_pb, w_o, w_up, w_down):
    depth = w_in.shape[0]
    Bp, S, D = x_prompt.shape
    Bs, T, _ = x_sample.shape
    past = cache_kv_latent.shape[2]
    q_lora = g_q_a.shape[-1]
    lat = g_kv_a.shape[-1]
    n_nope = g_q_nope.shape[-1]
    n_rope = g_q_rope.shape[-1]
    H = w_uk.shape[2]
    sg_w = g_sg.shape[-1]
    off_q = 2 * sg_w
    off_kv = off_q + q_lora
    off_gate = off_kv + lat + n_rope
    assert n_nope == LANES and w_uv.shape[-1] == LANES and 2 * n_rope == LANES
    assert sg_w // SG_GROUPS == LANES and H == N_HEADS

    nb = Bp + Bs
    nb_pad = -(-nb // 8) * 8
    c_all = jnp.concatenate([c_prompt, c_sample, jnp.zeros((nb_pad - nb, D), F32)], axis=0)

    y_p, y_s = x_prompt, x_sample.reshape(1, Bs * T, D)
    outs = [[] for _ in range(5)]
    for l in range(depth):
        def w_in_cols(a, b):
            return w_in[l][:, a:b].astype(BF16)
        w_uq_l = w_uq[l].astype(BF16).reshape(q_lora, H, n_nope + n_rope)
        w_uq_pad = jnp.concatenate([w_uq_l, w_uq_l[:, :, n_nope:]], axis=2)
        q_scale = float((n_nope + n_rope) ** -0.5 * LOG2E)
        def sq_norm_bound(g_nope, g_rope):
            return n_nope * jnp.max(jnp.square(g_nope)) + n_rope * jnp.max(jnp.square(g_rope))
        score_bound = q_scale * jnp.sqrt(sq_norm_bound(g_q_nope[l], g_q_rope[l])
                                         * sq_norm_bound(g_k_nope[l], g_k_rope[l]))
        bounded = (score_bound <= BOUND_MAX).astype(jnp.int32).reshape(1)
        k_shift = jnp.where(jnp.arange(LANES) == LANES // 2, -score_bound, 0.0).astype(F32).reshape(1, LANES)
        P = {
            "n_rope": n_rope,
            "q_scale": q_scale, "k_shift": k_shift, "bounded": bounded,
            "g_norm1": g_norm1[l], "g_norm2": g_norm2[l], "g_sg": g_sg[l],
            "w_u": w_in_cols(0, sg_w), "w_v": w_in_cols(sg_w, off_q), "w_q": w_in_cols(off_q, off_kv),
            "w_kv": jnp.concatenate([w_in_cols(off_kv, off_gate), w_in_cols(off_kv + lat, off_gate)], axis=1),
            "w_g": w_in_cols(off_gate, w_in.shape[2]),
            "g_q_a": g_q_a[l], "w_uq": w_uq_pad.reshape(q_lora, H * HEAD_PAD),
            "g_q_nope": g_q_nope[l], "g_q_rope": _dup_lanes(g_q_rope[l]),
            "g_kv_a": g_kv_a[l], "g_k_rope": _dup_lanes(g_k_rope[l]),
            "w_uk": w_uk[l].astype(BF16).reshape(lat, H * n_nope),
            "w_uk_t": w_uk[l].astype(BF16).reshape(lat, H * n_nope).T,
            "w_uv": w_uv[l].astype(BF16).reshape(lat, H * LANES),
            "g_k_nope": g_k_nope[l],
            "w_s": w_s[l],
            "b_rows": jnp.repeat(b_s[l].T, LANES, axis=1),
            "w_lanes": jnp.repeat(w_s[l][:, :T, :T].transpose(1, 2, 0), LANES, axis=2),
            "w_pa": w_pa[l].astype(BF16), "w_pb": w_pb[l].astype(BF16), "w_o": w_o[l].astype(BF16),
            "w_up": w_up[l].astype(BF16), "w_down": w_down[l].astype(BF16),
        }
        mod = ada_project(c_all, w_ada[l], b_ada[l])
        mods_p = [a.reshape(Bp, 1, D) for a in jnp.split(mod[:Bp], 6, axis=-1)]
        mods_s = [jnp.repeat(a, T, axis=0).reshape(1, Bs * T, D)
                  for a in jnp.split(mod[Bp:nb], 6, axis=-1)]

        ts_p = min(1024, S)
        y_p, lp, kp = _layer(y_p, mods_p, jnp.arange(S), P, ts_p)
        pos_s = jnp.tile(past + jnp.arange(T), Bs)
        y_s, ls, ks, vs = _layer(y_s, mods_s, pos_s, P, Bs * T,
                                 cache=(cache_kv_latent, cache_k_rope, l, Bs, T))
        for lst, a in zip(outs, (lp, kp, ls.reshape(Bs, T, lat), ks.reshape(Bs, T, n_rope),
                                 vs.reshape(Bs, T, sg_w))):
            lst.append(a)
    return (y_p, y_s.reshape(Bs, T, D)) + tuple(o[0][None] if depth == 1 else jnp.stack(o) for o in outs)
```

```python
import functools
import math

import jax
import jax.numpy as jnp
import numpy as np
from jax import lax
from jax.experimental import pallas as pl
from jax.experimental.pallas import tpu as pltpu

F32 = jnp.float32
BF16 = jnp.bfloat16

EPS = 1e-6
ROPE_BASE = 10000.0
N_HEADS = 16
CHUNK = 64
SG_CHUNK = 128
SG_GROUPS = 16
LANES = 128
HEAD_PAD = 256
LOG2E = math.log2(math.e)
BOUND_MAX = 50.0

VMEM_LIMIT = 58 * 1024 * 1024
VMEM_LIMIT_MLP = 61 * 1024 * 1024


def _params(*sem):
    return pltpu.CompilerParams(dimension_semantics=sem, vmem_limit_bytes=VMEM_LIMIT)


def _rms(x, g):
    ms = jnp.mean(x * x, axis=-1, keepdims=True)
    return x * lax.rsqrt(ms + EPS) * g


def _rope_group(t, tc, ts):
    return t * tc + pltpu.roll(t, LANES - LANES // 4, 1) * ts


def _gelu(z):
    return 0.5 * z * (1.0 + lax.erf(z * np.float32(math.sqrt(0.5))))


def _mod_spec(mod, ts):
    d = mod.shape[-1]
    if mod.shape[1] == 1:
        return pl.BlockSpec((None, 1, d), lambda g, s, *_: (g, 0, 0))
    return pl.BlockSpec((None, ts, d), lambda g, s, *_: (g, s, 0))


def _ada_kernel(c_ref, w_ref, b_ref, o_ref):
    c = c_ref[...]
    s = (c * jax.nn.sigmoid(c)).astype(BF16)
    o_ref[...] = jnp.dot(s, w_ref[...].astype(BF16), preferred_element_type=F32) + b_ref[...]


def ada_project(c, w_ada, b_ada, tn=1024):
    r, d = c.shape
    n = w_ada.shape[1]
    return pl.pallas_call(
        _ada_kernel,
        grid=(n // tn,),
        in_specs=[pl.BlockSpec((r, d), lambda j: (0, 0)),
                  pl.BlockSpec((d, tn), lambda j: (0, j)),
                  pl.BlockSpec((1, tn), lambda j: (0, j))],
        out_specs=pl.BlockSpec((r, tn), lambda j: (0, j)),
        out_shape=jax.ShapeDtypeStruct((r, n), F32),
        compiler_params=_params("arbitrary"),
        name="ada_project",
    )(c, w_ada, b_ada.reshape(1, n))


def _proj_act_kernel(h_ref, w_ref, o_ref, *, act):
    z = jnp.dot(h_ref[...], w_ref[...], preferred_element_type=F32)
    if act == "gelu":
        a = _gelu(z)
    else:
        a = 0.5 * (1.0 + jnp.tanh(0.5 * z))
    o_ref[...] = a.astype(o_ref.dtype)


def proj_act(h, w, act, ts, tn):
    G, S, D = h.shape
    n = w.shape[1]
    return pl.pallas_call(
        functools.partial(_proj_act_kernel, act=act),
        grid=(G, S // ts, n // tn),
        in_specs=[pl.BlockSpec((None, ts, D), lambda g, s, j: (g, s, 0)),
                  pl.BlockSpec((D, tn), lambda g, s, j: (0, j))],
        out_specs=pl.BlockSpec((None, ts, tn), lambda g, s, j: (g, s, j)),
        out_shape=jax.ShapeDtypeStruct((G, S, n), BF16),
        compiler_params=_params("arbitrary", "arbitrary", "arbitrary"),
        name="proj_" + act,
    )(h, w)


def _proj_gelu_norm_kernel(h_ref, w_ref, g_ref, o_ref):
    z = jnp.dot(h_ref[...], w_ref[...], preferred_element_type=F32)
    o_ref[...] = _rms(_gelu(z), g_ref[...]).astype(o_ref.dtype)


def proj_gelu_norm(h, w, g, ts, out_dtype):
    G, S, D = h.shape
    n = w.shape[1]
    return pl.pallas_call(
        _proj_gelu_norm_kernel,
        grid=(G, S // ts),
        in_specs=[pl.BlockSpec((None, ts, D), lambda g_, s: (g_, s, 0)),
                  pl.BlockSpec((D, n), lambda g_, s: (0, 0)),
                  pl.BlockSpec((1, n), lambda g_, s: (0, 0))],
        out_specs=pl.BlockSpec((None, ts, n), lambda g_, s: (g_, s, 0)),
        out_shape=jax.ShapeDtypeStruct((G, S, n), out_dtype),
        compiler_params=_params("arbitrary", "arbitrary"),
        name="proj_gelu_norm",
    )(h, w, g.reshape(1, n))


def _q_proj_kernel(x_ref, g1_ref, sc_ref, sh_ref, wq_ref, gqa_ref, wuq_ref, gn_ref, gr_ref, tc_ref, ts_ref,
                   h_ref, q_ref, *, q_scale):
    h = (_rms(x_ref[...], g1_ref[...]) * (1.0 + sc_ref[...]) + sh_ref[...]).astype(h_ref.dtype)
    h_ref[...] = h
    zq = jnp.dot(h, wq_ref[...], preferred_element_type=F32)
    zn = _rms(zq, gqa_ref[...]).astype(BF16)
    tc, ts = tc_ref[...], ts_ref[...]
    gains = jnp.concatenate([gn_ref[...], gr_ref[...]], axis=1) * q_scale
    same_tile = (lax.broadcasted_iota(jnp.int32, (HEAD_PAD, HEAD_PAD), 0) // LANES
                 == lax.broadcasted_iota(jnp.int32, (HEAD_PAD, HEAD_PAD), 1) // LANES)
    tile_mean = jnp.where(same_tile, 1.0 / LANES, 0.0).astype(BF16)
    shift_lane = jnp.where(lax.broadcasted_iota(jnp.int32, (1, LANES), 1) == LANES // 2, 1.0, 0.0)
    for pair in range(q_ref.shape[0] // 2):
        blk2 = jnp.dot(zn, wuq_ref[:, pair * 2 * HEAD_PAD:(pair + 1) * 2 * HEAD_PAD],
                       preferred_element_type=F32)
        for sub in range(2):
            hh = 2 * pair + sub
            blk = blk2[:, sub * HEAD_PAD:(sub + 1) * HEAD_PAD]
            ms = jnp.dot((blk * blk).astype(BF16), tile_mean, preferred_element_type=F32)
            y = blk * lax.rsqrt(ms + EPS) * gains
            q_ref[hh, :, :LANES] = y[:, :LANES].astype(q_ref.dtype)
            q_ref[hh, :, LANES:] = (_rope_group(y[:, LANES:], tc, ts) + shift_lane).astype(q_ref.dtype)


def prenorm_q_project(x, g_norm, sc, sh, w_q, g_q_a, w_uq_pad, g_nope, g_rope_pad, tabs, ts, q_scale):
    G, S, D = x.shape
    ql = w_q.shape[1]
    H = w_uq_pad.shape[1] // HEAD_PAD
    const = lambda g, s: (0, 0)
    tab_spec = pl.BlockSpec((ts, LANES), lambda g, s: (s, 0))
    row = pl.BlockSpec((None, ts, D), lambda g, s: (g, s, 0))
    return pl.pallas_call(
        functools.partial(_q_proj_kernel, q_scale=q_scale),
        grid=(G, S // ts),
        in_specs=[row, pl.BlockSpec((1, D), const), _mod_spec(sc, ts), _mod_spec(sh, ts),
                  pl.BlockSpec((D, ql), const),
                  pl.BlockSpec((1, ql), const),
                  pl.BlockSpec((ql, H * HEAD_PAD), const),
                  pl.BlockSpec((1, LANES), const),
                  pl.BlockSpec((1, LANES), const),
                  tab_spec, tab_spec],
        out_specs=[row, pl.BlockSpec((None, H, ts, HEAD_PAD), lambda g, s: (g, 0, s, 0))],
        out_shape=[jax.ShapeDtypeStruct((G, S, D), BF16),
                   jax.ShapeDtypeStruct((G, H, S, HEAD_PAD), BF16)],
        compiler_params=_params("arbitrary", "arbitrary"),
        name="prenorm_q_project",
    )(x, g_norm.reshape(1, D), sc, sh, w_q, g_q_a.reshape(1, ql), w_uq_pad, g_nope.reshape(1, LANES),
      g_rope_pad, *tabs)


def _kv_proj_kernel(h_ref, wkv_ref, gkva_ref, gkr_ref, tc_ref, ts_ref, *rest, n_rope, expand):
    if expand:
        wuk_ref, wuv_ref, gkn_ref, kshift_ref, c_ref, kr_ref, k_ref, v_ref = rest
    else:
        c_ref, kr_ref, krp_ref = rest
    lat = c_ref.shape[-1]
    z = jnp.dot(h_ref[...], wkv_ref[...], preferred_element_type=F32)
    c = _rms(z[:, :lat], gkva_ref[...])
    c_ref[...] = c
    rope = _rope_group(_rms(z[:, lat:], gkr_ref[...]), tc_ref[...], ts_ref[...])
    kr_ref[...] = rope[:, :n_rope]
    if not expand:
        krp_ref[...] = rope.astype(BF16)
        return
    cb = c.astype(BF16)
    gkn = gkn_ref[...]
    rope_b = (rope + kshift_ref[...]).astype(BF16)
    for pair in range(k_ref.shape[0] // 2):
        cols = slice(pair * 2 * LANES, (pair + 1) * 2 * LANES)
        kn2 = jnp.dot(cb, wuk_ref[:, cols], preferred_element_type=F32)
        v2 = jnp.dot(cb, wuv_ref[:, cols], preferred_element_type=F32)
        for sub in range(2):
            hh = 2 * pair + sub
            lanes = slice(sub * LANES, (sub + 1) * LANES)
            k_ref[hh, :, :LANES] = _rms(kn2[:, lanes], gkn).astype(k_ref.dtype)
            k_ref[hh, :, LANES:] = rope_b
            v_ref[hh] = v2[:, lanes].astype(v_ref.dtype)


def kv_project(h, w_kv, g_kv_a, g_k_rope_pad, tabs, ts, n_rope, expand_weights=None):
    G, S, D = h.shape
    lat = g_kv_a.shape[-1]
    const = lambda g, s: (0, 0)
    tab_spec = pl.BlockSpec((ts, LANES), lambda g, s: (s, 0))
    in_specs = [pl.BlockSpec((None, ts, D), lambda g, s: (g, s, 0)),
                pl.BlockSpec((D, lat + LANES), const),
                pl.BlockSpec((1, lat), const),
                pl.BlockSpec((1, LANES), const),
                tab_spec, tab_spec]
    args = [h, w_kv, g_kv_a.reshape(1, lat), g_k_rope_pad, *tabs]
    out_specs = [pl.BlockSpec((None, ts, lat), lambda g, s: (g, s, 0)),
                 pl.BlockSpec((None, ts, n_rope), lambda g, s: (g, s, 0))]
    out_shape = [jax.ShapeDtypeStruct((G, S, lat), F32),
                 jax.ShapeDtypeStruct((G, S, n_rope), F32)]
    expand = expand_weights is not None
    if expand:
        w_uk, w_uv, g_k_nope, k_shift = expand_weights
        H = w_uk.shape[1] // LANES
        in_specs += [pl.BlockSpec(w_uk.shape, const), pl.BlockSpec(w_uv.shape, const),
                     pl.BlockSpec((1, LANES), const), pl.BlockSpec((1, LANES), const)]
        args += [w_uk, w_uv, g_k_nope.reshape(1, LANES), k_shift]
        out_specs += [pl.BlockSpec((None, H, ts, HEAD_PAD), lambda g, s: (g, 0, s, 0)),
                      pl.BlockSpec((None, H, ts, LANES), lambda g, s: (g, 0, s, 0))]
        out_shape += [jax.ShapeDtypeStruct((G, H, S, HEAD_PAD), BF16),
                      jax.ShapeDtypeStruct((G, H, S, LANES), BF16)]
    else:
        out_specs.append(pl.BlockSpec((None, ts, LANES), lambda g, s: (g, s, 0)))
        out_shape.append(jax.ShapeDtypeStruct((G, S, LANES), BF16))
    return pl.pallas_call(
        functools.partial(_kv_proj_kernel, n_rope=n_rope, expand=expand),
        grid=(G, S // ts),
        in_specs=in_specs,
        out_specs=out_specs,
        out_shape=out_shape,
        compiler_params=_params("arbitrary", "arbitrary"),
        name="kv_project",
    )(*args)


def _gelu_gate_kernel(h_ref, wu_ref, v_ref, w_ref, b_ref, o_ref, wm_sc, u_sc, *, n_chunks):
    u_sc[...] = _gelu(jnp.dot(h_ref[...], wu_ref[...], preferred_element_type=F32)).astype(u_sc.dtype)
    row = lax.broadcasted_iota(jnp.int32, (SG_CHUNK, SG_CHUNK), 0)
    col = lax.broadcasted_iota(jnp.int32, (SG_CHUNK, SG_CHUNK), 1)
    tril = col <= row
    n_groups = w_ref.shape[0]
    for g in range(n_groups):
        wm_sc[g] = jnp.where(tril, w_ref[g], 0.0).astype(BF16)

    def chunk_body(n, carry):
        rows = pl.ds(pl.multiple_of(n * SG_CHUNK, SG_CHUNK), SG_CHUNK)
        for g in range(n_groups):
            lanes = slice(g * LANES, (g + 1) * LANES)
            mix = jnp.dot(wm_sc[g], v_ref[rows, lanes], preferred_element_type=F32) + b_ref[:, lanes]
            o_ref[rows, lanes] = (u_sc[rows, lanes].astype(F32) * mix).astype(o_ref.dtype)
        return carry

    lax.fori_loop(0, n_chunks, chunk_body, 0)


def gelu_spatial_gate(h, w_u, v, w_s, bias_rows, ts):
    G, S, D = h.shape
    W = w_u.shape[1]
    blk = pl.BlockSpec((None, ts, W), lambda g, s: (g, s, 0))
    return pl.pallas_call(
        functools.partial(_gelu_gate_kernel, n_chunks=ts // SG_CHUNK),
        grid=(G, S // ts),
        in_specs=[pl.BlockSpec((None, ts, D), lambda g, s: (g, s, 0)),
                  pl.BlockSpec(w_u.shape, lambda g, s: (0, 0)),
                  blk,
                  pl.BlockSpec(w_s.shape, lambda g, s: (0, 0, 0)),
                  pl.BlockSpec(bias_rows.shape, lambda g, s: (0, 0))],
        out_specs=blk,
        out_shape=jax.ShapeDtypeStruct((G, S, W), BF16),
        scratch_shapes=[pltpu.VMEM(w_s.shape, BF16), pltpu.VMEM((ts, W), BF16)],
        compiler_params=_params("arbitrary", "arbitrary"),
        name="gelu_spatial_gate",
    )(h, w_u, v, w_s, bias_rows)


def _spatial_gate_open_kernel(u_ref, v_ref, wl_ref, b_ref, o_ref):
    T = u_ref.shape[1]
    for i in range(T):
        acc = b_ref[i:i + 1, :] + wl_ref[i, 0:1, :] * v_ref[:, 0, :]
        for j in range(1, i + 1):
            acc = acc + wl_ref[i, j:j + 1, :] * v_ref[:, j, :]
        o_ref[:, i, :] = (u_ref[:, i, :].astype(F32) * acc).astype(o_ref.dtype)


def spatial_gate_open(u, v, w_lanes, bias_rows):
    B, T, W = u.shape
    full = lambda a: pl.BlockSpec(a.shape, lambda i: (0,) * a.ndim)
    return pl.pallas_call(
        _spatial_gate_open_kernel,
        grid=(1,),
        in_specs=[full(u), full(v), full(w_lanes), full(bias_rows)],
        out_specs=pl.BlockSpec((B, T, W), lambda i: (0, 0, 0)),
        out_shape=jax.ShapeDtypeStruct((B, T, W), BF16),
        compiler_params=_params("arbitrary"),
        name="spatial_gate_open",
    )(u, v, w_lanes, bias_rows)


def _attn_kernel(bounded_ref, q_ref, k_ref, v_ref, o_ref, m_sc, acc_sc, *, tq, tk, td):
    i = pl.program_id(2)
    dv = v_ref.shape[-1]
    unroll = tq // tk
    nt = (((1,), (1,)), ((), ()))
    ones = jnp.ones((tk, LANES), BF16)
    d0 = pl.multiple_of(i * tq, tq)
    chunk_mask = (lax.broadcasted_iota(jnp.int32, (td, td), 1) // CHUNK
                  <= lax.broadcasted_iota(jnp.int32, (td, td), 0) // CHUNK)

    def scores(rows0, k0, keys):
        return lax.dot_general(q_ref[rows0:, :], k_ref[pl.ds(k0, keys), :], nt, preferred_element_type=F32)

    def diag_scores(c):
        r0 = c * td
        s = scores(r0, d0 + r0, td)
        top = jnp.where(chunk_mask, s[:td], -jnp.inf)
        return top if r0 + td == tq else jnp.concatenate([top, s[td:]], axis=0)

    def values(k0, keys):
        return jnp.concatenate([v_ref[pl.ds(k0, keys), :], ones[:keys]], axis=1)

    def finish():
        acc = acc_sc[...]
        o_ref[...] = (acc[:, :dv] / acc[:, dv:]).astype(o_ref.dtype)

    @pl.when(bounded_ref[0] == 1)
    def _():
        for c in range(tq // td):
            r0 = c * td
            pv = jnp.dot(jnp.exp2(diag_scores(c)).astype(BF16), values(d0 + r0, td),
                         preferred_element_type=F32)
            if c == 0:
                acc_sc[...] = pv
            else:
                acc_sc[r0:, :] += pv

        def sweep(k0, n_tiles):
            acc = acc_sc[...]
            for u in range(n_tiles):
                p = jnp.exp2(scores(0, k0 + u * tk, tk)).astype(BF16)
                acc = acc + jnp.dot(p, values(k0 + u * tk, tk), preferred_element_type=F32)
            acc_sc[...] = acc

        def body(j, carry):
            sweep(pl.multiple_of(j * (2 * tq), 2 * tq), 2 * unroll)
            return carry

        lax.fori_loop(0, i // 2, body, 0)

        @pl.when(i % 2 == 1)
        def _():
            sweep(pl.multiple_of((i - 1) * tq, tq), unroll)

        finish()

    @pl.when(bounded_ref[0] != 1)
    def _():
        def step(s, k0, m_old, acc_old):
            rows, keys = s.shape
            row_max = jnp.max(s, axis=-1, keepdims=True)
            m_new = (jnp.broadcast_to(row_max, (rows, LANES)) if m_old is None
                     else jnp.maximum(m_old, row_max))
            p = jnp.concatenate([jnp.exp2(s[:, c * LANES:(c + 1) * LANES] - m_new)
                                 for c in range(keys // LANES)], axis=1).astype(BF16)
            pv = jnp.dot(p, values(k0, keys), preferred_element_type=F32)
            if m_old is None:
                return m_new, pv
            alpha = jnp.exp2(m_old - m_new)
            return m_new, jnp.concatenate([alpha] * ((dv + LANES) // LANES), axis=1) * acc_old + pv

        for c in range(tq // td):
            r0 = c * td
            if c == 0:
                m, acc = step(diag_scores(c), d0, None, None)
            else:
                m, acc = step(diag_scores(c), d0 + r0, m_sc[r0:, :], acc_sc[r0:, :])
            m_sc[r0:, :] = m
            acc_sc[r0:, :] = acc

        def body(j, carry):
            k0 = pl.multiple_of(j * tq, tq)
            ss = [scores(0, k0 + u * tk, tk) for u in range(unroll)]
            m, acc = m_sc[...], acc_sc[...]
            for u in range(unroll):
                m, acc = step(ss[u], k0 + u * tk, m, acc)
            m_sc[...] = m
            acc_sc[...] = acc
            return carry

        lax.fori_loop(0, i, body, 0)
        finish()


def attention_prompt(bounded, q, k, v, tq, tk, td):
    G, H, S, _ = q.shape
    dv = v.shape[-1]
    assert dv == LANES and tq % tk == 0 and tq % td == 0 and td % LANES == 0 and td <= tk
    return pl.pallas_call(
        functools.partial(_attn_kernel, tq=tq, tk=tk, td=td),
        grid=(G, H, S // tq),
        in_specs=[pl.BlockSpec(memory_space=pltpu.SMEM),
                  pl.BlockSpec((None, None, tq, HEAD_PAD), lambda g, h, i: (g, h, i, 0)),
                  pl.BlockSpec((None, None, S, HEAD_PAD), lambda g, h, i: (g, h, 0, 0)),
                  pl.BlockSpec((None, None, S, dv), lambda g, h, i: (g, h, 0, 0))],
        out_specs=pl.BlockSpec((None, tq, dv), lambda g, h, i: (g, i, h)),
        out_shape=jax.ShapeDtypeStruct((G, S, H * dv), BF16),
        scratch_shapes=[pltpu.VMEM((tq, LANES), F32), pltpu.VMEM((tq, dv + LANES), F32)],
        compiler_params=_params("arbitrary", "arbitrary", "arbitrary"),
        name="attention_prompt",
    )(bounded, q, k, v)


def _attn_cached_kernel(q_ref, cache_ref, ckr_ref, cnew_ref, krnew_ref, wukt_ref, wuv_ref, gkn_ref,
                        o_ref, call_sc, krall_sc, p_sc, *, past, n_new):
    H = q_ref.shape[0]
    L = past + n_new
    Lp = call_sc.shape[0]
    lat = call_sc.shape[1]
    call_sc[0:past, :] = cache_ref[...].astype(BF16)
    call_sc[past:L, :] = cnew_ref[...].astype(BF16)
    call_sc[L:Lp, :] = jnp.zeros((Lp - L, lat), BF16)
    n_rope = ckr_ref.shape[-1]
    krall_sc[0:past, n_rope:] = jnp.zeros((past, LANES - n_rope), BF16)
    krall_sc[0:past, :n_rope] = ckr_ref[...].astype(BF16)
    krall_sc[past:L, :] = krnew_ref[...]
    krall_sc[L:Lp, :] = jnp.zeros((Lp - L, LANES), BF16)
    call = call_sc[...]
    gkn = gkn_ref[...]
    valid = lax.broadcasted_iota(jnp.int32, (n_new, Lp), 1) < L
    nt = (((1,), (1,)), ((), ()))
    q_rope = jnp.concatenate([q_ref[hh][:, LANES:] for hh in range(H)], axis=0)
    s_rope = lax.dot_general(q_rope, krall_sc[...], nt, preferred_element_type=F32)
    group = 4
    for grp in range(H // group):
        kn_t = lax.dot_general(wukt_ref[grp * group * LANES:(grp + 1) * group * LANES, :], call, nt,
                               preferred_element_type=F32)
        for sub in range(group):
            hh = group * grp + sub
            blk = kn_t[sub * LANES:(sub + 1) * LANES, :]
            inv_rms = lax.rsqrt(jnp.mean(blk * blk, axis=0, keepdims=True) + EPS)
            q_nope = (q_ref[hh][:, :LANES].astype(F32) * gkn).astype(BF16)
            s = jnp.dot(q_nope, blk.astype(BF16), preferred_element_type=F32) * inv_rms
            s = s + s_rope[hh * n_new:(hh + 1) * n_new, :]
            s = jnp.where(valid, s, -jnp.inf)
            m = jnp.max(s, axis=-1, keepdims=True)
            p = jnp.exp2(s - m)
            p = p / jnp.sum(p, axis=-1, keepdims=True)
            p_sc[hh * n_new:(hh + 1) * n_new, :] = p.astype(BF16)
    o_lat = jnp.dot(p_sc[...], call, preferred_element_type=F32).astype(BF16)
    for hh in range(H):
        o_ref[:, hh * LANES:(hh + 1) * LANES] = jnp.dot(
            o_lat[hh * n_new:(hh + 1) * n_new, :], wuv_ref[:, hh * LANES:(hh + 1) * LANES],
            preferred_element_type=F32).astype(o_ref.dtype)


def attention_cached(q, cache_lat, cache_kr, layer, c_new, kr_new_pad, w_uk_t, w_uv, g_k_nope, n_new):
    _, H, BT, _ = q.shape
    _, B, past, lat = cache_lat.shape
    n_rope = cache_kr.shape[-1]
    L = past + n_new
    Lp = -(-L // LANES) * LANES
    const = lambda b: (0, 0)
    return pl.pallas_call(
        functools.partial(_attn_cached_kernel, past=past, n_new=n_new),
        grid=(B,),
        in_specs=[pl.BlockSpec((None, H, n_new, HEAD_PAD), lambda b: (0, 0, b, 0)),
                  pl.BlockSpec((None, None, past, lat), lambda b: (layer, b, 0, 0)),
                  pl.BlockSpec((None, None, past, n_rope), lambda b: (layer, b, 0, 0)),
                  pl.BlockSpec((None, n_new, lat), lambda b: (0, b, 0)),
                  pl.BlockSpec((None, n_new, LANES), lambda b: (0, b, 0)),
                  pl.BlockSpec(w_uk_t.shape, const), pl.BlockSpec(w_uv.shape, const),
                  pl.BlockSpec((1, LANES), const)],
        out_specs=pl.BlockSpec((None, n_new, H * LANES), lambda b: (0, b, 0)),
        out_shape=jax.ShapeDtypeStruct((1, BT, H * LANES), BF16),
        scratch_shapes=[pltpu.VMEM((Lp, lat), BF16), pltpu.VMEM((Lp, LANES), BF16),
                        pltpu.VMEM((H * n_new, Lp), BF16)],
        compiler_params=_params("arbitrary"),
        name="attention_cached",
    )(q, cache_lat, cache_kr, c_new, kr_new_pad, w_uk_t, w_uv, g_k_nope.reshape(1, LANES))


def _merge_kernel(a_ref, b_ref, ga_ref, gb_ref, wpa_ref, wpb_ref, o_ref):
    pa = jnp.dot(a_ref[...], wpa_ref[...], preferred_element_type=F32)
    pb = jnp.dot(b_ref[...], wpb_ref[...], preferred_element_type=F32)
    o_ref[...] = (ga_ref[...].astype(F32) * pa + gb_ref[...].astype(F32) * pb).astype(o_ref.dtype)


def merge_branches(o_sg, o_mla, gates, w_pa, w_pb, ts, tn):
    G, S, W = o_sg.shape
    D = w_pa.shape[1]
    nj = D // tn
    row = lambda g, s, j: (g, s, 0)
    return pl.pallas_call(
        _merge_kernel,
        grid=(G, S // ts, nj),
        in_specs=[pl.BlockSpec((None, ts, W), row),
                  pl.BlockSpec((None, ts, o_mla.shape[-1]), row),
                  pl.BlockSpec((None, ts, tn), lambda g, s, j: (g, s, j)),
                  pl.BlockSpec((None, ts, tn), lambda g, s, j: (g, s, j + nj)),
                  pl.BlockSpec((W, tn), lambda g, s, j: (0, j)),
                  pl.BlockSpec((o_mla.shape[-1], tn), lambda g, s, j: (0, j))],
        out_specs=pl.BlockSpec((None, ts, tn), lambda g, s, j: (g, s, j)),
        out_shape=jax.ShapeDtypeStruct((G, S, D), BF16),
        compiler_params=_params("arbitrary", "arbitrary", "arbitrary"),
        name="merge_branches",
    )(o_sg, o_mla, gates, gates, w_pa, w_pb)


def _out_proj_kernel(m_ref, wo_ref, x_ref, g1_ref, gn_ref, sc_ref, sh_ref, x1_ref, h2_ref):
    y = jnp.dot(m_ref[...], wo_ref[...], preferred_element_type=F32)
    x1 = x_ref[...] + g1_ref[...] * y
    x1_ref[...] = x1
    h2_ref[...] = (_rms(x1, gn_ref[...]) * (1.0 + sc_ref[...]) + sh_ref[...]).astype(h2_ref.dtype)


def out_project(m, w_o, x, g1, g_norm2, sc2, sh2, ts):
    G, S, D = x.shape
    row = pl.BlockSpec((None, ts, D), lambda g, s: (g, s, 0))
    return pl.pallas_call(
        _out_proj_kernel,
        grid=(G, S // ts),
        in_specs=[row, pl.BlockSpec(w_o.shape, lambda g, s: (0, 0)), row,
                  _mod_spec(g1, ts), pl.BlockSpec((1, D), lambda g, s: (0, 0)),
                  _mod_spec(sc2, ts), _mod_spec(sh2, ts)],
        out_specs=[row, row],
        out_shape=[jax.ShapeDtypeStruct((G, S, D), F32), jax.ShapeDtypeStruct((G, S, D), BF16)],
        compiler_params=_params("arbitrary", "arbitrary"),
        name="out_project",
    )(m, w_o, x, g1, g_norm2.reshape(1, D), sc2, sh2)


def _mlp_kernel(h_ref, wup_ref, wdn_ref, x1_ref, g2_ref, o_ref, part_sc, *, n_steps):
    j = pl.program_id(2)

    def chunk():
        hid = jnp.dot(h_ref[...], wup_ref[...], preferred_element_type=F32)
        hid = jnp.square(jnp.maximum(hid, 0.0)).astype(BF16)
        return jnp.dot(hid, wdn_ref[...], preferred_element_type=F32)

    def residual(total):
        o_ref[...] = x1_ref[...] + g2_ref[...] * total

    if n_steps == 1:
        residual(chunk())
        return
    last = n_steps - 1

    @pl.when(j == 0)
    def _():
        part_sc[...] = chunk()

    if n_steps > 2:
        @pl.when(j == 1)
        def _():
            o_ref[...] = part_sc[...]
            part_sc[...] = chunk()

        @pl.when(jnp.logical_and(j > 1, j < last))
        def _():
            o_ref[...] += part_sc[...]
            part_sc[...] = chunk()

    @pl.when(j == last)
    def _():
        prev = part_sc[...] if n_steps == 2 else o_ref[...] + part_sc[...]
        residual(prev + chunk())


def mlp_residual(h2, w_up, w_down, x1, g2, ts, th):
    G, S, D = x1.shape
    hidden = w_up.shape[1]
    row = pl.BlockSpec((None, ts, D), lambda g, s, j: (g, s, 0))
    return pl.pallas_call(
        functools.partial(_mlp_kernel, n_steps=hidden // th),
        grid=(G, S // ts, hidden // th),
        in_specs=[row,
                  pl.BlockSpec((D, th), lambda g, s, j: (0, j)),
                  pl.BlockSpec((th, D), lambda g, s, j: (j, 0)),
                  row, _mod_spec(g2, ts)],
        out_specs=row,
        out_shape=jax.ShapeDtypeStruct((G, S, D), F32),
        scratch_shapes=[pltpu.VMEM((ts, D), F32)],
        compiler_params=pltpu.CompilerParams(dimension_semantics=("arbitrary",) * 3,
                                             vmem_limit_bytes=VMEM_LIMIT_MLP),
        name="mlp_residual",
    )(h2, w_up, w_down, x1, g2)


def _rope_tables(pos, n_rope):
    inv = jnp.float32(ROPE_BASE) ** (-jnp.arange(0, n_rope, 2, dtype=F32) / n_rope)
    ang = pos.astype(F32)[:, None] * inv[None, :]
    cos, sin = lax.optimization_barrier((jnp.cos(ang), jnp.sin(ang)))
    pad = jnp.zeros((pos.shape[0], LANES - n_rope), F32)
    return (jnp.concatenate([cos, cos, pad], axis=1), jnp.concatenate([-sin, sin, pad], axis=1))


def _dup_lanes(g):
    return jnp.concatenate([g.astype(F32), g.astype(F32)]).reshape(1, 2 * g.shape[0])


def _layer(x, mods, pos, P, ts, *, cache=None):
    sh1, sc1, g1, sh2, sc2, g2 = mods
    n_rope = P["n_rope"]
    tabs = _rope_tables(pos, n_rope)
    h, q = prenorm_q_project(x, P["g_norm1"], sc1, sh1, P["w_q"], P["g_q_a"], P["w_uq"], P["g_q_nope"],
                             P["g_q_rope"], tabs, ts, P["q_scale"])
    gates = proj_act(h, P["w_g"], "sigmoid", ts, P["w_g"].shape[1] // 2)
    if cache is None:
        v = proj_gelu_norm(h, P["w_v"], P["g_sg"], ts, BF16)
        o_sg = gelu_spatial_gate(h, P["w_u"], v, P["w_s"], P["b_rows"], ts)
        c_kv, k_rope, k, vv = kv_project(h, P["w_kv"], P["g_kv_a"], P["g_k_rope"], tabs, ts, n_rope,
                                         expand_weights=(P["w_uk"], P["w_uv"], P["g_k_nope"], P["k_shift"]))
        o_mla = attention_prompt(P["bounded"], q, k, vv, tq=min(2048, x.shape[1]),
                                 tk=min(1024, x.shape[1]), td=min(512, x.shape[1]))
        extra = ()
    else:
        cache_lat, cache_kr, layer, B, T = cache
        u = proj_act(h, P["w_u"], "gelu", ts, P["w_u"].shape[1])
        v = proj_gelu_norm(h, P["w_v"], P["g_sg"], ts, F32)
        W = v.shape[-1]
        o_sg = spatial_gate_open(u.reshape(B, T, W), v.reshape(B, T, W), P["w_lanes"][:T, :T],
                                 P["b_rows"][:T]).reshape(1, B * T, W)
        c_kv, k_rope, kr_pad = kv_project(h, P["w_kv"], P["g_kv_a"], P["g_k_rope"], tabs, ts, n_rope)
        o_mla = attention_cached(q, cache_lat, cache_kr, layer, c_kv, kr_pad, P["w_uk_t"], P["w_uv"],
                                 P["g_k_nope"], T)
        extra = (v,)
    m = merge_branches(o_sg, o_mla, gates, P["w_pa"], P["w_pb"], ts, P["w_pa"].shape[1] // 2)
    ts_res = min(ts, 512)
    x1, h2 = out_project(m, P["w_o"], x, g1, P["g_norm2"], sc2, sh2, ts_res)
    y = mlp_residual(h2, P["w_up"], P["w_down"], x1, g2, ts_res, min(2048, P["w_up"].shape[1]))
    return (y, c_kv, k_rope) + extra


def kernel(x_prompt, x_sample, cache_kv_latent, cache_k_rope, c_prompt, c_sample, w_ada, b_ada, g_norm1, g_norm2, w_in, g_sg, w_s, b_s, g_q_a, w_uq, g_q_nope, g_q_rope, g_kv_a, g_k_rope, w_uk, g_k_nope, w_uv, w_pa, w_pb, w_o, w_up, w_down):
    depth = w_in.shape[0]
    Bp, S, D = x_prompt.shape
    Bs, T, _ = x_sample.shape
    past = cache_kv_latent.shape[2]
    q_lora = g_q_a.shape[-1]
    lat = g_kv_a.shape[-1]
    n_nope = g_q_nope.shape[-1]
    n_rope = g_q_rope.shape[-1]
    H = w_uk.shape[2]
    sg_w = g_sg.shape[-1]
    off_q = 2 * sg_w
    off_kv = off_q + q_lora
    off_gate = off_kv + lat + n_rope
    assert n_nope == LANES and w_uv.shape[-1] == LANES and 2 * n_rope == LANES
    assert sg_w // SG_GROUPS == LANES and H == N_HEADS

    nb = Bp + Bs
    nb_pad = -(-nb // 8) * 8
    c_all = jnp.concatenate([c_prompt, c_sample, jnp.zeros((nb_pad - nb, D), F32)], axis=0)

    y_p, y_s = x_prompt, x_sample.reshape(1, Bs * T, D)
    outs = [[] for _ in range(5)]
    for l in range(depth):
        def w_in_cols(a, b):
            return w_in[l][:, a:b].astype(BF16)
        w_uq_l = w_uq[l].astype(BF16).reshape(q_lora, H, n_nope + n_rope)
        w_uq_pad = jnp.concatenate([w_uq_l, w_uq_l[:, :, n_nope:]], axis=2)
        q_scale = float((n_nope + n_rope) ** -0.5 * LOG2E)
        def sq_norm_bound(g_nope, g_rope):
            return n_nope * jnp.max(jnp.square(g_nope)) + n_rope * jnp.max(jnp.square(g_rope))
        score_bound = q_scale * jnp.sqrt(sq_norm_bound(g_q_nope[l], g_q_rope[l])
                                         * sq_norm_bound(g_k_nope[l], g_k_rope[l]))
        bounded = (score_bound <= BOUND_MAX).astype(jnp.int32).reshape(1)
        k_shift = jnp.where(jnp.arange(LANES) == LANES // 2, -score_bound, 0.0).astype(F32).reshape(1, LANES)
        P = {
            "n_rope": n_rope,
            "q_scale": q_scale, "k_shift": k_shift, "bounded": bounded,
            "g_norm1": g_norm1[l], "g_norm2": g_norm2[l], "g_sg": g_sg[l],
            "w_u": w_in_cols(0, sg_w), "w_v": w_in_cols(sg_w, off_q), "w_q": w_in_cols(off_q, off_kv),
            "w_kv": jnp.concatenate([w_in_cols(off_kv, off_gate), w_in_cols(off_kv + lat, off_gate)], axis=1),
            "w_g": w_in_cols(off_gate, w_in.shape[2]),
            "g_q_a": g_q_a[l], "w_uq": w_uq_pad.reshape(q_lora, H * HEAD_PAD),
            "g_q_nope": g_q_nope[l], "g_q_rope": _dup_lanes(g_q_rope[l]),
            "g_kv_a": g_kv_a[l], "g_k_rope": _dup_lanes(g_k_rope[l]),
            "w_uk": w_uk[l].astype(BF16).reshape(lat, H * n_nope),
            "w_uk_t": w_uk[l].astype(BF16).reshape(lat, H * n_nope).T,
            "w_uv": w_uv[l].astype(BF16).reshape(lat, H * LANES),
            "g_k_nope": g_k_nope[l],
            "w_s": w_s[l],
            "b_rows": jnp.repeat(b_s[l].T, LANES, axis=1),
            "w_lanes": jnp.repeat(w_s[l][:, :T, :T].transpose(1, 2, 0), LANES, axis=2),
            "w_pa": w_pa[l].astype(BF16), "w_pb": w_pb[l].astype(BF16), "w_o": w_o[l].astype(BF16),
            "w_up": w_up[l].astype(BF16), "w_down": w_down[l].astype(BF16),
        }
        mod = ada_project(c_all, w_ada[l], b_ada[l])
        mods_p = [a.reshape(Bp, 1, D) for a in jnp.split(mod[:Bp], 6, axis=-1)]
        mods_s = [jnp.repeat(a, T, axis=0).reshape(1, Bs * T, D)
                  for a in jnp.split(mod[Bp:nb], 6, axis=-1)]

        ts_p = min(1024, S)
        y_p, lp, kp = _layer(y_p, mods_p, jnp.arange(S), P, ts_p)
        pos_s = jnp.tile(past + jnp.arange(T), Bs)
        y_s, ls, ks, vs = _layer(y_s, mods_s, pos_s, P, Bs * T,
                                 cache=(cache_kv_latent, cache_k_rope, l, Bs, T))
        for lst, a in zip(outs, (lp, kp, ls.reshape(Bs, T, lat), ks.reshape(Bs, T, n_rope),
                                 vs.reshape(Bs, T, sg_w))):
            lst.append(a)
    return (y_p, y_s.reshape(Bs, T, D)) + tuple(o[0][None] if depth == 1 else jnp.stack(o) for o in outs)
```

```python
import functools
import math

import jax
import jax.numpy as jnp
import numpy as np
from jax import lax
from jax.experimental import pallas as pl
from jax.experimental.pallas import tpu as pltpu

F32 = jnp.float32
BF16 = jnp.bfloat16

EPS = 1e-6
ROPE_BASE = 10000.0
N_HEADS = 16
CHUNK = 64
SG_CHUNK = 128
SG_GROUPS = 16
LANES = 128
HEAD_PAD = 256
LOG2E = math.log2(math.e)
BOUND_MAX = 50.0

VMEM_LIMIT = 58 * 1024 * 1024
VMEM_LIMIT_MLP = 61 * 1024 * 1024


def _params(*sem):
    return pltpu.CompilerParams(dimension_semantics=sem, vmem_limit_bytes=VMEM_LIMIT)


def _rms(x, g):
    ms = jnp.mean(x * x, axis=-1, keepdims=True)
    return x * lax.rsqrt(ms + EPS) * g


def _rope_group(t, tc, ts):
    return t * tc + pltpu.roll(t, LANES - LANES // 4, 1) * ts


def _gelu(z):
    return 0.5 * z * (1.0 + lax.erf(z * np.float32(math.sqrt(0.5))))


def _mod_spec(mod, ts):
    d = mod.shape[-1]
    if mod.shape[1] == 1:
        return pl.BlockSpec((None, 1, d), lambda g, s, *_: (g, 0, 0))
    return pl.BlockSpec((None, ts, d), lambda g, s, *_: (g, s, 0))


def _ada_kernel(c_ref, w_ref, b_ref, o_ref):
    c = c_ref[...]
    s = (c * jax.nn.sigmoid(c)).astype(BF16)
    o_ref[...] = jnp.dot(s, w_ref[...].astype(BF16), preferred_element_type=F32) + b_ref[...]


def ada_project(c, w_ada, b_ada, tn=1024):
    r, d = c.shape
    n = w_ada.shape[1]
    return pl.pallas_call(
        _ada_kernel,
        grid=(n // tn,),
        in_specs=[pl.BlockSpec((r, d), lambda j: (0, 0)),
                  pl.BlockSpec((d, tn), lambda j: (0, j)),
                  pl.BlockSpec((1, tn), lambda j: (0, j))],
        out_specs=pl.BlockSpec((r, tn), lambda j: (0, j)),
        out_shape=jax.ShapeDtypeStruct((r, n), F32),
        compiler_params=_params("arbitrary"),
        name="ada_project",
    )(c, w_ada, b_ada.reshape(1, n))


def _proj_act_kernel(h_ref, w_ref, o_ref, *, act):
    z = jnp.dot(h_ref[...], w_ref[...], preferred_element_type=F32)
    if act == "gelu":
        a = _gelu(z)
    else:
        a = 0.5 * (1.0 + jnp.tanh(0.5 * z))
    o_ref[...] = a.astype(o_ref.dtype)


def proj_act(h, w, act, ts, tn):
    G, S, D = h.shape
    n = w.shape[1]
    return pl.pallas_call(
        functools.partial(_proj_act_kernel, act=act),
        grid=(G, S // ts, n // tn),
        in_specs=[pl.BlockSpec((None, ts, D), lambda g, s, j: (g, s, 0)),
                  pl.BlockSpec((D, tn), lambda g, s, j: (0, j))],
        out_specs=pl.BlockSpec((None, ts, tn), lambda g, s, j: (g, s, j)),
        out_shape=jax.ShapeDtypeStruct((G, S, n), BF16),
        compiler_params=_params("arbitrary", "arbitrary", "arbitrary"),
        name="proj_" + act,
    )(h, w)


def _proj_gelu_norm_kernel(h_ref, w_ref, g_ref, o_ref):
    z = jnp.dot(h_ref[...], w_ref[...], preferred_element_type=F32)
    o_ref[...] = _rms(_gelu(z), g_ref[...]).astype(o_ref.dtype)


def proj_gelu_norm(h, w, g, ts, out_dtype):
    G, S, D = h.shape
    n = w.shape[1]
    return pl.pallas_call(
        _proj_gelu_norm_kernel,
        grid=(G, S // ts),
        in_specs=[pl.BlockSpec((None, ts, D), lambda g_, s: (g_, s, 0)),
                  pl.BlockSpec((D, n), lambda g_, s: (0, 0)),
                  pl.BlockSpec((1, n), lambda g_, s: (0, 0))],
        out_specs=pl.BlockSpec((None, ts, n), lambda g_, s: (g_, s, 0)),
        out_shape=jax.ShapeDtypeStruct((G, S, n), out_dtype),
        compiler_params=_params("arbitrary", "arbitrary"),
        name="proj_gelu_norm",
    )(h, w, g.reshape(1, n))


def _q_proj_kernel(x_ref, g1_ref, sc_ref, sh_ref, wq_ref, gqa_ref, wuq_ref, gn_ref, gr_ref, tc_ref, ts_ref,
                   h_ref, q_ref, *, q_scale):
    h = (_rms(x_ref[...], g1_ref[...]) * (1.0 + sc_ref[...]) + sh_ref[...]).astype(h_ref.dtype)
    h_ref[...] = h
    zq = jnp.dot(h, wq_ref[...], preferred_element_type=F32)
    zn = _rms(zq, gqa_ref[...]).astype(BF16)
    tc, ts = tc_ref[...], ts_ref[...]
    gains = jnp.concatenate([gn_ref[...], gr_ref[...]], axis=1) * q_scale
    same_tile = (lax.broadcasted_iota(jnp.int32, (HEAD_PAD, HEAD_PAD), 0) // LANES
                 == lax.broadcasted_iota(jnp.int32, (HEAD_PAD, HEAD_PAD), 1) // LANES)
    tile_mean = jnp.where(same_tile, 1.0 / LANES, 0.0).astype(BF16)
    shift_lane = jnp.where(lax.broadcasted_iota(jnp.int32, (1, LANES), 1) == LANES // 2, 1.0, 0.0)
    for pair in range(q_ref.shape[0] // 2):
        blk2 = jnp.dot(zn, wuq_ref[:, pair * 2 * HEAD_PAD:(pair + 1) * 2 * HEAD_PAD],
                       preferred_element_type=F32)
        for sub in range(2):
            hh = 2 * pair + sub
            blk = blk2[:, sub * HEAD_PAD:(sub + 1) * HEAD_PAD]
            ms = jnp.dot((blk * blk).astype(BF16), tile_mean, preferred_element_type=F32)
            y = blk * lax.rsqrt(ms + EPS) * gains
            q_ref[hh, :, :LANES] = y[:, :LANES].astype(q_ref.dtype)
            q_ref[hh, :, LANES:] = (_rope_group(y[:, LANES:], tc, ts) + shift_lane).astype(q_ref.dtype)


def prenorm_q_project(x, g_norm, sc, sh, w_q, g_q_a, w_uq_pad, g_nope, g_rope_pad, tabs, ts, q_scale):
    G, S, D = x.shape
    ql = w_q.shape[1]
    H = w_uq_pad.shape[1] // HEAD_PAD
    const = lambda g, s: (0, 0)
    tab_spec = pl.BlockSpec((ts, LANES), lambda g, s: (s, 0))
    row = pl.BlockSpec((None, ts, D), lambda g, s: (g, s, 0))
    return pl.pallas_call(
        functools.partial(_q_proj_kernel, q_scale=q_scale),
        grid=(G, S // ts),
        in_specs=[row, pl.BlockSpec((1, D), const), _mod_spec(sc, ts), _mod_spec(sh, ts),
                  pl.BlockSpec((D, ql), const),
                  pl.BlockSpec((1, ql), const),
                  pl.BlockSpec((ql, H * HEAD_PAD), const),
                  pl.BlockSpec((1, LANES), const),
                  pl.BlockSpec((1, LANES), const),
                  tab_spec, tab_spec],
        out_specs=[row, pl.BlockSpec((None, H, ts, HEAD_PAD), lambda g, s: (g, 0, s, 0))],
        out_shape=[jax.ShapeDtypeStruct((G, S, D), BF16),
                   jax.ShapeDtypeStruct((G, H, S, HEAD_PAD), BF16)],
        compiler_params=_params("arbitrary", "arbitrary"),
        name="prenorm_q_project",
    )(x, g_norm.reshape(1, D), sc, sh, w_q, g_q_a.reshape(1, ql), w_uq_pad, g_nope.reshape(1, LANES),
      g_rope_pad, *tabs)


def _kv_proj_kernel(h_ref, wkv_ref, gkva_ref, gkr_ref, tc_ref, ts_ref, *rest, n_rope, expand):
    if expand:
        wuk_ref, wuv_ref, gkn_ref, kshift_ref, c_ref, kr_ref, k_ref, v_ref = rest
    else:
        c_ref, kr_ref, krp_ref = rest
    lat = c_ref.shape[-1]
    z = jnp.dot(h_ref[...], wkv_ref[...], preferred_element_type=F32)
    c = _rms(z[:, :lat], gkva_ref[...])
    c_ref[...] = c
    rope = _rope_group(_rms(z[:, lat:], gkr_ref[...]), tc_ref[...], ts_ref[...])
    kr_ref[...] = rope[:, :n_rope]
    if not expand:
        krp_ref[...] = rope.astype(BF16)
        return
    cb = c.astype(BF16)
    gkn = gkn_ref[...]
    rope_b = (rope + kshift_ref[...]).astype(BF16)
    for pair in range(k_ref.shape[0] // 2):
        cols = slice(pair * 2 * LANES, (pair + 1) * 2 * LANES)
        kn2 = jnp.dot(cb, wuk_ref[:, cols], preferred_element_type=F32)
        v2 = jnp.dot(cb, wuv_ref[:, cols], preferred_element_type=F32)
        for sub in range(2):
            hh = 2 * pair + sub
            lanes = slice(sub * LANES, (sub + 1) * LANES)
            k_ref[hh, :, :LANES] = _rms(kn2[:, lanes], gkn).astype(k_ref.dtype)
            k_ref[hh, :, LANES:] = rope_b
            v_ref[hh] = v2[:, lanes].astype(v_ref.dtype)


def kv_project(h, w_kv, g_kv_a, g_k_rope_pad, tabs, ts, n_rope, expand_weights=None):
    G, S, D = h.shape
    lat = g_kv_a.shape[-1]
    const = lambda g, s: (0, 0)
    tab_spec = pl.BlockSpec((ts, LANES), lambda g, s: (s, 0))
    in_specs = [pl.BlockSpec((None, ts, D), lambda g, s: (g, s, 0)),
                pl.BlockSpec((D, lat + LANES), const),
                pl.BlockSpec((1, lat), const),
                pl.BlockSpec((1, LANES), const),
                tab_spec, tab_spec]
    args = [h, w_kv, g_kv_a.reshape(1, lat), g_k_rope_pad, *tabs]
    out_specs = [pl.BlockSpec((None, ts, lat), lambda g, s: (g, s, 0)),
                 pl.BlockSpec((None, ts, n_rope), lambda g, s: (g, s, 0))]
    out_shape = [jax.ShapeDtypeStruct((G, S, lat), F32),
                 jax.ShapeDtypeStruct((G, S, n_rope), F32)]
    expand = expand_weights is not None
    if expand:
        w_uk, w_uv, g_k_nope, k_shift = expand_weights
        H = w_uk.shape[1] // LANES
        in_specs += [pl.BlockSpec(w_uk.shape, const), pl.BlockSpec(w_uv.shape, const),
                     pl.BlockSpec((1, LANES), const), pl.BlockSpec((1, LANES), const)]
        args += [w_uk, w_uv, g_k_nope.reshape(1, LANES), k_shift]
        out_specs += [pl.BlockSpec((None, H, ts, HEAD_PAD), lambda g, s: (g, 0, s, 0)),
                      pl.BlockSpec((None, H, ts, LANES), lambda g, s: (g, 0, s, 0))]
        out_shape += [jax.ShapeDtypeStruct((G, H, S, HEAD_PAD), BF16),
                      jax.ShapeDtypeStruct((G, H, S, LANES), BF16)]
    else:
        out_specs.append(pl.BlockSpec((None, ts, LANES), lambda g, s: (g, s, 0)))
        out_shape.append(jax.ShapeDtypeStruct((G, S, LANES), BF16))
    return pl.pallas_call(
        functools.partial(_kv_proj_kernel, n_rope=n_rope, expand=expand),
        grid=(G, S // ts),
        in_specs=in_specs,
        out_specs=out_specs,
        out_shape=out_shape,
        compiler_params=_params("arbitrary", "arbitrary"),
        name="kv_project",
    )(*args)


def _gelu_gate_kernel(h_ref, wu_ref, v_ref, w_ref, b_ref, o_ref, wm_sc, u_sc, *, n_chunks):
    u_sc[...] = _gelu(jnp.dot(h_ref[...], wu_ref[...], preferred_element_type=F32)).astype(u_sc.dtype)
    row = lax.broadcasted_iota(jnp.int32, (SG_CHUNK, SG_CHUNK), 0)
    col = lax.broadcasted_iota(jnp.int32, (SG_CHUNK, SG_CHUNK), 1)
    tril = col <= row
    n_groups = w_ref.shape[0]
    for g in range(n_groups):
        wm_sc[g] = jnp.where(tril, w_ref[g], 0.0).astype(BF16)

    def chunk_body(n, carry):
        rows = pl.ds(pl.multiple_of(n * SG_CHUNK, SG_CHUNK), SG_CHUNK)
        for g in range(n_groups):
            lanes = slice(g * LANES, (g + 1) * LANES)
            mix = jnp.dot(wm_sc[g], v_ref[rows, lanes], preferred_element_type=F32) + b_ref[:, lanes]
            o_ref[rows, lanes] = (u_sc[rows, lanes].astype(F32) * mix).astype(o_ref.dtype)
        return carry

    lax.fori_loop(0, n_chunks, chunk_body, 0)


def gelu_spatial_gate(h, w_u, v, w_s, bias_rows, ts):
    G, S, D = h.shape
    W = w_u.shape[1]
    blk = pl.BlockSpec((None, ts, W), lambda g, s: (g, s, 0))
    return pl.pallas_call(
        functools.partial(_gelu_gate_kernel, n_chunks=ts // SG_CHUNK),
        grid=(G, S // ts),
        in_specs=[pl.BlockSpec((None, ts, D), lambda g, s: (g, s, 0)),
                  pl.BlockSpec(w_u.shape, lambda g, s: (0, 0)),
                  blk,
                  pl.BlockSpec(w_s.shape, lambda g, s: (0, 0, 0)),
                  pl.BlockSpec(bias_rows.shape, lambda g, s: (0, 0))],
        out_specs=blk,
        out_shape=jax.ShapeDtypeStruct((G, S, W), BF16),
        scratch_shapes=[pltpu.VMEM(w_s.shape, BF16), pltpu.VMEM((ts, W), BF16)],
        compiler_params=_params("arbitrary", "arbitrary"),
        name="gelu_spatial_gate",
    )(h, w_u, v, w_s, bias_rows)


def _spatial_gate_open_kernel(u_ref, v_ref, wl_ref, b_ref, o_ref):
    T = u_ref.shape[1]
    for i in range(T):
        acc = b_ref[i:i + 1, :] + wl_ref[i, 0:1, :] * v_ref[:, 0, :]
        for j in range(1, i + 1):
            acc = acc + wl_ref[i, j:j + 1, :] * v_ref[:, j, :]
        o_ref[:, i, :] = (u_ref[:, i, :].astype(F32) * acc).astype(o_ref.dtype)


def spatial_gate_open(u, v, w_lanes, bias_rows):
    B, T, W = u.shape
    full = lambda a: pl.BlockSpec(a.shape, lambda i: (0,) * a.ndim)
    return pl.pallas_call(
        _spatial_gate_open_kernel,
        grid=(1,),
        in_specs=[full(u), full(v), full(w_lanes), full(bias_rows)],
        out_specs=pl.BlockSpec((B, T, W), lambda i: (0, 0, 0)),
        out_shape=jax.ShapeDtypeStruct((B, T, W), BF16),
        compiler_params=_params("arbitrary"),
        name="spatial_gate_open",
    )(u, v, w_lanes, bias_rows)


def _attn_kernel(bounded_ref, q_ref, k_ref, v_ref, o_ref, m_sc, acc_sc, *, tq, tk, td):
    i = pl.program_id(2)
    dv = v_ref.shape[-1]
    unroll = tq // tk
    nt = (((1,), (1,)), ((), ()))
    ones = jnp.ones((tk, LANES), BF16)
    d0 = pl.multiple_of(i * tq, tq)
    chunk_mask = (lax.broadcasted_iota(jnp.int32, (td, td), 1) // CHUNK
                  <= lax.broadcasted_iota(jnp.int32, (td, td), 0) // CHUNK)

    def scores(rows0, k0, keys):
        return lax.dot_general(q_ref[rows0:, :], k_ref[pl.ds(k0, keys), :], nt, preferred_element_type=F32)

    def diag_scores(c):
        r0 = c * td
        s = scores(r0, d0 + r0, td)
        top = jnp.where(chunk_mask, s[:td], -jnp.inf)
        return top if r0 + td == tq else jnp.concatenate([top, s[td:]], axis=0)

    def values(k0, keys):
        return jnp.concatenate([v_ref[pl.ds(k0, keys), :], ones[:keys]], axis=1)

    def finish():
        acc = acc_sc[...]
        o_ref[...] = (acc[:, :dv] / acc[:, dv:]).astype(o_ref.dtype)

    @pl.when(bounded_ref[0] == 1)
    def _():
        for c in range(tq // td):
            r0 = c * td
            pv = jnp.dot(jnp.exp2(diag_scores(c)).astype(BF16), values(d0 + r0, td),
                         preferred_element_type=F32)
            if c == 0:
                acc_sc[...] = pv
            else:
                acc_sc[r0:, :] += pv

        def sweep(k0, n_tiles):
            acc = acc_sc[...]
            for u in range(n_tiles):
                p = jnp.exp2(scores(0, k0 + u * tk, tk)).astype(BF16)
                acc = acc + jnp.dot(p, values(k0 + u * tk, tk), preferred_element_type=F32)
            acc_sc[...] = acc

        def body(j, carry):
            sweep(pl.multiple_of(j * (2 * tq), 2 * tq), 2 * unroll)
            return carry

        lax.fori_loop(0, i // 2, body, 0)

        @pl.when(i % 2 == 1)
        def _():
            sweep(pl.multiple_of((i - 1) * tq, tq), unroll)

        finish()

    @pl.when(bounded_ref[0] != 1)
    def _():
        def step(s, k0, m_old, acc_old):
            rows, keys = s.shape
            row_max = jnp.max(s, axis=-1, keepdims=True)
            m_new = (jnp.broadcast_to(row_max, (rows, LANES)) if m_old is None
                     else jnp.maximum(m_old, row_max))
            p = jnp.concatenate([jnp.exp2(s[:, c * LANES:(c + 1) * LANES] - m_new)
                                 for c in range(keys // LANES)], axis=1).astype(BF16)
            pv = jnp.dot(p, values(k0, keys), preferred_element_type=F32)
            if m_old is None:
                return m_new, pv
            alpha = jnp.exp2(m_old - m_new)
            return m_new, jnp.concatenate([alpha] * ((dv + LANES) // LANES), axis=1) * acc_old + pv

        for c in range(tq // td):
            r0 = c * td
            if c == 0:
                m, acc = step(diag_scores(c), d0, None, None)
            else:
                m, acc = step(diag_scores(c), d0 + r0, m_sc[r0:, :], acc_sc[r0:, :])
            m_sc[r0:, :] = m
            acc_sc[r0:, :] = acc

        def body(j, carry):
            k0 = pl.multiple_of(j * tq, tq)
            ss = [scores(0, k0 + u * tk, tk) for u in range(unroll)]
            m, acc = m_sc[...], acc_sc[...]
            for u in range(unroll):
                m, acc = step(ss[u], k0 + u * tk, m, acc)
            m_sc[...] = m
            acc_sc[...] = acc
            return carry

        lax.fori_loop(0, i, body, 0)
        finish()


def attention_prompt(bounded, q, k, v, tq, tk, td):
    G, H, S, _ = q.shape
    dv = v.shape[-1]
    assert dv == LANES and tq % tk == 0 and tq % td == 0 and td % LANES == 0 and td <= tk
    return pl.pallas_call(
        functools.partial(_attn_kernel, tq=tq, tk=tk, td=td),
        grid=(G, H, S // tq),
        in_specs=[pl.BlockSpec(memory_space=pltpu.SMEM),
                  pl.BlockSpec((None, None, tq, HEAD_PAD), lambda g, h, i: (g, h, i, 0)),
                  pl.BlockSpec((None, None, S, HEAD_PAD), lambda g, h, i: (g, h, 0, 0)),
                  pl.BlockSpec((None, None, S, dv), lambda g, h, i: (g, h, 0, 0))],
        out_specs=pl.BlockSpec((None, tq, dv), lambda g, h, i: (g, i, h)),
        out_shape=jax.ShapeDtypeStruct((G, S, H * dv), BF16),
        scratch_shapes=[pltpu.VMEM((tq, LANES), F32), pltpu.VMEM((tq, dv + LANES), F32)],
        compiler_params=_params("arbitrary", "arbitrary", "arbitrary"),
        name="attention_prompt",
    )(bounded, q, k, v)


def _attn_cached_kernel(q_ref, cache_ref, ckr_ref, cnew_ref, krnew_ref, wukt_ref, wuv_ref, gkn_ref,
                        o_ref, call_sc, krall_sc, p_sc, *, past, n_new):
    H = q_ref.shape[0]
    L = past + n_new
    Lp = call_sc.shape[0]
    lat = call_sc.shape[1]
    call_sc[0:past, :] = cache_ref[...].astype(BF16)
    call_sc[past:L, :] = cnew_ref[...].astype(BF16)
    call_sc[L:Lp, :] = jnp.zeros((Lp - L, lat), BF16)
    n_rope = ckr_ref.shape[-1]
    krall_sc[0:past, n_rope:] = jnp.zeros((past, LANES - n_rope), BF16)
    krall_sc[0:past, :n_rope] = ckr_ref[...].astype(BF16)
    krall_sc[past:L, :] = krnew_ref[...]
    krall_sc[L:Lp, :] = jnp.zeros((Lp - L, LANES), BF16)
    call = call_sc[...]
    gkn = gkn_ref[...]
    valid = lax.broadcasted_iota(jnp.int32, (n_new, Lp), 1) < L
    nt = (((1,), (1,)), ((), ()))
    q_rope = jnp.concatenate([q_ref[hh][:, LANES:] for hh in range(H)], axis=0)
    s_rope = lax.dot_general(q_rope, krall_sc[...], nt, preferred_element_type=F32)
    group = 4
    for grp in range(H // group):
        kn_t = lax.dot_general(wukt_ref[grp * group * LANES:(grp + 1) * group * LANES, :], call, nt,
                               preferred_element_type=F32)
        for sub in range(group):
            hh = group * grp + sub
            blk = kn_t[sub * LANES:(sub + 1) * LANES, :]
            inv_rms = lax.rsqrt(jnp.mean(blk * blk, axis=0, keepdims=True) + EPS)
            q_nope = (q_ref[hh][:, :LANES].astype(F32) * gkn).astype(BF16)
            s = jnp.dot(q_nope, blk.astype(BF16), preferred_element_type=F32) * inv_rms
            s = s + s_rope[hh * n_new:(hh + 1) * n_new, :]
            s = jnp.where(valid, s, -jnp.inf)
            m = jnp.max(s, axis=-1, keepdims=True)
            p = jnp.exp2(s - m)
            p = p / jnp.sum(p, axis=-1, keepdims=True)
            p_sc[hh * n_new:(hh + 1) * n_new, :] = p.astype(BF16)
    o_lat = jnp.dot(p_sc[...], call, preferred_element_type=F32).astype(BF16)
    for hh in range(H):
        o_ref[:, hh * LANES:(hh + 1) * LANES] = jnp.dot(
            o_lat[hh * n_new:(hh + 1) * n_new, :], wuv_ref[:, hh * LANES:(hh + 1) * LANES],
            preferred_element_type=F32).astype(o_ref.dtype)


def attention_cached(q, cache_lat, cache_kr, layer, c_new, kr_new_pad, w_uk_t, w_uv, g_k_nope, n_new):
    _, H, BT, _ = q.shape
    _, B, past, lat = cache_lat.shape
    n_rope = cache_kr.shape[-1]
    L = past + n_new
    Lp = -(-L // LANES) * LANES
    const = lambda b: (0, 0)
    return pl.pallas_call(
        functools.partial(_attn_cached_kernel, past=past, n_new=n_new),
        grid=(B,),
        in_specs=[pl.BlockSpec((None, H, n_new, HEAD_PAD), lambda b: (0, 0, b, 0)),
                  pl.BlockSpec((None, None, past, lat), lambda b: (layer, b, 0, 0)),
                  pl.BlockSpec((None, None, past, n_rope), lambda b: (layer, b, 0, 0)),
                  pl.BlockSpec((None, n_new, lat), lambda b: (0, b, 0)),
                  pl.BlockSpec((None, n_new, LANES), lambda b: (0, b, 0)),
                  pl.BlockSpec(w_uk_t.shape, const), pl.BlockSpec(w_uv.shape, const),
                  pl.BlockSpec((1, LANES), const)],
        out_specs=pl.BlockSpec((None, n_new, H * LANES), lambda b: (0, b, 0)),
        out_shape=jax.ShapeDtypeStruct((1, BT, H * LANES), BF16),
        scratch_shapes=[pltpu.VMEM((Lp, lat), BF16), pltpu.VMEM((Lp, LANES), BF16),
                        pltpu.VMEM((H * n_new, Lp), BF16)],
        compiler_params=_params("arbitrary"),
        name="attention_cached",
    )(q, cache_lat, cache_kr, c_new, kr_new_pad, w_uk_t, w_uv, g_k_nope.reshape(1, LANES))


def _merge_kernel(a_ref, b_ref, ga_ref, gb_ref, wpa_ref, wpb_ref, o_ref):
    pa = jnp.dot(a_ref[...], wpa_ref[...], preferred_element_type=F32)
    pb = jnp.dot(b_ref[...], wpb_ref[...], preferred_element_type=F32)
    o_ref[...] = (ga_ref[...].astype(F32) * pa + gb_ref[...].astype(F32) * pb).astype(o_ref.dtype)


def merge_branches(o_sg, o_mla, gates, w_pa, w_pb, ts, tn):
    G, S, W = o_sg.shape
    D = w_pa.shape[1]
    nj = D // tn
    row = lambda g, j, s: (g, s, 0)
    return pl.pallas_call(
        _merge_kernel,
        grid=(G, nj, S // ts),
        in_specs=[pl.BlockSpec((None, ts, W), row),
                  pl.BlockSpec((None, ts, o_mla.shape[-1]), row),
                  pl.BlockSpec((None, ts, tn), lambda g, j, s: (g, s, j)),
                  pl.BlockSpec((None, ts, tn), lambda g, j, s: (g, s, j + nj)),
                  pl.BlockSpec((W, tn), lambda g, j, s: (0, j)),
                  pl.BlockSpec((o_mla.shape[-1], tn), lambda g, j, s: (0, j))],
        out_specs=pl.BlockSpec((None, ts, tn), lambda g, j, s: (g, s, j)),
        out_shape=jax.ShapeDtypeStruct((G, S, D), BF16),
        compiler_params=_params("arbitrary", "arbitrary", "arbitrary"),
        name="merge_branches",
    )(o_sg, o_mla, gates, gates, w_pa, w_pb)


def _out_proj_kernel(m_ref, wo_ref, x_ref, g1_ref, gn_ref, sc_ref, sh_ref, x1_ref, h2_ref):
    y = jnp.dot(m_ref[...], wo_ref[...], preferred_element_type=F32)
    x1 = x_ref[...] + g1_ref[...] * y
    x1_ref[...] = x1
    h2_ref[...] = (_rms(x1, gn_ref[...]) * (1.0 + sc_ref[...]) + sh_ref[...]).astype(h2_ref.dtype)


def out_project(m, w_o, x, g1, g_norm2, sc2, sh2, ts):
    G, S, D = x.shape
    row = pl.BlockSpec((None, ts, D), lambda g, s: (g, s, 0))
    return pl.pallas_call(
        _out_proj_kernel,
        grid=(G, S // ts),
        in_specs=[row, pl.BlockSpec(w_o.shape, lambda g, s: (0, 0)), row,
                  _mod_spec(g1, ts), pl.BlockSpec((1, D), lambda g, s: (0, 0)),
                  _mod_spec(sc2, ts), _mod_spec(sh2, ts)],
        out_specs=[row, row],
        out_shape=[jax.ShapeDtypeStruct((G, S, D), F32), jax.ShapeDtypeStruct((G, S, D), BF16)],
        compiler_params=_params("arbitrary", "arbitrary"),
        name="out_project",
    )(m, w_o, x, g1, g_norm2.reshape(1, D), sc2, sh2)


def _mlp_kernel(h_ref, wup_ref, wdn_ref, x1_ref, g2_ref, o_ref, part_sc, *, n_steps):
    j = pl.program_id(2)

    def chunk():
        hid = jnp.dot(h_ref[...], wup_ref[...], preferred_element_type=F32)
        hid = jnp.square(jnp.maximum(hid, 0.0)).astype(BF16)
        return jnp.dot(hid, wdn_ref[...], preferred_element_type=F32)

    def residual(total):
        o_ref[...] = x1_ref[...] + g2_ref[...] * total

    if n_steps == 1:
        residual(chunk())
        return
    last = n_steps - 1

    @pl.when(j == 0)
    def _():
        part_sc[...] = chunk()

    if n_steps > 2:
        @pl.when(j == 1)
        def _():
            o_ref[...] = part_sc[...]
            part_sc[...] = chunk()

        @pl.when(jnp.logical_and(j > 1, j < last))
        def _():
            o_ref[...] += part_sc[...]
            part_sc[...] = chunk()

    @pl.when(j == last)
    def _():
        prev = part_sc[...] if n_steps == 2 else o_ref[...] + part_sc[...]
        residual(prev + chunk())


def mlp_residual(h2, w_up, w_down, x1, g2, ts, th):
    G, S, D = x1.shape
    hidden = w_up.shape[1]
    row = pl.BlockSpec((None, ts, D), lambda g, s, j: (g, s, 0))
    return pl.pallas_call(
        functools.partial(_mlp_kernel, n_steps=hidden // th),
        grid=(G, S // ts, hidden // th),
        in_specs=[row,
                  pl.BlockSpec((D, th), lambda g, s, j: (0, j)),
                  pl.BlockSpec((th, D), lambda g, s, j: (j, 0)),
                  row, _mod_spec(g2, ts)],
        out_specs=row,
        out_shape=jax.ShapeDtypeStruct((G, S, D), F32),
        scratch_shapes=[pltpu.VMEM((ts, D), F32)],
        compiler_params=pltpu.CompilerParams(dimension_semantics=("arbitrary",) * 3,
                                             vmem_limit_bytes=VMEM_LIMIT_MLP),
        name="mlp_residual",
    )(h2, w_up, w_down, x1, g2)


def _rope_tables(pos, n_rope):
    inv = jnp.float32(ROPE_BASE) ** (-jnp.arange(0, n_rope, 2, dtype=F32) / n_rope)
    ang = pos.astype(F32)[:, None] * inv[None, :]
    cos, sin = lax.optimization_barrier((jnp.cos(ang), jnp.sin(ang)))
    pad = jnp.zeros((pos.shape[0], LANES - n_rope), F32)
    return (jnp.concatenate([cos, cos, pad], axis=1), jnp.concatenate([-sin, sin, pad], axis=1))


def _dup_lanes(g):
    return jnp.concatenate([g.astype(F32), g.astype(F32)]).reshape(1, 2 * g.shape[0])


def _layer(x, mods, pos, P, ts, *, cache=None):
    sh1, sc1, g1, sh2, sc2, g2 = mods
    n_rope = P["n_rope"]
    tabs = _rope_tables(pos, n_rope)
    h, q = prenorm_q_project(x, P["g_norm1"], sc1, sh1, P["w_q"], P["g_q_a"], P["w_uq"], P["g_q_nope"],
                             P["g_q_rope"], tabs, ts, P["q_scale"])
    gates = proj_act(h, P["w_g"], "sigmoid", ts, P["w_g"].shape[1] // 2)
    if cache is None:
        v = proj_gelu_norm(h, P["w_v"], P["g_sg"], ts, BF16)
        o_sg = gelu_spatial_gate(h, P["w_u"], v, P["w_s"], P["b_rows"], ts)
        c_kv, k_rope, k, vv = kv_project(h, P["w_kv"], P["g_kv_a"], P["g_k_rope"], tabs, ts, n_rope,
                                         expand_weights=(P["w_uk"], P["w_uv"], P["g_k_nope"], P["k_shift"]))
        o_mla = attention_prompt(P["bounded"], q, k, vv, tq=min(2048, x.shape[1]),
                                 tk=min(1024, x.shape[1]), td=min(512, x.shape[1]))
        extra = ()
    else:
        cache_lat, cache_kr, layer, B, T = cache
        u = proj_act(h, P["w_u"], "gelu", ts, P["w_u"].shape[1])
        v = proj_gelu_norm(h, P["w_v"], P["g_sg"], ts, F32)
        W = v.shape[-1]
        o_sg = spatial_gate_open(u.reshape(B, T, W), v.reshape(B, T, W), P["w_lanes"][:T, :T],
                                 P["b_rows"][:T]).reshape(1, B * T, W)
        c_kv, k_rope, kr_pad = kv_project(h, P["w_kv"], P["g_kv_a"], P["g_k_rope"], tabs, ts, n_rope)
        o_mla = attention_cached(q, cache_lat, cache_kr, layer, c_kv, kr_pad, P["w_uk_t"], P["w_uv"],
                                 P["g_k_nope"], T)
        extra = (v,)
    m = merge_branches(o_sg, o_mla, gates, P["w_pa"], P["w_pb"], ts, P["w_pa"].shape[1] // 2)
    ts_res = min(ts, 512)
    x1, h2 = out_project(m, P["w_o"], x, g1, P["g_norm2"], sc2, sh2, ts_res)
    y = mlp_residual(h2, P["w_up"], P["w_down"], x1, g2, ts_res, min(2048, P["w_up"].shape[1]))
    return (y, c_kv, k_rope) + extra


def kernel(x_prompt, x_sample, cache_kv_latent, cache_k_rope, c_prompt, c_sample, w_ada, b_ada, g_norm1, g_norm2, w_in, g_sg, w_s, b_s, g_q_a, w_uq, g_q_nope, g_q_rope, g_kv_a, g_k_rope, w_uk, g_k_nope, w_uv, w_pa, w_pb, w_o, w_up, w_down):
    depth = w_in.shape[0]
    Bp, S, D = x_prompt.shape
    Bs, T, _ = x_sample.shape
    past = cache_kv_latent.shape[2]
    q_lora = g_q_a.shape[-1]
    lat = g_kv_a.shape[-1]
    n_nope = g_q_nope.shape[-1]
    n_rope = g_q_rope.shape[-1]
    H = w_uk.shape[2]
    sg_w = g_sg.shape[-1]
    off_q = 2 * sg_w
    off_kv = off_q + q_lora
    off_gate = off_kv + lat + n_rope
    assert n_nope == LANES and w_uv.shape[-1] == LANES and 2 * n_rope == LANES
    assert sg_w // SG_GROUPS == LANES and H == N_HEADS

    nb = Bp + Bs
    nb_pad = -(-nb // 8) * 8
    c_all = jnp.concatenate([c_prompt, c_sample, jnp.zeros((nb_pad - nb, D), F32)], axis=0)

    y_p, y_s = x_prompt, x_sample.reshape(1, Bs * T, D)
    outs = [[] for _ in range(5)]
    for l in range(depth):
        def w_in_cols(a, b):
            return w_in[l][:, a:b].astype(BF16)
        w_uq_l = w_uq[l].astype(BF16).reshape(q_lora, H, n_nope + n_rope)
        w_uq_pad = jnp.concatenate([w_uq_l, w_uq_l[:, :, n_nope:]], axis=2)
        q_scale = float((n_nope + n_rope) ** -0.5 * LOG2E)
        def sq_norm_bound(g_nope, g_rope):
            return n_nope * jnp.max(jnp.square(g_nope)) + n_rope * jnp.max(jnp.square(g_rope))
        score_bound = q_scale * jnp.sqrt(sq_norm_bound(g_q_nope[l], g_q_rope[l])
                                         * sq_norm_bound(g_k_nope[l], g_k_rope[l]))
        bounded = (score_bound <= BOUND_MAX).astype(jnp.int32).reshape(1)
        k_shift = jnp.where(jnp.arange(LANES) == LANES // 2, -score_bound, 0.0).astype(F32).reshape(1, LANES)
        P = {
            "n_rope": n_rope,
            "q_scale": q_scale, "k_shift": k_shift, "bounded": bounded,
            "g_norm1": g_norm1[l], "g_norm2": g_norm2[l], "g_sg": g_sg[l],
            "w_u": w_in_cols(0, sg_w), "w_v": w_in_cols(sg_w, off_q), "w_q": w_in_cols(off_q, off_kv),
            "w_kv": jnp.concatenate([w_in_cols(off_kv, off_gate), w_in_cols(off_kv + lat, off_gate)], axis=1),
            "w_g": w_in_cols(off_gate, w_in.shape[2]),
            "g_q_a": g_q_a[l], "w_uq": w_uq_pad.reshape(q_lora, H * HEAD_PAD),
            "g_q_nope": g_q_nope[l], "g_q_rope": _dup_lanes(g_q_rope[l]),
            "g_kv_a": g_kv_a[l], "g_k_rope": _dup_lanes(g_k_rope[l]),
            "w_uk": w_uk[l].astype(BF16).reshape(lat, H * n_nope),
            "w_uk_t": w_uk[l].astype(BF16).reshape(lat, H * n_nope).T,
            "w_uv": w_uv[l].astype(BF16).reshape(lat, H * LANES),
            "g_k_nope": g_k_nope[l],
            "w_s": w_s[l],
            "b_rows": jnp.repeat(b_s[l].T, LANES, axis=1),
            "w_lanes": jnp.repeat(w_s[l][:, :T, :T].transpose(1, 2, 0), LANES, axis=2),
            "w_pa": w_pa[l].astype(BF16), "w_pb": w_pb[l].astype(BF16), "w_o": w_o[l].astype(BF16),
            "w_up": w_up[l].astype(BF16), "w_down": w_down[l].astype(BF16),
        }
        mod = ada_project(c_all, w_ada[l], b_ada[l])
        mods_p = [a.reshape(Bp, 1, D) for a in jnp.split(mod[:Bp], 6, axis=-1)]
        mods_s = [jnp.repeat(a, T, axis=0).reshape(1, Bs * T, D)
                  for a in jnp.split(mod[Bp:nb], 6, axis=-1)]

        ts_p = min(1024, S)
        y_p, lp, kp = _layer(y_p, mods_p, jnp.arange(S), P, ts_p)
        pos_s = jnp.tile(past + jnp.arange(T), Bs)
        y_s, ls, ks, vs = _layer(y_s, mods_s, pos_s, P, Bs * T,
                                 cache=(cache_kv_latent, cache_k_rope, l, Bs, T))
        for lst, a in zip(outs, (lp, kp, ls.reshape(Bs, T, lat), ks.reshape(Bs, T, n_rope),
                                 vs.reshape(Bs, T, sg_w))):
            lst.append(a)
    return (y_p, y_s.reshape(Bs, T, D)) + tuple(o[0][None] if depth == 1 else jnp.stack(o) for o in outs)
```
